```python
import jax, jax.numpy as jnp
from jax import lax
import numpy as np

D_MODEL = 1024
BATCH = 4
SEQ = 8192
DEPTH = 1

CHUNK = 64
PLE_DIM = 256
HG_HEADS = 4
HG_DK = 128
HG_DV = (D_MODEL // 2) // HG_HEADS
HG_WIDTH = HG_HEADS * HG_DV
ML_HEADS = 4
ML_DV = (D_MODEL // 2) // ML_HEADS
ML_DQK = ML_DV // 2
ML_WIDTH = ML_HEADS * ML_DV
MIX_WIDTH = HG_WIDTH + ML_WIDTH
CONV_K = 4
D_FF = ((8 * D_MODEL + 3 * 256 - 1) // (3 * 256)) * 256
ALPHA = float((2 * DEPTH) ** 0.25)
BETA = float((8 * DEPTH) ** -0.25)
LN_EPS = 1e-5
RMS_EPS = 1e-6
ML_I_BIAS = -2.0
ML_F_BIAS_LO = 3.0
ML_F_BIAS_HI = 6.0
PROJ_SIZES = (
    HG_HEADS * HG_DK,
    HG_HEADS * HG_DK,
    HG_WIDTH,
    HG_WIDTH,
    ML_HEADS * ML_DQK,
    ML_HEADS * ML_DQK,
    ML_WIDTH,
    ML_WIDTH,
    ML_HEADS,
    ML_HEADS,
)
PROJ_WIDTH = sum(PROJ_SIZES)

kernel_name = "hymba_hgrn2_mlstm_deepnorm_block"


def _split_cols(u):
    idx, acc = [], 0
    for s in PROJ_SIZES[:-1]:
        acc += s
        idx.append(acc)
    return jnp.split(u, idx, axis=-1)


def layer_norm(x, g, b):
    xf = x.astype(jnp.float32)
    mu = jnp.mean(xf, -1, keepdims=True)
    var = jnp.mean(jnp.square(xf - mu), -1, keepdims=True)
    return ((xf - mu) * lax.rsqrt(var + LN_EPS)).astype(x.dtype) * g + b


def head_rms_norm(h, g):
    hf = h.astype(jnp.float32)
    hf = hf * lax.rsqrt(jnp.mean(hf * hf, -1, keepdims=True) + RMS_EPS)
    B, S, H, Dh = h.shape
    return hf.reshape(B, S, H * Dh).astype(g.dtype) * g


def causal_conv(x, w, b):
    S = x.shape[1]
    xp = jnp.pad(x, ((0, 0), (CONV_K - 1, 0), (0, 0)))
    out = xp[:, 0:S] * w[0]
    for k in range(1, CONV_K):
        out = out + xp[:, k:k + S] * w[k]
    return out + b


def to_chunks(t):
    B, S, H, D = t.shape
    return t.reshape(B, S // CHUNK, CHUNK, H, D).transpose(1, 0, 3, 2, 4)


def gate_chunks(t):
    B, S, H = t.shape
    return t.reshape(B, S // CHUNK, CHUNK, H).transpose(1, 0, 3, 2)


def from_chunks(t):
    N, B, H, C, D = t.shape
    return t.transpose(1, 0, 3, 2, 4).reshape(B, N * C, H, D)


def hgrn2_mixer(q, log_f, k, v):
    B, S, H, DK = q.shape
    DV = v.shape[-1]
    mask = jnp.tril(jnp.ones((CHUNK, CHUNK), dtype=bool))[:, :, None]

    def step(state, inp):
        q_, g_, k_, v_ = inp
        b = jnp.cumsum(g_, axis=2)
        diff = b[:, :, :, None, :] - b[:, :, None, :, :]
        decay = jnp.exp(jnp.where(mask, diff, -jnp.inf))
        scores = jnp.einsum('bhtd,bhsd,bhtsd->bhts', q_, k_, decay)
        o_intra = jnp.einsum('bhts,bhsv->bhtv', scores, v_)
        o_inter = jnp.einsum('bhtd,bhdv->bhtv', q_ * jnp.exp(b), state)
        b_last = b[:, :, -1:, :]
        k_dec = k_ * jnp.exp(b_last - b)
        new_state = jnp.exp(b_last[:, :, 0, :])[..., None] * state + jnp.einsum('bhsd,bhsv->bhdv', k_dec, v_)
        return new_state, o_intra + o_inter

    state0 = jnp.zeros((B, H, DK, DV), jnp.float32)
    _, o = lax.scan(step, state0, (to_chunks(q), to_chunks(log_f), to_chunks(k), to_chunks(v)))
    return from_chunks(o).astype(v.dtype)


def mlstm_mixer(q, k, v, i_pre, log_f):
    B, S, H, DQK = q.shape
    DV = v.shape[-1]
    q = q * (DQK ** -0.5)
    mask = jnp.tril(jnp.ones((CHUNK, CHUNK), dtype=bool))

    def step(carry, inp):
        C_st, n_st, m_st = carry
        q_, k_, v_, ig, lf = inp
        g = jnp.cumsum(lf, axis=-1)
        dmat = g[..., :, None] - g[..., None, :] + ig[..., None, :]
        dmat = jnp.where(mask, dmat, -jnp.inf)
        m_inter = g + m_st[..., None]
        m_t = jnp.maximum(m_inter, jnp.max(dmat, -1))
        w_intra = jnp.exp(dmat - m_t[..., None])
        w_inter = jnp.exp(m_inter - m_t)
        qk = jnp.einsum('bhtd,bhsd->bhts', q_, k_) * w_intra
        num = jnp.einsum('bhts,bhsv->bhtv', qk, v_) + w_inter[..., None] * jnp.einsum('bhtd,bhdv->bhtv', q_, C_st)
        den = jnp.sum(qk, -1) + w_inter * jnp.einsum('bhtd,bhd->bht', q_, n_st)
        h = num / jnp.maximum(jnp.abs(den), jnp.exp(-m_t))[..., None]
        g_last = g[..., -1]
        a = g_last[..., None] - g + ig
        m_new = jnp.maximum(g_last + m_st, jnp.max(a, -1))
        ws = jnp.exp(a - m_new[..., None])
        w_old = jnp.exp(g_last + m_st - m_new)
        C_new = w_old[..., None, None] * C_st + jnp.einsum('bhs,bhsd,bhsv->bhdv', ws, k_, v_)
        n_new = w_old[..., None] * n_st + jnp.einsum('bhs,bhsd->bhd', ws, k_)
        return (C_new, n_new, m_new), h

    carry0 = (jnp.zeros((B, H, DQK, DV), jnp.float32),
              jnp.zeros((B, H, DQK), jnp.float32),
              jnp.zeros((B, H), jnp.float32))
    _, h = lax.scan(step, carry0, (to_chunks(q), to_chunks(k), to_chunks(v),
                                   gate_chunks(i_pre), gate_chunks(log_f)))
    return from_chunks(h).astype(v.dtype)


def setup_inputs(seed: int = 0) -> dict:
    key = jax.random.key(seed)
    ks = jax.random.split(key, 20)
    f32 = jnp.float32
    nrm = lambda k, shape, scale: jax.random.normal(k, shape, f32) * scale
    x = nrm(ks[0], (BATCH, SEQ, D_MODEL), 1.0)
    p = nrm(ks[1], (DEPTH, BATCH, SEQ, PLE_DIM), 1.0)
    w_in = nrm(ks[2], (DEPTH, D_MODEL, PROJ_WIDTH), D_MODEL ** -0.5)
    b_in = nrm(ks[3], (DEPTH, PROJ_WIDTH), 0.02)
    ig_off = PROJ_WIDTH - 2 * ML_HEADS
    fg_off = PROJ_WIDTH - ML_HEADS
    b_in = b_in.at[:, ig_off:fg_off].add(ML_I_BIAS)
    b_in = b_in.at[:, fg_off:].add(jnp.linspace(ML_F_BIAS_LO, ML_F_BIAS_HI, ML_HEADS, dtype=f32))
    hg_lb_logits = nrm(ks[4], (DEPTH + 1, HG_HEADS * HG_DK), 0.5)
    ml_conv_w = nrm(ks[5], (DEPTH, CONV_K, 2 * ML_HEADS * ML_DQK), CONV_K ** -0.5)
    ml_conv_b = nrm(ks[6], (DEPTH, 2 * ML_HEADS * ML_DQK), 0.02)
    hg_norm_g = 1.0 + nrm(ks[7], (DEPTH, HG_WIDTH), 0.02)
    ml_norm_g = 1.0 + nrm(ks[8], (DEPTH, ML_WIDTH), 0.02)
    w_out = nrm(ks[9], (DEPTH, MIX_WIDTH, D_MODEL), BETA * MIX_WIDTH ** -0.5)
    ln1_g = 1.0 + nrm(ks[10], (DEPTH, D_MODEL), 0.02)
    ln1_b = nrm(ks[11], (DEPTH, D_MODEL), 0.02)
    w_ffn_gate = nrm(ks[12], (DEPTH, D_MODEL, D_FF), D_MODEL ** -0.5)
    w_ffn_up = nrm(ks[13], (DEPTH, D_MODEL, D_FF), D_MODEL ** -0.5)
    w_ffn_down = nrm(ks[14], (DEPTH, D_FF, D_MODEL), BETA * D_FF ** -0.5)
    ln2_g = 1.0 + nrm(ks[15], (DEPTH, D_MODEL), 0.02)
    ln2_b = nrm(ks[16], (DEPTH, D_MODEL), 0.02)
    ple_w_proj = nrm(ks[17], (DEPTH, PLE_DIM, D_MODEL), PLE_DIM ** -0.5)
    ple_w_gate = nrm(ks[18], (DEPTH, D_MODEL, D_MODEL), D_MODEL ** -0.5)
    ple_b_gate = nrm(ks[19], (DEPTH, D_MODEL), 0.02)
    return {"x": x, "p": p, "w_in": w_in, "b_in": b_in, "hg_lb_logits": hg_lb_logits,
            "ml_conv_w": ml_conv_w, "ml_conv_b": ml_conv_b, "hg_norm_g": hg_norm_g,
            "ml_norm_g": ml_norm_g, "w_out": w_out, "ln1_g": ln1_g, "ln1_b": ln1_b,
            "w_ffn_gate": w_ffn_gate, "w_ffn_up": w_ffn_up, "w_ffn_down": w_ffn_down,
            "ln2_g": ln2_g, "ln2_b": ln2_b, "ple_w_proj": ple_w_proj,
            "ple_w_gate": ple_w_gate, "ple_b_gate": ple_b_gate}


def reference(x, p, w_in, b_in, hg_lb_logits, ml_conv_w, ml_conv_b, hg_norm_g, ml_norm_g,
              w_out, ln1_g, ln1_b, w_ffn_gate, w_ffn_up, w_ffn_down, ln2_g, ln2_b,
              ple_w_proj, ple_w_gate, ple_b_gate):
    B, S, _ = x.shape
    lower_bounds = jnp.cumsum(jax.nn.softmax(hg_lb_logits.astype(jnp.float32), axis=0), axis=0)
    for i in range(DEPTH):
        u = x @ w_in[i] + b_in[i]
        hq, hf, hv, hgate, mq, mk, mv, mo, mig, mfg = _split_cols(u)

        lb = lower_bounds[i]
        log_f = jnp.logaddexp(jnp.log(lb), jnp.log1p(-lb) + jax.nn.log_sigmoid(hf.astype(jnp.float32)))
        k_hg = -jnp.expm1(log_f)
        o_hg = hgrn2_mixer(jax.nn.silu(hq).reshape(B, S, HG_HEADS, HG_DK),
                           log_f.reshape(B, S, HG_HEADS, HG_DK),
                           k_hg.reshape(B, S, HG_HEADS, HG_DK),
                           hv.reshape(B, S, HG_HEADS, HG_DV))
        o_hg = head_rms_norm(o_hg, hg_norm_g[i]) * jax.nn.silu(hgate)

        qk_c = jax.nn.silu(causal_conv(jnp.concatenate([mq, mk], -1), ml_conv_w[i], ml_conv_b[i]))
        mq_c, mk_c = jnp.split(qk_c, 2, axis=-1)
        h_ml = mlstm_mixer(mq_c.reshape(B, S, ML_HEADS, ML_DQK),
                           mk_c.reshape(B, S, ML_HEADS, ML_DQK),
                           mv.reshape(B, S, ML_HEADS, ML_DV),
                           mig.astype(jnp.float32),
                           jax.nn.log_sigmoid(mfg.astype(jnp.float32)))
        o_ml = head_rms_norm(h_ml, ml_norm_g[i]) * jax.nn.sigmoid(mo)

        mix = jnp.concatenate([o_hg, o_ml], -1) @ w_out[i]
        x = layer_norm(ALPHA * x + mix, ln1_g[i], ln1_b[i])

        ffn = (jax.nn.silu(x @ w_ffn_gate[i]) * (x @ w_ffn_up[i])) @ w_ffn_down[i]
        x = layer_norm(ALPHA * x + ffn, ln2_g[i], ln2_b[i])

        x = x + jax.nn.sigmoid(x @ ple_w_gate[i] + ple_b_gate[i]) * (p[i] @ ple_w_proj[i])
    return x
```

```python
import functools

import jax
import jax.numpy as jnp
from jax import lax
from jax.experimental import pallas as pl
from jax.experimental.pallas import tpu as pltpu

F32 = jnp.float32
BF16 = jnp.bfloat16

CHUNK = 64
SUB = 16
N_SUB = CHUNK // SUB
HG_HEADS = 4
HG_DK = 128
HG_DV = 128
HG_WIDTH = HG_HEADS * HG_DV
ML_HEADS = 4
ML_DQK = 64
ML_DV = 128
ML_WIDTH = ML_HEADS * ML_DV
ML_QK_WIDTH = ML_HEADS * ML_DQK
CONV_K = 4
LN_EPS = 1e-5
RMS_EPS = 1e-6

OFF_HQ = 0
OFF_HF = OFF_HQ + HG_HEADS * HG_DK
OFF_HV = OFF_HF + HG_HEADS * HG_DK
OFF_HG = OFF_HV + HG_WIDTH
OFF_MQ = OFF_HG + HG_WIDTH
OFF_MK = OFF_MQ + ML_QK_WIDTH
OFF_MV = OFF_MK + ML_QK_WIDTH
OFF_MO = OFF_MV + ML_WIDTH
OFF_GATES = OFF_MO + ML_WIDTH
LANE = 128
GATE_ROWS = 16
PROJ_PAD = OFF_GATES + LANE

VMEM_LIMIT = 56 * 1024 * 1024


def _sigmoid(x):
    return 1.0 / (1.0 + jnp.exp(-x))


def _silu(x):
    return x * _sigmoid(x)


def _log_sigmoid(x):
    return jnp.minimum(x, 0.0) - jnp.log(1.0 + jnp.exp(-jnp.abs(x)))


def _split3(x):
    hi = x.astype(BF16)
    r1 = x - hi.astype(F32)
    mid = r1.astype(BF16)
    lo = (r1 - mid.astype(F32)).astype(BF16)
    return hi, mid, lo


def _dot(a, b):
    return jnp.dot(a, b, preferred_element_type=F32)


def _dot_nt(a, b):
    return lax.dot_general(a, b, (((1,), (1,)), ((), ())), preferred_element_type=F32)


def _dot_tn(a, b):
    return lax.dot_general(a, b, (((0,), (0,)), ((), ())), preferred_element_type=F32)


def _rows(blocks):
    return jnp.concatenate(blocks, axis=0)


def _inproj_kernel(x_ref, w_ref, b_ref, wgt_ref, bgt_ref, lbl_ref,
                   logf_ref, hq_ref, hk_ref, hv_ref, hg_ref,
                   mq_ref, mk_ref, mv_ref, mo_ref, gcol_ref, grow_ref):
    xb = x_ref[...].astype(BF16)

    def proj(lo, width):
        return _dot(xb, w_ref[:, lo:lo + width]) + b_ref[:, lo:lo + width]

    hq_ref[...] = _silu(proj(OFF_HQ, HG_WIDTH)).astype(BF16)

    logits = lbl_ref[...]
    mx = jnp.max(logits, axis=0, keepdims=True)
    ex = jnp.exp(logits - mx)
    den = jnp.sum(ex, axis=0, keepdims=True)
    lb = ex[0:1, :] / den
    one_m_lb = (den - ex[0:1, :]) / den
    sig = _sigmoid(proj(OFF_HF, HG_WIDTH))
    logf_ref[...] = jnp.log(lb + one_m_lb * sig)
    hk_ref[...] = (one_m_lb * (1.0 - sig)).astype(BF16)

    hv_ref[...] = proj(OFF_HV, HG_WIDTH).astype(BF16)
    hg_ref[...] = _silu(proj(OFF_HG, HG_WIDTH)).astype(BF16)
    mq_ref[...] = proj(OFF_MQ, ML_QK_WIDTH).astype(BF16)
    mk_ref[...] = proj(OFF_MK, ML_QK_WIDTH).astype(BF16)
    mv_ref[...] = proj(OFF_MV, ML_WIDTH).astype(BF16)
    mo_ref[...] = _sigmoid(proj(OFF_MO, ML_WIDTH)).astype(BF16)

    g = proj(OFF_GATES, LANE)
    lane = lax.broadcasted_iota(jnp.int32, g.shape, 1)
    gcol_ref[...] = jnp.where(lane < ML_HEADS, g, _log_sigmoid(g))
    gt = _dot_nt(wgt_ref[...], xb) + bgt_ref[...]
    sub = lax.broadcasted_iota(jnp.int32, gt.shape, 0)
    grow_ref[...] = jnp.where(sub < ML_HEADS, gt, _log_sigmoid(gt))


def _inproj(x2, w_pad, b_pad, wgt, bgt, lb_logits, tm):
    t, d = x2.shape
    grid = (t // tm,)
    row = lambda i: (i, 0)
    const = lambda i: (0, 0)
    out_shapes = (
        jax.ShapeDtypeStruct((t, HG_WIDTH), F32),
        jax.ShapeDtypeStruct((t, HG_WIDTH), BF16),
        jax.ShapeDtypeStruct((t, HG_WIDTH), BF16),
        jax.ShapeDtypeStruct((t, HG_WIDTH), BF16),
        jax.ShapeDtypeStruct((t, HG_WIDTH), BF16),
        jax.ShapeDtypeStruct((t, ML_QK_WIDTH), BF16),
        jax.ShapeDtypeStruct((t, ML_QK_WIDTH), BF16),
        jax.ShapeDtypeStruct((t, ML_WIDTH), BF16),
        jax.ShapeDtypeStruct((t, ML_WIDTH), BF16),
        jax.ShapeDtypeStruct((t, LANE), F32),
        jax.ShapeDtypeStruct((GATE_ROWS, t), F32),
    )
    out_specs = (
        pl.BlockSpec((tm, HG_WIDTH), row), pl.BlockSpec((tm, HG_WIDTH), row),
        pl.BlockSpec((tm, HG_WIDTH), row), pl.BlockSpec((tm, HG_WIDTH), row),
        pl.BlockSpec((tm, HG_WIDTH), row), pl.BlockSpec((tm, ML_QK_WIDTH), row),
        pl.BlockSpec((tm, ML_QK_WIDTH), row), pl.BlockSpec((tm, ML_WIDTH), row),
        pl.BlockSpec((tm, ML_WIDTH), row), pl.BlockSpec((tm, LANE), row),
        pl.BlockSpec((GATE_ROWS, tm), lambda i: (0, i)),
    )
    in_specs = [
        pl.BlockSpec((tm, d), row),
        pl.BlockSpec(w_pad.shape, const),
        pl.BlockSpec(b_pad.shape, const),
        pl.BlockSpec(wgt.shape, const),
        pl.BlockSpec(bgt.shape, const),
        pl.BlockSpec(lb_logits.shape, const),
    ]
    return pl.pallas_call(
        _inproj_kernel, grid=grid, in_specs=in_specs, out_specs=out_specs, out_shape=out_shapes,
        compiler_params=pltpu.CompilerParams(dimension_semantics=("arbitrary",),
                                             vmem_limit_bytes=VMEM_LIMIT),
        name="inproj",
    )(x2, w_pad, b_pad, wgt, bgt, lb_logits)


def _cumsum_rows(tril_bf, x):
    hi, mid, lo = _split3(x)
    return _dot(tril_bf, hi) + _dot(tril_bf, mid) + _dot(tril_bf, lo)


def _head_rms(o, width):
    ms = jnp.sum(o * o, axis=-1, keepdims=True) * (1.0 / width)
    return o * lax.rsqrt(ms + RMS_EPS)


def _hgrn2_kernel(logf_ref, q_ref, k_ref, v_ref, og_ref, gn_ref, o_ref,
                  st_ref, b_scr, k_scr, *, n_chunks):
    @pl.when(pl.program_id(1) == 0)
    def _():
        st_ref[...] = jnp.zeros_like(st_ref)

    row = lax.broadcasted_iota(jnp.int32, (CHUNK, CHUNK), 0)
    col = lax.broadcasted_iota(jnp.int32, (CHUNK, CHUNK), 1)
    causal = col <= row
    tril_bf = causal.astype(BF16)
    row_blk = row // SUB
    rel = col - row_blk * SUB
    diag_mask = causal & (row_blk == col // SUB)

    def bcast_rows(r):
        return jnp.broadcast_to(r, (SUB, r.shape[1]))

    def chunk(c, carry):
        r0 = pl.multiple_of(c * CHUNK, CHUNK)
        rows = pl.ds(r0, CHUNK)
        b = _cumsum_rows(tril_bf, logf_ref[rows, :])
        qf = q_ref[rows, :].astype(F32)
        kf = k_ref[rows, :].astype(F32)
        v = v_ref[rows, :]
        b_scr[...] = b
        k_scr[...] = kf

        beta = [jnp.zeros((1, HG_WIDTH), F32)]
        beta += [b_scr[pl.ds(SUB * m - 1, 1), :] for m in range(1, N_SUB + 1)]
        beta_lo = _rows([bcast_rows(beta[m]) for m in range(N_SUB)])
        beta_hi = _rows([bcast_rows(beta[m + 1]) for m in range(N_SUB)])
        q_st = qf * jnp.exp(b - beta_lo)
        k_end = kf * jnp.exp(beta_hi - b)

        def boundary_decay(i, j):
            return jnp.exp(beta[i] - beta[j])

        q_in = q_st * _rows([bcast_rows(boundary_decay(m, 0)) for m in range(N_SUB)])
        k_dec = k_end * _rows([bcast_rows(boundary_decay(N_SUB, m + 1)) for m in range(N_SUB)])
        chunk_decay = boundary_decay(N_SUB, 0)

        outs = []
        for h in range(HG_HEADS):
            hs = slice(h * HG_DK, (h + 1) * HG_DK)
            b_h = b[:, hs]
            q_h = qf[:, hs]
            d_acc = jnp.zeros((CHUNK, CHUNK), F32)
            for j in range(SUB):
                b_j = _rows([bcast_rows(b_scr[pl.ds(m * SUB + j, 1), hs]) for m in range(N_SUB)])
                k_j = _rows([bcast_rows(k_scr[pl.ds(m * SUB + j, 1), hs]) for m in range(N_SUB)])
                p = q_h * k_j * jnp.exp(jnp.minimum(b_h - b_j, 0.0))
                d_acc = jnp.where(rel == j, jnp.sum(p, axis=-1, keepdims=True), d_acc)
            zero = jnp.zeros((SUB, HG_DK), F32)
            off = [jnp.zeros((SUB, CHUNK), F32)]
            for i in range(1, N_SUB):
                k_i = _rows([
                    k_end[j * SUB:(j + 1) * SUB, hs] * boundary_decay(i, j + 1)[:, hs] if j < i else zero
                    for j in range(N_SUB)])
                off.append(_dot_nt(q_st[i * SUB:(i + 1) * SUB, hs].astype(BF16), k_i.astype(BF16)))
            a = (jnp.where(diag_mask, d_acc, 0.0) + _rows(off)).astype(BF16)
            st = st_ref[h]
            o_h = _dot(a, v[:, hs]) + _dot_nt(q_in[:, hs].astype(BF16), st.astype(BF16))
            st_ref[h] = st * chunk_decay[:, hs] + _dot_tn(v[:, hs], k_dec[:, hs].astype(BF16))
            outs.append(_head_rms(o_h, HG_DV))
        o = jnp.concatenate(outs, axis=-1)
        o_ref[rows, :] = (o * gn_ref[...] * og_ref[rows, :].astype(F32)).astype(o_ref.dtype)
        return carry

    lax.fori_loop(0, n_chunks, chunk, 0)


def _hgrn2(logf, q, k, v, og, gn, batch, seq, ts):
    t = logf.shape[0]
    steps = seq // ts
    grid = (batch, steps)
    row = lambda b, s: (b * steps + s, 0)
    spec = pl.BlockSpec((ts, HG_WIDTH), row)
    return pl.pallas_call(
        functools.partial(_hgrn2_kernel, n_chunks=ts // CHUNK),
        grid=grid,
        in_specs=[spec, spec, spec, spec, spec, pl.BlockSpec((1, HG_WIDTH), lambda b, s: (0, 0))],
        out_specs=spec,
        out_shape=jax.ShapeDtypeStruct((t, HG_WIDTH), BF16),
        scratch_shapes=[pltpu.VMEM((HG_HEADS, HG_DV, HG_DK), F32),
                        pltpu.VMEM((CHUNK, HG_WIDTH), F32),
                        pltpu.VMEM((CHUNK, HG_WIDTH), F32)],
        compiler_params=pltpu.CompilerParams(dimension_semantics=("arbitrary", "arbitrary"),
                                             vmem_limit_bytes=VMEM_LIMIT),
        name="hgrn2",
    )(logf, q, k, v, og, gn)


CONV_PAD = 8


def _mlstm_kernel(mq_ref, mk_ref, mv_ref, og_ref, gcol_ref, grow_ref, cw_ref, cb_ref, gn_ref, o_ref,
                  cbuf, qk_scr, c_ref, n_ref, m_ref, *, n_chunks):
    ts = n_chunks * CHUNK

    @pl.when(pl.program_id(1) == 0)
    def _():
        cbuf[0:CONV_PAD, :] = jnp.zeros((CONV_PAD, 2 * ML_QK_WIDTH), F32)
        c_ref[...] = jnp.zeros_like(c_ref)
        n_ref[...] = jnp.zeros_like(n_ref)
        m_ref[...] = jnp.zeros_like(m_ref)

    cbuf[CONV_PAD:CONV_PAD + ts, 0:ML_QK_WIDTH] = mq_ref[...].astype(F32)
    cbuf[CONV_PAD:CONV_PAD + ts, ML_QK_WIDTH:] = mk_ref[...].astype(F32)
    acc = cb_ref[...] + cw_ref[0:1, :] * cbuf[pl.ds(CONV_PAD - (CONV_K - 1), ts), :]
    for tap in range(1, CONV_K):
        acc = acc + cw_ref[tap:tap + 1, :] * cbuf[pl.ds(CONV_PAD - (CONV_K - 1) + tap, ts), :]
    qk_scr[...] = _silu(acc)
    cbuf[0:CONV_PAD, :] = cbuf[ts:ts + CONV_PAD, :]

    row = lax.broadcasted_iota(jnp.int32, (CHUNK, CHUNK), 0)
    col = lax.broadcasted_iota(jnp.int32, (CHUNK, CHUNK), 1)
    causal = col <= row
    tril_bf = causal.astype(BF16)
    triu_bf = (row <= col).astype(BF16)
    q_scale = ML_DQK ** -0.5

    def chunk(c, carry):
        r0 = pl.multiple_of(c * CHUNK, CHUNK)
        rows = pl.ds(r0, CHUNK)
        gcol = gcol_ref[rows, :]
        grow = grow_ref[c]
        gc_all = _cumsum_rows(tril_bf, gcol)
        hi, mid, lo = _split3(grow)
        gr_all = _dot(hi, triu_bf) + _dot(mid, triu_bf) + _dot(lo, triu_bf)
        v = mv_ref[rows, :]
        outs = []
        for h in range(ML_HEADS):
            qs = slice(h * ML_DQK, (h + 1) * ML_DQK)
            ks = slice(ML_QK_WIDTH + h * ML_DQK, ML_QK_WIDTH + (h + 1) * ML_DQK)
            vs = slice(h * ML_DV, (h + 1) * ML_DV)
            q = qk_scr[rows, qs] * q_scale
            k = qk_scr[rows, ks]
            g_c = gc_all[:, ML_HEADS + h:ML_HEADS + h + 1]
            i_c = gcol[:, h:h + 1]
            g_r = gr_all[ML_HEADS + h:ML_HEADS + h + 1, :]
            i_r = grow[h:h + 1, :]
            g_last = g_c[CHUNK - 1:CHUNK, :]
            m_st = m_ref[h:h + 1, 0:1]
            c_st = c_ref[h]
            n_st = n_ref[h:h + 1, :]

            dmat = jnp.where(causal, g_c - g_r + i_r, -jnp.inf)
            m_inter = g_c + m_st
            m_t = jnp.maximum(m_inter, jnp.max(dmat, axis=-1, keepdims=True))
            w_intra = jnp.exp(dmat - m_t)
            w_inter = jnp.exp(m_inter - m_t)
            qb = q.astype(BF16)
            qk = _dot_nt(qb, k.astype(BF16)) * w_intra
            num = _dot(qk.astype(BF16), v[:, vs]) + w_inter * _dot(qb, c_st.astype(BF16))
            den = jnp.sum(qk, axis=-1, keepdims=True) + w_inter * jnp.sum(q * n_st, axis=-1, keepdims=True)
            hh = num / jnp.maximum(jnp.abs(den), jnp.exp(-m_t))

            a = g_last - g_c + i_c
            m_new = jnp.maximum(g_last + m_st, jnp.max(a, axis=0, keepdims=True))
            ws = jnp.exp(a - m_new)
            w_old = jnp.exp(g_last + m_st - m_new)
            kw = k * ws
            c_ref[h] = w_old * c_st + _dot_tn(kw.astype(BF16), v[:, vs])
            n_ref[h:h + 1, :] = w_old * n_st + jnp.sum(kw, axis=0, keepdims=True)
            m_ref[h:h + 1, :] = jnp.broadcast_to(m_new, (1, LANE))
            outs.append(_head_rms(hh, ML_DV))
        o = jnp.concatenate(outs, axis=-1)
        o_ref[rows, :] = (o * gn_ref[...] * og_ref[rows, :].astype(F32)).astype(o_ref.dtype)
        return carry

    lax.fori_loop(0, n_chunks, chunk, 0)


def _mlstm(mq, mk, mv, og, gcol, grow3, conv_w, conv_b, gn, batch, seq, ts):
    t = mq.shape[0]
    steps = seq // ts
    n_chunks = ts // CHUNK
    grid = (batch, steps)
    row = lambda b, s: (b * steps + s, 0)
    const = lambda b, s: (0, 0)
    return pl.pallas_call(
        functools.partial(_mlstm_kernel, n_chunks=n_chunks),
        grid=grid,
        in_specs=[pl.BlockSpec((ts, ML_QK_WIDTH), row), pl.BlockSpec((ts, ML_QK_WIDTH), row),
                  pl.BlockSpec((ts, ML_WIDTH), row), pl.BlockSpec((ts, ML_WIDTH), row),
                  pl.BlockSpec((ts, LANE), row),
                  pl.BlockSpec((n_chunks, GATE_ROWS, CHUNK), lambda b, s: (b * steps + s, 0, 0)),
                  pl.BlockSpec(conv_w.shape, const), pl.BlockSpec(conv_b.shape, const),
                  pl.BlockSpec((1, ML_WIDTH), const)],
        out_specs=pl.BlockSpec((ts, ML_WIDTH), row),
        out_shape=jax.ShapeDtypeStruct((t, ML_WIDTH), BF16),
        scratch_shapes=[pltpu.VMEM((ts + CONV_PAD, 2 * ML_QK_WIDTH), F32),
                        pltpu.VMEM((ts, 2 * ML_QK_WIDTH), F32),
                        pltpu.VMEM((ML_HEADS, ML_DQK, ML_DV), F32),
                        pltpu.VMEM((8, ML_DQK), F32),
                        pltpu.VMEM((8, LANE), F32)],
        compiler_params=pltpu.CompilerParams(dimension_semantics=("arbitrary", "arbitrary"),
                                             vmem_limit_bytes=VMEM_LIMIT),
        name="mlstm",
    )(mq, mk, mv, og, gcol, grow3, conv_w, conv_b, gn)


FF_TILE = 256


def _layer_norm(x, g, b):
    mu = jnp.mean(x, axis=-1, keepdims=True)
    xc = x - mu
    var = jnp.mean(xc * xc, axis=-1, keepdims=True)
    return xc * lax.rsqrt(var + LN_EPS) * g + b


def _post_kernel(ohg_ref, oml_ref, x_ref, p_ref, wo_ref, ln1g_ref, ln1b_ref, wg_ref, wu_ref, wd_ref,
                 ln2g_ref, ln2b_ref, wpp_ref, wpg_ref, bpg_ref, out_ref, *, alpha, d_ff):
    mix = _dot(ohg_ref[...], wo_ref[0:HG_WIDTH, :]) + _dot(oml_ref[...], wo_ref[HG_WIDTH:, :])
    h1 = _layer_norm(alpha * x_ref[...] + mix, ln1g_ref[...], ln1b_ref[...])
    h1b = h1.astype(BF16)
    ffn = jnp.zeros(h1.shape, F32)
    for lo in range(0, d_ff, FF_TILE):
        gate = _dot(h1b, wg_ref[:, lo:lo + FF_TILE])
        up = _dot(h1b, wu_ref[:, lo:lo + FF_TILE])
        ffn = ffn + _dot((_silu(gate) * up).astype(BF16), wd_ref[lo:lo + FF_TILE, :])
    h2 = _layer_norm(alpha * h1 + ffn, ln2g_ref[...], ln2b_ref[...])
    pgate = _sigmoid(_dot(h2.astype(BF16), wpg_ref[...]) + bpg_ref[...])
    pemb = _dot(p_ref[...].astype(BF16), wpp_ref[...])
    out_ref[...] = h2 + pgate * pemb


def _post(ohg, oml, x2, p2, wo, ln1g, ln1b, wg, wu, wd, ln2g, ln2b, wpp, wpg, bpg, alpha, tm):
    t, d = x2.shape
    d_ff = wg.shape[1]
    assert d_ff % FF_TILE == 0
    row = lambda i: (i, 0)

    def const(a):
        return pl.BlockSpec(a.shape, lambda i: (0, 0), pipeline_mode=pl.Buffered(1))

    return pl.pallas_call(
        functools.partial(_post_kernel, alpha=alpha, d_ff=d_ff),
        grid=(t // tm,),
        in_specs=[pl.BlockSpec((tm, HG_WIDTH), row), pl.BlockSpec((tm, ML_WIDTH), row),
                  pl.BlockSpec((tm, d), row), pl.BlockSpec((tm, p2.shape[1]), row),
                  const(wo), const(ln1g), const(ln1b), const(wg), const(wu), const(wd),
                  const(ln2g), const(ln2b), const(wpp), const(wpg), const(bpg)],
        out_specs=pl.BlockSpec((tm, d), row),
        out_shape=jax.ShapeDtypeStruct((t, d), F32),
        compiler_params=pltpu.CompilerParams(dimension_semantics=("arbitrary",),
                                             vmem_limit_bytes=VMEM_LIMIT),
        name="post",
    )(ohg, oml, x2, p2, wo, ln1g, ln1b, wg, wu, wd, ln2g, ln2b, wpp, wpg, bpg)


def _pick(n, candidates):
    for c in candidates:
        if n % c == 0:
            return c
    raise ValueError(f"no tile for {n}")


def kernel(x, p, w_in, b_in, hg_lb_logits, ml_conv_w, ml_conv_b, hg_norm_g, ml_norm_g, w_out, ln1_g, ln1_b,
           w_ffn_gate, w_ffn_up, w_ffn_down, ln2_g, ln2_b, ple_w_proj, ple_w_gate, ple_b_gate):
    batch, seq, d = x.shape
    depth = w_in.shape[0]
    t = batch * seq
    alpha = float((2 * depth) ** 0.25)
    tm = _pick(t, (512, 256, 128, 64))
    ts = _pick(seq, (256, 128, 64))
    assert w_in.shape[2] == OFF_GATES + 2 * ML_HEADS

    x2 = x.reshape(t, d)
    for i in range(depth):
        w_i = w_in[i]
        pad = PROJ_PAD - w_i.shape[1]
        w_pad = jnp.pad(w_i, ((0, 0), (0, pad))).astype(BF16)
        b_pad = jnp.pad(b_in[i], (0, pad)).reshape(1, PROJ_PAD)
        wgt = jnp.pad(w_i[:, OFF_GATES:].T, ((0, GATE_ROWS - 2 * ML_HEADS), (0, 0))).astype(BF16)
        bgt = jnp.pad(b_in[i, OFF_GATES:], (0, GATE_ROWS - 2 * ML_HEADS)).reshape(GATE_ROWS, 1)
        assert depth == 1, "lower-bound cumsum is specialised to a single layer"

        (logf, hq, hk, hv, hg, mq, mk, mv, mo, gcol, grow) = _inproj(
            x2, w_pad, b_pad, wgt, bgt, hg_lb_logits, tm)
        grow3 = grow.reshape(GATE_ROWS, t // CHUNK, CHUNK).transpose(1, 0, 2)

        o_hg = _hgrn2(logf, hq, hk, hv, hg, hg_norm_g[i].reshape(1, HG_WIDTH), batch, seq, ts)
        o_ml = _mlstm(mq, mk, mv, mo, gcol, grow3, ml_conv_w[i], ml_conv_b[i].reshape(1, -1),
                      ml_norm_g[i].reshape(1, ML_WIDTH), batch, seq, ts)

        x2 = _post(o_hg, o_ml, x2, p[i].reshape(t, -1),
                   w_out[i].astype(BF16), ln1_g[i].reshape(1, d), ln1_b[i].reshape(1, d),
                   w_ffn_gate[i].astype(BF16), w_ffn_up[i].astype(BF16), w_ffn_down[i].astype(BF16),
                   ln2_g[i].reshape(1, d), ln2_b[i].reshape(1, d),
                   ple_w_proj[i].astype(BF16), ple_w_gate[i].astype(BF16), ple_b_gate[i].reshape(1, d),
                   alpha, tm)
    return x2.reshape(batch, seq, d)
```

```python
import functools

import numpy as np
import jax
import jax.numpy as jnp
from jax import lax
from jax.experimental import pallas as pl
from jax.experimental.pallas import tpu as pltpu

F32 = jnp.float32
BF16 = jnp.bfloat16

CHUNK = 64
SUB = 16
N_SUB = CHUNK // SUB
HG_HEADS = 4
HG_DK = 128
HG_DV = 128
HG_WIDTH = HG_HEADS * HG_DV
ML_HEADS = 4
ML_DQK = 64
ML_DV = 128
ML_WIDTH = ML_HEADS * ML_DV
ML_QK_WIDTH = ML_HEADS * ML_DQK
CONV_K = 4
LN_EPS = 1e-5
RMS_EPS = 1e-6

OFF_HQ = 0
OFF_HF = OFF_HQ + HG_HEADS * HG_DK
OFF_HV = OFF_HF + HG_HEADS * HG_DK
OFF_HG = OFF_HV + HG_WIDTH
OFF_MQ = OFF_HG + HG_WIDTH
OFF_MK = OFF_MQ + ML_QK_WIDTH
OFF_MV = OFF_MK + ML_QK_WIDTH
OFF_MO = OFF_MV + ML_WIDTH
OFF_GATES = OFF_MO + ML_WIDTH
LANE = 128
GATE_TILE = 8
GATE_ROWS = 2 * GATE_TILE

VMEM_LIMIT = 56 * 1024 * 1024


def _sigmoid(x):
    return 1.0 / (1.0 + jnp.exp(-x))


def _silu(x):
    return x * _sigmoid(x)


def _log_sigmoid(x):
    return jnp.minimum(x, 0.0) - jnp.log(1.0 + jnp.exp(-jnp.abs(x)))


def _split3(x):
    hi = x.astype(BF16)
    r1 = x - hi.astype(F32)
    mid = r1.astype(BF16)
    lo = (r1 - mid.astype(F32)).astype(BF16)
    return hi, mid, lo


def _dot(a, b):
    return jnp.dot(a, b, preferred_element_type=F32)


def _dot_nt(a, b):
    return lax.dot_general(a, b, (((1,), (1,)), ((), ())), preferred_element_type=F32)


def _dot_tn(a, b):
    return lax.dot_general(a, b, (((0,), (0,)), ((), ())), preferred_element_type=F32)


def _rows(blocks):
    return jnp.concatenate(blocks, axis=0)


def _inproj_kernel(x_ref, w_ref, b_ref, wgt_ref, bgt_ref, lbl_ref,
                   logf_ref, hq_ref, hk_ref, hv_ref, hg_ref,
                   mq_ref, mk_ref, mv_ref, mo_ref, grow_ref):
    xb = x_ref[...].astype(BF16)

    def proj(lo, width):
        return _dot(xb, w_ref[:, lo:lo + width]) + b_ref[:, lo:lo + width]

    hq_ref[...] = _silu(proj(OFF_HQ, HG_WIDTH)).astype(BF16)

    logits = lbl_ref[...]
    mx = jnp.max(logits, axis=0, keepdims=True)
    ex = jnp.exp(logits - mx)
    den = jnp.sum(ex, axis=0, keepdims=True)
    lb = ex[0:1, :] / den
    one_m_lb = (den - ex[0:1, :]) / den
    sig = _sigmoid(proj(OFF_HF, HG_WIDTH))
    logf_ref[...] = jnp.log(lb + one_m_lb * sig)
    hk_ref[...] = (one_m_lb * (1.0 - sig)).astype(BF16)

    hv_ref[...] = proj(OFF_HV, HG_WIDTH).astype(BF16)
    hg_ref[...] = _silu(proj(OFF_HG, HG_WIDTH)).astype(BF16)
    mq_ref[...] = proj(OFF_MQ, ML_QK_WIDTH).astype(BF16)
    mk_ref[...] = proj(OFF_MK, ML_QK_WIDTH).astype(BF16)
    mv_ref[...] = proj(OFF_MV, ML_WIDTH).astype(BF16)
    mo_ref[...] = _sigmoid(proj(OFF_MO, ML_WIDTH)).astype(BF16)

    gt = _dot_nt(wgt_ref[...], xb) + bgt_ref[...]
    sub = lax.broadcasted_iota(jnp.int32, gt.shape, 0)
    is_fgate = (sub >= GATE_TILE) & (sub < GATE_TILE + ML_HEADS)
    grow_ref[...] = jnp.where(is_fgate, _log_sigmoid(gt), gt)


def _inproj(x2, w_bf, b_row, wgt, bgt, lb_logits, tm):
    t, d = x2.shape
    grid = (t // tm,)
    row = lambda i: (i, 0)
    const = lambda i: (0, 0)
    out_shapes = (
        jax.ShapeDtypeStruct((t, HG_WIDTH), F32),
        jax.ShapeDtypeStruct((t, HG_WIDTH), BF16),
        jax.ShapeDtypeStruct((t, HG_WIDTH), BF16),
        jax.ShapeDtypeStruct((t, HG_WIDTH), BF16),
        jax.ShapeDtypeStruct((t, HG_WIDTH), BF16),
        jax.ShapeDtypeStruct((t, ML_QK_WIDTH), BF16),
        jax.ShapeDtypeStruct((t, ML_QK_WIDTH), BF16),
        jax.ShapeDtypeStruct((t, ML_WIDTH), BF16),
        jax.ShapeDtypeStruct((t, ML_WIDTH), BF16),
        jax.ShapeDtypeStruct((GATE_ROWS, t), F32),
    )
    out_specs = (
        pl.BlockSpec((tm, HG_WIDTH), row), pl.BlockSpec((tm, HG_WIDTH), row),
        pl.BlockSpec((tm, HG_WIDTH), row), pl.BlockSpec((tm, HG_WIDTH), row),
        pl.BlockSpec((tm, HG_WIDTH), row), pl.BlockSpec((tm, ML_QK_WIDTH), row),
        pl.BlockSpec((tm, ML_QK_WIDTH), row), pl.BlockSpec((tm, ML_WIDTH), row),
        pl.BlockSpec((tm, ML_WIDTH), row),
        pl.BlockSpec((GATE_ROWS, tm), lambda i: (0, i)),
    )
    in_specs = [
        pl.BlockSpec((tm, d), row),
        pl.BlockSpec(w_bf.shape, const),
        pl.BlockSpec(b_row.shape, const),
        pl.BlockSpec(wgt.shape, const),
        pl.BlockSpec(bgt.shape, const),
        pl.BlockSpec(lb_logits.shape, const),
    ]
    return pl.pallas_call(
        _inproj_kernel, grid=grid, in_specs=in_specs, out_specs=out_specs, out_shape=out_shapes,
        compiler_params=pltpu.CompilerParams(dimension_semantics=("arbitrary",),
                                             vmem_limit_bytes=VMEM_LIMIT),
        name="inproj",
    )(x2, w_bf, b_row, wgt, bgt, lb_logits)


def _cumsum_rows(tril_bf, x):
    hi, mid, lo = _split3(x)
    return _dot(tril_bf, hi) + _dot(tril_bf, mid) + _dot(tril_bf, lo)


def _head_rms(o, width):
    ms = jnp.sum(o * o, axis=-1, keepdims=True) * (1.0 / width)
    return o * lax.rsqrt(ms + RMS_EPS)


def _hgrn2_kernel(logf_ref, q_ref, k_ref, v_ref, og_ref, gn_ref, o_ref,
                  st_ref, b_scr, k_scr, *, n_chunks):
    @pl.when(pl.program_id(1) == 0)
    def _():
        st_ref[...] = jnp.zeros_like(st_ref)

    row = lax.broadcasted_iota(jnp.int32, (CHUNK, CHUNK), 0)
    col = lax.broadcasted_iota(jnp.int32, (CHUNK, CHUNK), 1)
    causal = col <= row
    tril_bf = causal.astype(BF16)
    row_blk = row // SUB
    rel = col - row_blk * SUB
    diag_mask = causal & (row_blk == col // SUB)

    def bcast_rows(r):
        return jnp.broadcast_to(r, (SUB, r.shape[1]))

    def chunk(c, carry):
        r0 = pl.multiple_of(c * CHUNK, CHUNK)
        rows = pl.ds(r0, CHUNK)
        b = _cumsum_rows(tril_bf, logf_ref[rows, :])
        qf = q_ref[rows, :].astype(F32)
        kf = k_ref[rows, :].astype(F32)
        v = v_ref[rows, :]
        b_scr[...] = b
        k_scr[...] = kf

        beta = [jnp.zeros((1, HG_WIDTH), F32)]
        beta += [b_scr[pl.ds(SUB * m - 1, 1), :] for m in range(1, N_SUB + 1)]
        beta_lo = _rows([bcast_rows(beta[m]) for m in range(N_SUB)])
        beta_hi = _rows([bcast_rows(beta[m + 1]) for m in range(N_SUB)])
        q_st = qf * jnp.exp(b - beta_lo)
        k_end = kf * jnp.exp(beta_hi - b)

        def boundary_decay(i, j):
            return jnp.exp(beta[i] - beta[j])

        q_in = q_st * _rows([bcast_rows(boundary_decay(m, 0)) for m in range(N_SUB)])
        k_dec = k_end * _rows([bcast_rows(boundary_decay(N_SUB, m + 1)) for m in range(N_SUB)])
        chunk_decay = boundary_decay(N_SUB, 0)

        outs = []
        for h in range(HG_HEADS):
            hs = slice(h * HG_DK, (h + 1) * HG_DK)
            b_h = b[:, hs]
            q_h = qf[:, hs]
            d_acc = jnp.zeros((CHUNK, CHUNK), F32)
            for j in range(SUB):
                b_j = _rows([bcast_rows(b_scr[pl.ds(m * SUB + j, 1), hs]) for m in range(N_SUB)])
                k_j = _rows([bcast_rows(k_scr[pl.ds(m * SUB + j, 1), hs]) for m in range(N_SUB)])
                p = q_h * k_j * jnp.exp(jnp.minimum(b_h - b_j, 0.0))
                d_acc = jnp.where(rel == j, jnp.sum(p, axis=-1, keepdims=True), d_acc)
            zero = jnp.zeros((SUB, HG_DK), F32)
            off = [jnp.zeros((SUB, CHUNK), F32)]
            for i in range(1, N_SUB):
                k_i = _rows([
                    k_end[j * SUB:(j + 1) * SUB, hs] * boundary_decay(i, j + 1)[:, hs] if j < i else zero
                    for j in range(N_SUB)])
                off.append(_dot_nt(q_st[i * SUB:(i + 1) * SUB, hs].astype(BF16), k_i.astype(BF16)))
            a = (jnp.where(diag_mask, d_acc, 0.0) + _rows(off)).astype(BF16)
            st = st_ref[h]
            o_h = _dot(a, v[:, hs]) + _dot_nt(q_in[:, hs].astype(BF16), st.astype(BF16))
            st_ref[h] = st * chunk_decay[:, hs] + _dot_tn(v[:, hs], k_dec[:, hs].astype(BF16))
            outs.append(_head_rms(o_h, HG_DV))
        o = jnp.concatenate(outs, axis=-1)
        o_ref[rows, :] = (o * gn_ref[...] * og_ref[rows, :].astype(F32)).astype(o_ref.dtype)
        return carry

    lax.fori_loop(0, n_chunks, chunk, 0)


def _hgrn2(logf, q, k, v, og, gn, batch, seq, ts):
    t = logf.shape[0]
    steps = seq // ts
    grid = (batch, steps)
    row = lambda b, s: (b * steps + s, 0)
    spec = pl.BlockSpec((ts, HG_WIDTH), row)
    return pl.pallas_call(
        functools.partial(_hgrn2_kernel, n_chunks=ts // CHUNK),
        grid=grid,
        in_specs=[spec, spec, spec, spec, spec, pl.BlockSpec((1, HG_WIDTH), lambda b, s: (0, 0))],
        out_specs=spec,
        out_shape=jax.ShapeDtypeStruct((t, HG_WIDTH), BF16),
        scratch_shapes=[pltpu.VMEM((HG_HEADS, HG_DV, HG_DK), F32),
                        pltpu.VMEM((CHUNK, HG_WIDTH), F32),
                        pltpu.VMEM((CHUNK, HG_WIDTH), F32)],
        compiler_params=pltpu.CompilerParams(dimension_semantics=("arbitrary", "arbitrary"),
                                             vmem_limit_bytes=VMEM_LIMIT),
        name="hgrn2",
    )(logf, q, k, v, og, gn)


CONV_PAD = 8
ML_TILE = 256
SEL_ROWS = 128
GRP_M, GRP_WI, GRP_EN, GRP_ONE = 0, 32, 64, 96
ML_AUG = 2 * ML_DV


def _mlstm_constants(tl):
    ident = np.eye(tl, dtype=np.float32)
    triu = np.triu(np.ones((tl, tl), np.float32))
    bias = np.where(np.tril(np.ones((tl, tl), bool)), 0.0, -np.inf).astype(np.float32)
    sel_d = np.zeros((GRP_ONE, ML_HEADS * tl), np.float32)
    sel_w = np.zeros((SEL_ROWS, ML_HEADS * ML_AUG), np.float32)
    for h in range(ML_HEADS):
        for k in range(3):
            sel_d[GRP_M + GATE_TILE * k + h, h * tl:(h + 1) * tl] = -1.0
            sel_w[GRP_WI + GATE_TILE * k + h, h * ML_AUG:h * ML_AUG + ML_DV] = 1.0
            sel_w[GRP_EN + GATE_TILE * k + h, h * ML_AUG + ML_DV:(h + 1) * ML_AUG] = 1.0
    return (jnp.asarray(ident, BF16), jnp.asarray(triu, BF16), jnp.asarray(bias),
            jnp.asarray(sel_d, BF16), jnp.asarray(sel_w, BF16))


def _split3_f32(x):
    return [s.astype(F32) for s in _split3(x)]


def _mlstm_kernel(mq_ref, mk_ref, mv_ref, og_ref, grow_ref, cw_ref, cb_ref, gn_ref,
                  ident_ref, triu_ref, bias_ref, seld_ref, selw_ref, o_ref,
                  cbuf, c_ref, m_ref, *, tl):
    @pl.when(pl.program_id(1) == 0)
    def _():
        cbuf[0:CONV_PAD, :] = jnp.zeros((CONV_PAD, 2 * ML_QK_WIDTH), F32)
        c_ref[...] = jnp.zeros_like(c_ref)
        m_ref[...] = jnp.zeros_like(m_ref)

    cbuf[CONV_PAD:CONV_PAD + tl, 0:ML_QK_WIDTH] = mq_ref[...].astype(F32)
    cbuf[CONV_PAD:CONV_PAD + tl, ML_QK_WIDTH:] = mk_ref[...].astype(F32)
    acc = cb_ref[...] + cw_ref[0:1, :] * cbuf[pl.ds(CONV_PAD - (CONV_K - 1), tl), :]
    for tap in range(1, CONV_K):
        acc = acc + cw_ref[tap:tap + 1, :] * cbuf[pl.ds(CONV_PAD - (CONV_K - 1) + tap, tl), :]
    qk = _silu(acc)
    cbuf[0:CONV_PAD, :] = cbuf[tl:tl + CONV_PAD, :]
    q_bf = (qk[:, 0:ML_QK_WIDTH] * (ML_DQK ** -0.5)).astype(BF16)
    k = qk[:, ML_QK_WIDTH:]
    k_bf = k.astype(BF16)
    k_t = k.T

    gates = grow_ref[...]
    i_g = gates[0:GATE_TILE, :]
    hi, mid, lo = _split3(gates[GATE_TILE:, :])
    triu = triu_ref[...]
    part = _dot(_rows([hi, mid]), triu)
    g = part[0:GATE_TILE, :] + part[GATE_TILE:, :] + _dot(_rows([lo, lo]), triu)[0:GATE_TILE, :]
    u = i_g - g
    lane = lax.broadcasted_iota(jnp.int32, (GATE_TILE, tl), 1)
    cm = u
    shift = 1
    while shift < tl:
        cm = jnp.maximum(cm, jnp.where(lane >= shift, pltpu.roll(cm, shift, axis=1), -jnp.inf))
        shift *= 2
    m_prev = m_ref[...]
    m_run = jnp.maximum(m_prev, cm)
    w_inter = jnp.exp(m_prev - m_run)
    e_negm = jnp.exp(-(g + m_run))
    m_last = jnp.broadcast_to(m_run[:, tl - 1:tl], (GATE_TILE, tl))
    g_last = jnp.broadcast_to(g[:, tl - 1:tl], (GATE_TILE, tl))
    w_s = jnp.exp(u - m_last)
    w_old = jnp.exp(m_prev - m_last)
    m_ref[...] = g_last + m_last

    zeros8 = jnp.zeros((GATE_TILE, tl), F32)
    ones8 = jnp.ones((GATE_TILE, tl), F32)
    pack = _rows(_split3_f32(m_run) + [zeros8] + _split3_f32(w_inter) + [zeros8]
                 + _split3_f32(e_negm) + [zeros8] + [ones8, ones8, ones8, zeros8]).astype(BF16)
    pack_t = _dot_nt(ident_ref[...], pack).astype(BF16)
    u_rows = _rows(_split3_f32(u) + [zeros8])
    head_of_row = lax.broadcasted_iota(jnp.int32, u_rows.shape, 0) % GATE_TILE
    u_blocks = jnp.concatenate([jnp.where(head_of_row == h, u_rows, 0.0) for h in range(ML_HEADS)],
                               axis=1).astype(BF16)
    dmat = _dot(pack_t, _rows([seld_ref[...], u_blocks]))
    wrep = _dot(pack_t, selw_ref[...])

    v = mv_ref[...]
    bias = bias_ref[...]
    ones_bf = jnp.ones((tl, ML_DV), BF16)
    outs = []
    for h in range(ML_HEADS):
        qs = slice(h * ML_DQK, (h + 1) * ML_DQK)
        w_intra = jnp.exp(dmat[:, h * tl:(h + 1) * tl] + bias)
        p = (_dot_nt(q_bf[:, qs], k_bf[:, qs]) * w_intra).astype(BF16)
        v_aug = jnp.concatenate([v[:, h * ML_DV:(h + 1) * ML_DV], ones_bf], axis=1)
        c_aug = c_ref[h]
        intra = _dot(p, v_aug)
        inter = _dot(q_bf[:, qs], c_aug.astype(BF16))
        wi_rep = wrep[:, h * ML_AUG:h * ML_AUG + ML_DV]
        en_rep = wrep[:, h * ML_AUG + ML_DV:(h + 1) * ML_AUG]
        num = intra[:, 0:ML_DV] + wi_rep * inter[:, 0:ML_DV]
        den = intra[:, ML_DV:] + wi_rep * inter[:, ML_DV:]
        hh = num / jnp.maximum(jnp.abs(den), en_rep)
        kw_t = (k_t[qs, :] * w_s[h:h + 1, :]).astype(BF16)
        c_ref[h] = w_old[h:h + 1, 0:ML_AUG] * c_aug + _dot(kw_t, v_aug)
        outs.append(_head_rms(hh, ML_DV))
    o = jnp.concatenate(outs, axis=-1)
    o_ref[...] = (o * gn_ref[...] * og_ref[...].astype(F32)).astype(o_ref.dtype)


def _mlstm(mq, mk, mv, og, grow, conv_w, conv_b, gn, batch, seq):
    t = mq.shape[0]
    tl = ML_TILE
    assert seq % tl == 0 and tl >= ML_AUG
    steps = seq // tl
    consts = _mlstm_constants(tl)
    row = lambda b, s: (b * steps + s, 0)
    const = lambda b, s: (0, 0)
    return pl.pallas_call(
        functools.partial(_mlstm_kernel, tl=tl),
        grid=(batch, steps),
        in_specs=[pl.BlockSpec((tl, ML_QK_WIDTH), row), pl.BlockSpec((tl, ML_QK_WIDTH), row),
                  pl.BlockSpec((tl, ML_WIDTH), row), pl.BlockSpec((tl, ML_WIDTH), row),
                  pl.BlockSpec((GATE_ROWS, tl), lambda b, s: (0, b * steps + s)),
                  pl.BlockSpec(conv_w.shape, const), pl.BlockSpec(conv_b.shape, const),
                  pl.BlockSpec((1, ML_WIDTH), const)]
                 + [pl.BlockSpec(c.shape, const) for c in consts],
        out_specs=pl.BlockSpec((tl, ML_WIDTH), row),
        out_shape=jax.ShapeDtypeStruct((t, ML_WIDTH), BF16),
        scratch_shapes=[pltpu.VMEM((tl + CONV_PAD, 2 * ML_QK_WIDTH), F32),
                        pltpu.VMEM((ML_HEADS, ML_DQK, ML_AUG), F32),
                        pltpu.VMEM((GATE_TILE, tl), F32)],
        compiler_params=pltpu.CompilerParams(dimension_semantics=("arbitrary", "arbitrary"),
                                             vmem_limit_bytes=VMEM_LIMIT),
        name="mlstm",
    )(mq, mk, mv, og, grow, conv_w, conv_b, gn, *consts)


FF_TILE = 256


def _layer_norm(x, g, b):
    mu = jnp.mean(x, axis=-1, keepdims=True)
    xc = x - mu
    var = jnp.mean(xc * xc, axis=-1, keepdims=True)
    return xc * lax.rsqrt(var + LN_EPS) * g + b


def _post_kernel(ohg_ref, oml_ref, x_ref, p_ref, wo_ref, ln1g_ref, ln1b_ref, wg_ref, wu_ref, wd_ref,
                 ln2g_ref, ln2b_ref, wpp_ref, wpg_ref, bpg_ref, out_ref, *, alpha, d_ff):
    mix = _dot(ohg_ref[...], wo_ref[0:HG_WIDTH, :]) + _dot(oml_ref[...], wo_ref[HG_WIDTH:, :])
    h1 = _layer_norm(alpha * x_ref[...] + mix, ln1g_ref[...], ln1b_ref[...])
    h1b = h1.astype(BF16)
    ffn = jnp.zeros(h1.shape, F32)
    for lo in range(0, d_ff, FF_TILE):
        gate = _dot(h1b, wg_ref[:, lo:lo + FF_TILE])
        up = _dot(h1b, wu_ref[:, lo:lo + FF_TILE])
        ffn = ffn + _dot((_silu(gate) * up).astype(BF16), wd_ref[lo:lo + FF_TILE, :])
    h2 = _layer_norm(alpha * h1 + ffn, ln2g_ref[...], ln2b_ref[...])
    pgate = _sigmoid(_dot(h2.astype(BF16), wpg_ref[...]) + bpg_ref[...])
    pemb = _dot(p_ref[...].astype(BF16), wpp_ref[...])
    out_ref[...] = h2 + pgate * pemb


def _post(ohg, oml, x2, p2, wo, ln1g, ln1b, wg, wu, wd, ln2g, ln2b, wpp, wpg, bpg, alpha, tm):
    t, d = x2.shape
    d_ff = wg.shape[1]
    assert d_ff % FF_TILE == 0
    row = lambda i: (i, 0)

    def const(a):
        return pl.BlockSpec(a.shape, lambda i: (0, 0), pipeline_mode=pl.Buffered(1))

    return pl.pallas_call(
        functools.partial(_post_kernel, alpha=alpha, d_ff=d_ff),
        grid=(t // tm,),
        in_specs=[pl.BlockSpec((tm, HG_WIDTH), row), pl.BlockSpec((tm, ML_WIDTH), row),
                  pl.BlockSpec((tm, d), row), pl.BlockSpec((tm, p2.shape[1]), row),
                  const(wo), const(ln1g), const(ln1b), const(wg), const(wu), const(wd),
                  const(ln2g), const(ln2b), const(wpp), const(wpg), const(bpg)],
        out_specs=pl.BlockSpec((tm, d), row),
        out_shape=jax.ShapeDtypeStruct((t, d), F32),
        compiler_params=pltpu.CompilerParams(dimension_semantics=("arbitrary",),
                                             vmem_limit_bytes=VMEM_LIMIT),
        name="post",
    )(ohg, oml, x2, p2, wo, ln1g, ln1b, wg, wu, wd, ln2g, ln2b, wpp, wpg, bpg)


def _pick(n, candidates):
    for c in candidates:
        if n % c == 0:
            return c
    raise ValueError(f"no tile for {n}")


def kernel(x, p, w_in, b_in, hg_lb_logits, ml_conv_w, ml_conv_b, hg_norm_g, ml_norm_g, w_out, ln1_g, ln1_b,
           w_ffn_gate, w_ffn_up, w_ffn_down, ln2_g, ln2_b, ple_w_proj, ple_w_gate, ple_b_gate):
    batch, seq, d = x.shape
    depth = w_in.shape[0]
    t = batch * seq
    alpha = float((2 * depth) ** 0.25)
    tm = _pick(t, (512, 256))
    ts = _pick(seq, (256, 128, 64))
    assert w_in.shape[2] == OFF_GATES + 2 * ML_HEADS
    assert depth == 1, "lower-bound cumsum is specialised to a single layer"

    x2 = x.reshape(t, d)
    for i in range(depth):
        w_i = w_in[i]
        w_bf = w_i[:, :OFF_GATES].astype(BF16)
        b_row = b_in[i, :OFF_GATES].reshape(1, OFF_GATES)
        wg_t = w_i[:, OFF_GATES:].T
        gate_pad = ((0, GATE_TILE - ML_HEADS), (0, 0))
        wgt = jnp.concatenate([jnp.pad(wg_t[:ML_HEADS], gate_pad), jnp.pad(wg_t[ML_HEADS:], gate_pad)]).astype(BF16)
        bg = b_in[i, OFF_GATES:].reshape(2 * ML_HEADS, 1)
        bgt = jnp.concatenate([jnp.pad(bg[:ML_HEADS], gate_pad), jnp.pad(bg[ML_HEADS:], gate_pad)])

        (logf, hq, hk, hv, hg, mq, mk, mv, mo, grow) = _inproj(x2, w_bf, b_row, wgt, bgt, hg_lb_logits, tm)

        o_hg = _hgrn2(logf, hq, hk, hv, hg, hg_norm_g[i].reshape(1, HG_WIDTH), batch, seq, ts)
        o_ml = _mlstm(mq, mk, mv, mo, grow, ml_conv_w[i], ml_conv_b[i].reshape(1, -1),
                      ml_norm_g[i].reshape(1, ML_WIDTH), batch, seq)

        x2 = _post(o_hg, o_ml, x2, p[i].reshape(t, -1),
                   w_out[i].astype(BF16), ln1_g[i].reshape(1, d), ln1_b[i].reshape(1, d),
                   w_ffn_gate[i].astype(BF16), w_ffn_up[i].astype(BF16), w_ffn_down[i].astype(BF16),
                   ln2_g[i].reshape(1, d), ln2_b[i].reshape(1, d),
                   ple_w_proj[i].astype(BF16), ple_w_gate[i].astype(BF16), ple_b_gate[i].reshape(1, d),
                   alpha, tm)
    return x2.reshape(batch, seq, d)
```

```python
import functools

import numpy as np
import jax
import jax.numpy as jnp
from jax import lax
from jax.experimental import pallas as pl
from jax.experimental.pallas import tpu as pltpu

F32 = jnp.float32
BF16 = jnp.bfloat16

CHUNK = 64
SUB = 16
N_SUB = CHUNK // SUB
EX = 8
N_EX = CHUNK // EX
EX_PER_SUB = SUB // EX
LOG2E = 1.4426950408889634
HG_HEADS = 4
HG_DK = 128
HG_DV = 128
HG_WIDTH = HG_HEADS * HG_DV
ML_HEADS = 4
ML_DQK = 64
ML_DV = 128
ML_WIDTH = ML_HEADS * ML_DV
ML_QK_WIDTH = ML_HEADS * ML_DQK
CONV_K = 4
LN_EPS = 1e-5
RMS_EPS = 1e-6

OFF_HQ = 0
OFF_HF = OFF_HQ + HG_HEADS * HG_DK
OFF_HV = OFF_HF + HG_HEADS * HG_DK
OFF_HG = OFF_HV + HG_WIDTH
OFF_MQ = OFF_HG + HG_WIDTH
OFF_MK = OFF_MQ + ML_QK_WIDTH
OFF_MV = OFF_MK + ML_QK_WIDTH
OFF_MO = OFF_MV + ML_WIDTH
OFF_GATES = OFF_MO + ML_WIDTH
LANE = 128
GATE_TILE = 8
GATE_ROWS = 2 * GATE_TILE

VMEM_LIMIT = 56 * 1024 * 1024


def _sigmoid(x):
    return 1.0 / (1.0 + jnp.exp(-x))


def _silu(x):
    return x * _sigmoid(x)


def _log_sigmoid(x):
    return jnp.minimum(x, 0.0) - jnp.log(1.0 + jnp.exp(-jnp.abs(x)))


def _split3(x):
    hi = x.astype(BF16)
    r1 = x - hi.astype(F32)
    mid = r1.astype(BF16)
    lo = (r1 - mid.astype(F32)).astype(BF16)
    return hi, mid, lo


def _dot(a, b):
    return jnp.dot(a, b, preferred_element_type=F32)


def _dot_nt(a, b):
    return lax.dot_general(a, b, (((1,), (1,)), ((), ())), preferred_element_type=F32)


def _dot_tn(a, b):
    return lax.dot_general(a, b, (((0,), (0,)), ((), ())), preferred_element_type=F32)


def _rows(blocks):
    return jnp.concatenate(blocks, axis=0)


def _inproj_kernel(x_ref, w_ref, b_ref, wgt_ref, bgt_ref, lbl_ref,
                   logf_ref, hq_ref, hk_ref, hv_ref, hg_ref,
                   mq_ref, mk_ref, mv_ref, mo_ref, grow_ref):
    xb = x_ref[...].astype(BF16)

    def proj(lo, width):
        return _dot(xb, w_ref[:, lo:lo + width]) + b_ref[:, lo:lo + width]

    hq_ref[...] = _silu(proj(OFF_HQ, HG_WIDTH)).astype(BF16)

    logits = lbl_ref[...]
    mx = jnp.max(logits, axis=0, keepdims=True)
    ex = jnp.exp(logits - mx)
    den = jnp.sum(ex, axis=0, keepdims=True)
    lb = ex[0:1, :] / den
    one_m_lb = (den - ex[0:1, :]) / den
    sig = _sigmoid(proj(OFF_HF, HG_WIDTH))
    logf_ref[...] = jnp.log(lb + one_m_lb * sig)
    hk_ref[...] = (one_m_lb * (1.0 - sig)).astype(BF16)

    hv_ref[...] = proj(OFF_HV, HG_WIDTH).astype(BF16)
    hg_ref[...] = _silu(proj(OFF_HG, HG_WIDTH)).astype(BF16)
    mq_ref[...] = proj(OFF_MQ, ML_QK_WIDTH).astype(BF16)
    mk_ref[...] = proj(OFF_MK, ML_QK_WIDTH).astype(BF16)
    mv_ref[...] = proj(OFF_MV, ML_WIDTH).astype(BF16)
    mo_ref[...] = _sigmoid(proj(OFF_MO, ML_WIDTH)).astype(BF16)

    gt = _dot_nt(wgt_ref[...], xb) + bgt_ref[...]
    sub = lax.broadcasted_iota(jnp.int32, gt.shape, 0)
    is_fgate = (sub >= GATE_TILE) & (sub < GATE_TILE + ML_HEADS)
    grow_ref[...] = jnp.where(is_fgate, _log_sigmoid(gt), gt)


def _inproj(x2, w_bf, b_row, wgt, bgt, lb_logits, tm):
    t, d = x2.shape
    grid = (t // tm,)
    row = lambda i: (i, 0)
    const = lambda i: (0, 0)
    out_shapes = (
        jax.ShapeDtypeStruct((t, HG_WIDTH), F32),
        jax.ShapeDtypeStruct((t, HG_WIDTH), BF16),
        jax.ShapeDtypeStruct((t, HG_WIDTH), BF16),
        jax.ShapeDtypeStruct((t, HG_WIDTH), BF16),
        jax.ShapeDtypeStruct((t, HG_WIDTH), BF16),
        jax.ShapeDtypeStruct((t, ML_QK_WIDTH), BF16),
        jax.ShapeDtypeStruct((t, ML_QK_WIDTH), BF16),
        jax.ShapeDtypeStruct((t, ML_WIDTH), BF16),
        jax.ShapeDtypeStruct((t, ML_WIDTH), BF16),
        jax.ShapeDtypeStruct((GATE_ROWS, t), F32),
    )
    out_specs = (
        pl.BlockSpec((tm, HG_WIDTH), row), pl.BlockSpec((tm, HG_WIDTH), row),
        pl.BlockSpec((tm, HG_WIDTH), row), pl.BlockSpec((tm, HG_WIDTH), row),
        pl.BlockSpec((tm, HG_WIDTH), row), pl.BlockSpec((tm, ML_QK_WIDTH), row),
        pl.BlockSpec((tm, ML_QK_WIDTH), row), pl.BlockSpec((tm, ML_WIDTH), row),
        pl.BlockSpec((tm, ML_WIDTH), row),
        pl.BlockSpec((GATE_ROWS, tm), lambda i: (0, i)),
    )
    in_specs = [
        pl.BlockSpec((tm, d), row),
        pl.BlockSpec(w_bf.shape, const),
        pl.BlockSpec(b_row.shape, const),
        pl.BlockSpec(wgt.shape, const),
        pl.BlockSpec(bgt.shape, const),
        pl.BlockSpec(lb_logits.shape, const),
    ]
    return pl.pallas_call(
        _inproj_kernel, grid=grid, in_specs=in_specs, out_specs=out_specs, out_shape=out_shapes,
        compiler_params=pltpu.CompilerParams(dimension_semantics=("arbitrary",),
                                             vmem_limit_bytes=VMEM_LIMIT),
        name="inproj",
    )(x2, w_bf, b_row, wgt, bgt, lb_logits)


def _cumsum_rows(tril_bf, x):
    hi, mid, lo = _split3(x)
    return _dot(tril_bf, hi) + _dot(tril_bf, mid) + _dot(tril_bf, lo)


def _head_rms(o, width):
    ms = jnp.sum(o * o, axis=-1, keepdims=True) * (1.0 / width)
    return o * lax.rsqrt(ms + RMS_EPS)


def _hgrn2_select():
    sel = np.zeros((EX, HG_DK, CHUNK), np.float32)
    for j in range(EX):
        sel[j, :, j::EX] = 1.0
    return jnp.asarray(sel.reshape(EX * HG_DK, CHUNK), BF16)


def _hgrn2_kernel(logf_ref, q_ref, k_ref, v_ref, og_ref, gn_ref, sel_ref, tril_ref, o_ref,
                  st_ref, b_scr, c_scr, *, n_chunks):
    @pl.when(pl.program_id(1) == 0)
    def _():
        st_ref[...] = jnp.zeros_like(st_ref)

    row = lax.broadcasted_iota(jnp.int32, (CHUNK, CHUNK), 0)
    col = lax.broadcasted_iota(jnp.int32, (CHUNK, CHUNK), 1)
    causal = col <= row
    exact_mask = causal & (row // EX == col // EX)
    pair_mask = row // SUB == col // SUB
    sel = sel_ref[...]

    def blocks(vals):
        return _rows([jnp.broadcast_to(jnp.asarray(x, F32), (EX, HG_WIDTH)) for x in vals])

    b_scr[...] = _cumsum_rows(tril_ref[...], logf_ref[...]) * LOG2E
    c_scr[...] = b_scr[...] - jnp.log2(k_ref[...].astype(F32))

    def scores(c):
        r0 = c * CHUNK
        rows = pl.ds(r0, CHUNK)
        b2 = b_scr[rows, :]
        c2 = c_scr[rows, :]
        qf = q_ref[rows, :].astype(F32)

        bd = [jnp.zeros((1, HG_WIDTH), F32)]
        bd += [b_scr[pl.ds(r0 + EX * m - 1, 1), :] for m in range(1, N_EX + 1)]
        per_sub = EX_PER_SUB
        sub_start = blocks([bd[(m // per_sub) * per_sub] for m in range(N_EX)])
        q_st = qf * jnp.exp2(b2 - sub_start)
        q_in = (q_st * jnp.exp2(sub_start)).astype(BF16)
        k_dec = jnp.exp2(bd[N_EX] - c2).astype(BF16)
        chunk_decay = jnp.exp2(bd[N_EX])
        q_half = qf * jnp.exp2(b2 - blocks([bd[m] if m % per_sub else jnp.inf for m in range(N_EX)]))
        k_half = jnp.exp2(blocks([-jnp.inf if m % per_sub else bd[m + 1] for m in range(N_EX)]) - c2)

        prods = []
        for h in range(HG_HEADS):
            hs = slice(h * HG_DK, (h + 1) * HG_DK)
            per_key = []
            for j in range(EX):
                c_j = _rows([jnp.broadcast_to(c_scr[pl.ds(r0 + m * EX + j, 1), hs], (EX, HG_DK))
                            for m in range(N_EX)])
                per_key.append((qf[:, hs] * jnp.exp2(jnp.minimum(b2[:, hs] - c_j, 0.0))).astype(BF16))
            prods.append(jnp.concatenate(per_key, axis=1))
        exact_all = _dot(_rows(prods), sel)

        a_heads = []
        for h in range(HG_HEADS):
            hs = slice(h * HG_DK, (h + 1) * HG_DK)
            exact = exact_all[h * CHUNK:(h + 1) * CHUNK, :]
            half = _dot_nt(q_half[:, hs].astype(BF16), k_half[:, hs].astype(BF16))
            off = [jnp.zeros((SUB, CHUNK), F32)]
            for i in range(1, N_SUB):
                k_i = jnp.exp2(bd[i * per_sub][:, hs] - c2[0:i * SUB, hs])
                k_i = _rows([k_i, jnp.zeros((CHUNK - i * SUB, HG_DK), F32)])
                off.append(_dot_nt(q_st[i * SUB:(i + 1) * SUB, hs].astype(BF16), k_i.astype(BF16)))
            a = jnp.where(exact_mask, exact, jnp.where(pair_mask, half, 0.0)) + _rows(off)
            a_heads.append(a.astype(BF16))
        return a_heads, q_in, k_dec, chunk_decay

    def outputs(c, pre, states):
        a_heads, q_in, k_dec, chunk_decay = pre
        rows = pl.ds(c * CHUNK, CHUNK)
        v = v_ref[rows, :]
        outs = []
        for h in range(HG_HEADS):
            hs = slice(h * HG_DK, (h + 1) * HG_DK)
            st = states[h]
            o_h = _dot(a_heads[h], v[:, hs]) + _dot_nt(q_in[:, hs], st.astype(BF16))
            states[h] = st * chunk_decay[:, hs] + _dot_tn(v[:, hs], k_dec[:, hs])
            outs.append(_head_rms(o_h, HG_DV))
        o = jnp.concatenate(outs, axis=-1)
        o_ref[rows, :] = (o * gn_ref[...] * og_ref[rows, :].astype(F32)).astype(o_ref.dtype)

    states = [st_ref[h] for h in range(HG_HEADS)]
    pre = scores(0)
    for c in range(n_chunks):
        nxt = scores(c + 1) if c + 1 < n_chunks else None
        outputs(c, pre, states)
        pre = nxt
    for h in range(HG_HEADS):
        st_ref[h] = states[h]


def _hgrn2(logf, q, k, v, og, gn, batch, seq, ts):
    t = logf.shape[0]
    steps = seq // ts
    grid = (batch, steps)
    row = lambda b, s: (b * steps + s, 0)
    spec = pl.BlockSpec((ts, HG_WIDTH), row)
    const = lambda b, s: (0, 0)
    sel = _hgrn2_select()
    idx = np.arange(ts)
    tril = jnp.asarray((idx[:, None] >= idx[None, :]) & (idx[:, None] // CHUNK == idx[None, :] // CHUNK), BF16)
    return pl.pallas_call(
        functools.partial(_hgrn2_kernel, n_chunks=ts // CHUNK),
        grid=grid,
        in_specs=[spec, spec, spec, spec, spec, pl.BlockSpec((1, HG_WIDTH), const),
                  pl.BlockSpec(sel.shape, const), pl.BlockSpec(tril.shape, const)],
        out_specs=spec,
        out_shape=jax.ShapeDtypeStruct((t, HG_WIDTH), BF16),
        scratch_shapes=[pltpu.VMEM((HG_HEADS, HG_DV, HG_DK), F32),
                        pltpu.VMEM((ts, HG_WIDTH), F32),
                        pltpu.VMEM((ts, HG_WIDTH), F32)],
        compiler_params=pltpu.CompilerParams(dimension_semantics=("arbitrary", "arbitrary"),
                                             vmem_limit_bytes=VMEM_LIMIT),
        name="hgrn2",
    )(logf, q, k, v, og, gn, sel, tril)


CONV_PAD = 8
ML_TILE = 256
SEL_ROWS = 128
GRP_M, GRP_WI, GRP_EN, GRP_ONE = 0, 32, 64, 96
ML_AUG = 2 * ML_DV


def _mlstm_constants(tl):
    ident = np.eye(tl, dtype=np.float32)
    triu = np.triu(np.ones((tl, tl), np.float32))
    bias = np.where(np.tril(np.ones((tl, tl), bool)), 0.0, -np.inf).astype(np.float32)
    sel_d = np.zeros((GRP_ONE, ML_HEADS * tl), np.float32)
    sel_w = np.zeros((SEL_ROWS, ML_HEADS * ML_AUG), np.float32)
    for h in range(ML_HEADS):
        for k in range(3):
            sel_d[GRP_M + GATE_TILE * k + h, h * tl:(h + 1) * tl] = -1.0
            sel_w[GRP_WI + GATE_TILE * k + h, h * ML_AUG:h * ML_AUG + ML_DV] = 1.0
            sel_w[GRP_EN + GATE_TILE * k + h, h * ML_AUG + ML_DV:(h + 1) * ML_AUG] = 1.0
    return (jnp.asarray(ident, BF16), jnp.asarray(triu, BF16), jnp.asarray(bias),
            jnp.asarray(sel_d, BF16), jnp.asarray(sel_w, BF16))


def _split3_f32(x):
    return [s.astype(F32) for s in _split3(x)]


def _mlstm_kernel(mq_ref, mk_ref, mv_ref, og_ref, grow_ref, cw_ref, cb_ref, gn_ref,
                  ident_ref, triu_ref, bias_ref, seld_ref, selw_ref, o_ref,
                  cbuf, c_ref, m_ref, *, tl):
    @pl.when(pl.program_id(1) == 0)
    def _():
        cbuf[0:CONV_PAD, :] = jnp.zeros((CONV_PAD, 2 * ML_QK_WIDTH), F32)
        c_ref[...] = jnp.zeros_like(c_ref)
        m_ref[...] = jnp.zeros_like(m_ref)

    cbuf[CONV_PAD:CONV_PAD + tl, 0:ML_QK_WIDTH] = mq_ref[...].astype(F32)
    cbuf[CONV_PAD:CONV_PAD + tl, ML_QK_WIDTH:] = mk_ref[...].astype(F32)
    acc = cb_ref[...] + cw_ref[0:1, :] * cbuf[pl.ds(CONV_PAD - (CONV_K - 1), tl), :]
    for tap in range(1, CONV_K):
        acc = acc + cw_ref[tap:tap + 1, :] * cbuf[pl.ds(CONV_PAD - (CONV_K - 1) + tap, tl), :]
    qk = _silu(acc)
    cbuf[0:CONV_PAD, :] = cbuf[tl:tl + CONV_PAD, :]
    q_bf = (qk[:, 0:ML_QK_WIDTH] * (ML_DQK ** -0.5)).astype(BF16)
    k = qk[:, ML_QK_WIDTH:]
    k_bf = k.astype(BF16)
    k_t = k.T

    gates = grow_ref[...]
    i_g = gates[0:GATE_TILE, :]
    hi, mid, lo = _split3(gates[GATE_TILE:, :])
    triu = triu_ref[...]
    part = _dot(_rows([hi, mid]), triu)
    g = part[0:GATE_TILE, :] + part[GATE_TILE:, :] + _dot(_rows([lo, lo]), triu)[0:GATE_TILE, :]
    u = i_g - g
    lane = lax.broadcasted_iota(jnp.int32, (GATE_TILE, tl), 1)
    cm = u
    shift = 1
    while shift < tl:
        cm = jnp.maximum(cm, jnp.where(lane >= shift, pltpu.roll(cm, shift, axis=1), -jnp.inf))
        shift *= 2
    m_prev = m_ref[...]
    m_run = jnp.maximum(m_prev, cm)
    w_inter = jnp.exp(m_prev - m_run)
    e_negm = jnp.exp(-(g + m_run))
    m_last = jnp.broadcast_to(m_run[:, tl - 1:tl], (GATE_TILE, tl))
    g_last = jnp.broadcast_to(g[:, tl - 1:tl], (GATE_TILE, tl))
    w_s = jnp.exp(u - m_last)
    w_old = jnp.exp(m_prev - m_last)
    m_ref[...] = g_last + m_last

    zeros8 = jnp.zeros((GATE_TILE, tl), F32)
    ones8 = jnp.ones((GATE_TILE, tl), F32)
    pack = _rows(_split3_f32(m_run) + [zeros8] + _split3_f32(w_inter) + [zeros8]
                 + _split3_f32(e_negm) + [zeros8] + [ones8, ones8, ones8, zeros8]).astype(BF16)
    pack_t = _dot_nt(ident_ref[...], pack).astype(BF16)
    u_rows = _rows(_split3_f32(u) + [zeros8])
    head_of_row = lax.broadcasted_iota(jnp.int32, u_rows.shape, 0) % GATE_TILE
    u_blocks = jnp.concatenate([jnp.where(head_of_row == h, u_rows, 0.0) for h in range(ML_HEADS)],
                               axis=1).astype(BF16)
    dmat = _dot(pack_t, _rows([seld_ref[...], u_blocks]))
    wrep = _dot(pack_t, selw_ref[...])

    v = mv_ref[...]
    bias = bias_ref[...]
    ones_bf = jnp.ones((tl, ML_DV), BF16)
    outs = []
    for h in range(ML_HEADS):
        qs = slice(h * ML_DQK, (h + 1) * ML_DQK)
        w_intra = jnp.exp(dmat[:, h * tl:(h + 1) * tl] + bias)
        p = (_dot_nt(q_bf[:, qs], k_bf[:, qs]) * w_intra).astype(BF16)
        v_aug = jnp.concatenate([v[:, h * ML_DV:(h + 1) * ML_DV], ones_bf], axis=1)
        c_aug = c_ref[h]
        intra = _dot(p, v_aug)
        inter = _dot(q_bf[:, qs], c_aug.astype(BF16))
        wi_rep = wrep[:, h * ML_AUG:h * ML_AUG + ML_DV]
        en_rep = wrep[:, h * ML_AUG + ML_DV:(h + 1) * ML_AUG]
        num = intra[:, 0:ML_DV] + wi_rep * inter[:, 0:ML_DV]
        den = intra[:, ML_DV:] + wi_rep * inter[:, ML_DV:]
        hh = num / jnp.maximum(jnp.abs(den), en_rep)
        kw_t = (k_t[qs, :] * w_s[h:h + 1, :]).astype(BF16)
        c_ref[h] = w_old[h:h + 1, 0:ML_AUG] * c_aug + _dot(kw_t, v_aug)
        outs.append(_head_rms(hh, ML_DV))
    o = jnp.concatenate(outs, axis=-1)
    o_ref[...] = (o * gn_ref[...] * og_ref[...].astype(F32)).astype(o_ref.dtype)


def _mlstm(mq, mk, mv, og, grow, conv_w, conv_b, gn, batch, seq):
    t = mq.shape[0]
    tl = ML_TILE
    assert seq % tl == 0 and tl >= ML_AUG
    steps = seq // tl
    consts = _mlstm_constants(tl)
    row = lambda b, s: (b * steps + s, 0)
    const = lambda b, s: (0, 0)
    return pl.pallas_call(
        functools.partial(_mlstm_kernel, tl=tl),
        grid=(batch, steps),
        in_specs=[pl.BlockSpec((tl, ML_QK_WIDTH), row), pl.BlockSpec((tl, ML_QK_WIDTH), row),
                  pl.BlockSpec((tl, ML_WIDTH), row), pl.BlockSpec((tl, ML_WIDTH), row),
                  pl.BlockSpec((GATE_ROWS, tl), lambda b, s: (0, b * steps + s)),
                  pl.BlockSpec(conv_w.shape, const), pl.BlockSpec(conv_b.shape, const),
                  pl.BlockSpec((1, ML_WIDTH), const)]
                 + [pl.BlockSpec(c.shape, const) for c in consts],
        out_specs=pl.BlockSpec((tl, ML_WIDTH), row),
        out_shape=jax.ShapeDtypeStruct((t, ML_WIDTH), BF16),
        scratch_shapes=[pltpu.VMEM((tl + CONV_PAD, 2 * ML_QK_WIDTH), F32),
                        pltpu.VMEM((ML_HEADS, ML_DQK, ML_AUG), F32),
                        pltpu.VMEM((GATE_TILE, tl), F32)],
        compiler_params=pltpu.CompilerParams(dimension_semantics=("arbitrary", "arbitrary"),
                                             vmem_limit_bytes=VMEM_LIMIT),
        name="mlstm",
    )(mq, mk, mv, og, grow, conv_w, conv_b, gn, *consts)


FF_TILE = 256


def _layer_norm(x, g, b):
    mu = jnp.mean(x, axis=-1, keepdims=True)
    xc = x - mu
    var = jnp.mean(xc * xc, axis=-1, keepdims=True)
    return xc * lax.rsqrt(var + LN_EPS) * g + b


def _post_kernel(ohg_ref, oml_ref, x_ref, p_ref, wo_ref, ln1g_ref, ln1b_ref, wg_ref, wu_ref, wd_ref,
                 ln2g_ref, ln2b_ref, wpp_ref, wpg_ref, bpg_ref, out_ref, *, alpha, d_ff):
    mix = _dot(ohg_ref[...], wo_ref[0:HG_WIDTH, :]) + _dot(oml_ref[...], wo_ref[HG_WIDTH:, :])
    h1 = _layer_norm(alpha * x_ref[...] + mix, ln1g_ref[...], ln1b_ref[...])
    h1b = h1.astype(BF16)
    ffn = jnp.zeros(h1.shape, F32)
    for lo in range(0, d_ff, FF_TILE):
        gate = _dot(h1b, wg_ref[:, lo:lo + FF_TILE])
        up = _dot(h1b, wu_ref[:, lo:lo + FF_TILE])
        ffn = ffn + _dot((_silu(gate) * up).astype(BF16), wd_ref[lo:lo + FF_TILE, :])
    h2 = _layer_norm(alpha * h1 + ffn, ln2g_ref[...], ln2b_ref[...])
    pgate = _sigmoid(_dot(h2.astype(BF16), wpg_ref[...]) + bpg_ref[...])
    pemb = _dot(p_ref[...].astype(BF16), wpp_ref[...])
    out_ref[...] = h2 + pgate * pemb


def _post(ohg, oml, x2, p2, wo, ln1g, ln1b, wg, wu, wd, ln2g, ln2b, wpp, wpg, bpg, alpha, tm):
    t, d = x2.shape
    d_ff = wg.shape[1]
    assert d_ff % FF_TILE == 0
    row = lambda i: (i, 0)

    def const(a):
        return pl.BlockSpec(a.shape, lambda i: (0, 0), pipeline_mode=pl.Buffered(1))

    return pl.pallas_call(
        functools.partial(_post_kernel, alpha=alpha, d_ff=d_ff),
        grid=(t // tm,),
        in_specs=[pl.BlockSpec((tm, HG_WIDTH), row), pl.BlockSpec((tm, ML_WIDTH), row),
                  pl.BlockSpec((tm, d), row), pl.BlockSpec((tm, p2.shape[1]), row),
                  const(wo), const(ln1g), const(ln1b), const(wg), const(wu), const(wd),
                  const(ln2g), const(ln2b), const(wpp), const(wpg), const(bpg)],
        out_specs=pl.BlockSpec((tm, d), row),
        out_shape=jax.ShapeDtypeStruct((t, d), F32),
        compiler_params=pltpu.CompilerParams(dimension_semantics=("arbitrary",),
                                             vmem_limit_bytes=VMEM_LIMIT),
        name="post",
    )(ohg, oml, x2, p2, wo, ln1g, ln1b, wg, wu, wd, ln2g, ln2b, wpp, wpg, bpg)


def _pick(n, candidates):
    for c in candidates:
        if n % c == 0:
            return c
    raise ValueError(f"no tile for {n}")


def kernel(x, p, w_in, b_in, hg_lb_logits, ml_conv_w, ml_conv_b, hg_norm_g, ml_norm_g, w_out, ln1_g, ln1_b,
           w_ffn_gate, w_ffn_up, w_ffn_down, ln2_g, ln2_b, ple_w_proj, ple_w_gate, ple_b_gate):
    batch, seq, d = x.shape
    depth = w_in.shape[0]
    t = batch * seq
    alpha = float((2 * depth) ** 0.25)
    tm = _pick(t, (512, 256))
    ts = _pick(seq, (256, 128, 64))
    assert w_in.shape[2] == OFF_GATES + 2 * ML_HEADS
    assert depth == 1, "lower-bound cumsum is specialised to a single layer"

    x2 = x.reshape(t, d)
    for i in range(depth):
        w_i = w_in[i]
        w_bf = w_i[:, :OFF_GATES].astype(BF16)
        b_row = b_in[i, :OFF_GATES].reshape(1, OFF_GATES)
        wg_t = w_i[:, OFF_GATES:].T
        gate_pad = ((0, GATE_TILE - ML_HEADS), (0, 0))
        wgt = jnp.concatenate([jnp.pad(wg_t[:ML_HEADS], gate_pad), jnp.pad(wg_t[ML_HEADS:], gate_pad)]).astype(BF16)
        bg = b_in[i, OFF_GATES:].reshape(2 * ML_HEADS, 1)
        bgt = jnp.concatenate([jnp.pad(bg[:ML_HEADS], gate_pad), jnp.pad(bg[ML_HEADS:], gate_pad)])

        (logf, hq, hk, hv, hg, mq, mk, mv, mo, grow) = _inproj(x2, w_bf, b_row, wgt, bgt, hg_lb_logits, tm)

        o_hg = _hgrn2(logf, hq, hk, hv, hg, hg_norm_g[i].reshape(1, HG_WIDTH), batch, seq, ts)
        o_ml = _mlstm(mq, mk, mv, mo, grow, ml_conv_w[i], ml_conv_b[i].reshape(1, -1),
                      ml_norm_g[i].reshape(1, ML_WIDTH), batch, seq)

        x2 = _post(o_hg, o_ml, x2, p[i].reshape(t, -1),
                   w_out[i].astype(BF16), ln1_g[i].reshape(1, d), ln1_b[i].reshape(1, d),
                   w_ffn_gate[i].astype(BF16), w_ffn_up[i].astype(BF16), w_ffn_down[i].astype(BF16),
                   ln2_g[i].reshape(1, d), ln2_b[i].reshape(1, d),
                   ple_w_proj[i].astype(BF16), ple_w_gate[i].astype(BF16), ple_b_gate[i].reshape(1, d),
                   alpha, tm)
    return x2.reshape(batch, seq, d)
```

```python
import functools

import numpy as np
import jax
import jax.numpy as jnp
from jax import lax
from jax.experimental import pallas as pl
from jax.experimental.pallas import tpu as pltpu

F32 = jnp.float32
BF16 = jnp.bfloat16

CHUNK = 64
SUB = 16
N_SUB = CHUNK // SUB
EX = 8
N_EX = CHUNK // EX
EX_PER_SUB = SUB // EX
LOG2E = 1.4426950408889634
HG_HEADS = 4
HG_DK = 128
HG_DV = 128
HG_WIDTH = HG_HEADS * HG_DV
ML_HEADS = 4
ML_DQK = 64
ML_DV = 128
ML_WIDTH = ML_HEADS * ML_DV
ML_QK_WIDTH = ML_HEADS * ML_DQK
CONV_K = 4
LN_EPS = 1e-5
RMS_EPS = 1e-6

OFF_HQ = 0
OFF_HF = OFF_HQ + HG_HEADS * HG_DK
OFF_HV = OFF_HF + HG_HEADS * HG_DK
OFF_HG = OFF_HV + HG_WIDTH
OFF_MQ = OFF_HG + HG_WIDTH
OFF_MK = OFF_MQ + ML_QK_WIDTH
OFF_MV = OFF_MK + ML_QK_WIDTH
OFF_MO = OFF_MV + ML_WIDTH
OFF_GATES = OFF_MO + ML_WIDTH
LANE = 128
GATE_TILE = 8
GATE_ROWS = 2 * GATE_TILE

VMEM_LIMIT = 60 * 1024 * 1024
MIX_TILE = 256
TAIL_TILE = 512


def _sigmoid(x):
    return 1.0 / (1.0 + jnp.exp(-x))


def _silu(x):
    return x * _sigmoid(x)


def _log_sigmoid(x):
    return jnp.minimum(x, 0.0) - jnp.log(1.0 + jnp.exp(-jnp.abs(x)))


def _split3(x):
    hi = x.astype(BF16)
    r1 = x - hi.astype(F32)
    mid = r1.astype(BF16)
    lo = (r1 - mid.astype(F32)).astype(BF16)
    return hi, mid, lo


def _split3_f32(x):
    return [s.astype(F32) for s in _split3(x)]


def _dot(a, b):
    return jnp.dot(a, b, preferred_element_type=F32)


def _dot_nt(a, b):
    return lax.dot_general(a, b, (((1,), (1,)), ((), ())), preferred_element_type=F32)


def _dot_tn(a, b):
    return lax.dot_general(a, b, (((0,), (0,)), ((), ())), preferred_element_type=F32)


def _rows(blocks):
    return jnp.concatenate(blocks, axis=0)


def _cumsum_rows(tril_bf, x):
    hi, mid, lo = _split3(x)
    return _dot(tril_bf, hi) + _dot(tril_bf, mid) + _dot(tril_bf, lo)


def _head_rms(o, width):
    ms = jnp.sum(o * o, axis=-1, keepdims=True) * (1.0 / width)
    return o * lax.rsqrt(ms + RMS_EPS)


def _inproj_kernel(x_ref, w_ref, b_ref, wgt_ref, bgt_ref, lbl_ref,
                   logf_ref, hq_ref, hk_ref, hv_ref, hg_ref,
                   mq_ref, mk_ref, mv_ref, mo_ref, grow_ref):
    xb = x_ref[...].astype(BF16)

    def proj(lo, width):
        return _dot(xb, w_ref[:, lo:lo + width]) + b_ref[:, lo:lo + width]

    hq_ref[...] = _silu(proj(OFF_HQ, HG_WIDTH)).astype(BF16)

    logits = lbl_ref[...]
    mx = jnp.max(logits, axis=0, keepdims=True)
    ex = jnp.exp(logits - mx)
    den = jnp.sum(ex, axis=0, keepdims=True)
    lb = ex[0:1, :] / den
    one_m_lb = (den - ex[0:1, :]) / den
    sig = _sigmoid(proj(OFF_HF, HG_WIDTH))
    logf_ref[...] = jnp.log(lb + one_m_lb * sig)
    hk_ref[...] = (one_m_lb * (1.0 - sig)).astype(BF16)

    hv_ref[...] = proj(OFF_HV, HG_WIDTH).astype(BF16)
    hg_ref[...] = _silu(proj(OFF_HG, HG_WIDTH)).astype(BF16)
    mq_ref[...] = proj(OFF_MQ, ML_QK_WIDTH).astype(BF16)
    mk_ref[...] = proj(OFF_MK, ML_QK_WIDTH).astype(BF16)
    mv_ref[...] = proj(OFF_MV, ML_WIDTH).astype(BF16)
    mo_ref[...] = _sigmoid(proj(OFF_MO, ML_WIDTH)).astype(BF16)

    gt = _dot_nt(wgt_ref[...], xb) + bgt_ref[...]
    sub = lax.broadcasted_iota(jnp.int32, gt.shape, 0)
    is_fgate = (sub >= GATE_TILE) & (sub < GATE_TILE + ML_HEADS)
    grow_ref[...] = jnp.where(is_fgate, _log_sigmoid(gt), gt)


def _inproj(x2, w_bf, b_row, wgt, bgt, lb_logits, tm):
    t, d = x2.shape
    grid = (t // tm,)
    row = lambda i: (i, 0)
    const = lambda i: (0, 0)
    out_shapes = (
        jax.ShapeDtypeStruct((t, HG_WIDTH), F32),
        jax.ShapeDtypeStruct((t, HG_WIDTH), BF16),
        jax.ShapeDtypeStruct((t, HG_WIDTH), BF16),
        jax.ShapeDtypeStruct((t, HG_WIDTH), BF16),
        jax.ShapeDtypeStruct((t, HG_WIDTH), BF16),
        jax.ShapeDtypeStruct((t, ML_QK_WIDTH), BF16),
        jax.ShapeDtypeStruct((t, ML_QK_WIDTH), BF16),
        jax.ShapeDtypeStruct((t, ML_WIDTH), BF16),
        jax.ShapeDtypeStruct((t, ML_WIDTH), BF16),
        jax.ShapeDtypeStruct((GATE_ROWS, t), F32),
    )
    out_specs = (
        pl.BlockSpec((tm, HG_WIDTH), row), pl.BlockSpec((tm, HG_WIDTH), row),
        pl.BlockSpec((tm, HG_WIDTH), row), pl.BlockSpec((tm, HG_WIDTH), row),
        pl.BlockSpec((tm, HG_WIDTH), row), pl.BlockSpec((tm, ML_QK_WIDTH), row),
        pl.BlockSpec((tm, ML_QK_WIDTH), row), pl.BlockSpec((tm, ML_WIDTH), row),
        pl.BlockSpec((tm, ML_WIDTH), row),
        pl.BlockSpec((GATE_ROWS, tm), lambda i: (0, i)),
    )
    in_specs = [
        pl.BlockSpec((tm, d), row),
        pl.BlockSpec(w_bf.shape, const),
        pl.BlockSpec(b_row.shape, const),
        pl.BlockSpec(wgt.shape, const),
        pl.BlockSpec(bgt.shape, const),
        pl.BlockSpec(lb_logits.shape, const),
    ]
    return pl.pallas_call(
        _inproj_kernel, grid=grid, in_specs=in_specs, out_specs=out_specs, out_shape=out_shapes,
        compiler_params=pltpu.CompilerParams(dimension_semantics=("arbitrary",),
                                             vmem_limit_bytes=VMEM_LIMIT),
        name="inproj",
    )(x2, w_bf, b_row, wgt, bgt, lb_logits)


def _hgrn2_select():
    sel = np.zeros((EX, HG_DK, CHUNK), np.float32)
    for j in range(EX):
        sel[j, :, j::EX] = 1.0
    return jnp.asarray(sel.reshape(EX * HG_DK, CHUNK), BF16)


def _chunk_tril(ts):
    idx = np.arange(ts)
    return jnp.asarray((idx[:, None] >= idx[None, :]) & (idx[:, None] // CHUNK == idx[None, :] // CHUNK), BF16)


def _hgrn2_phases(logf_ref, q_ref, k_ref, v_ref, og_ref, gn_ref, sel_ref, tril_ref, o_ref,
                  st_ref, b_scr, c_scr, n_chunks):
    row = lax.broadcasted_iota(jnp.int32, (CHUNK, CHUNK), 0)
    col = lax.broadcasted_iota(jnp.int32, (CHUNK, CHUNK), 1)
    exact_mask = (col <= row) & (row // EX == col // EX)
    pair_mask = row // SUB == col // SUB
    env = {}

    def blocks(vals):
        return _rows([jnp.broadcast_to(jnp.asarray(x, F32), (EX, HG_WIDTH)) for x in vals])

    def setup():
        b_scr[...] = _cumsum_rows(tril_ref[...], logf_ref[...]) * LOG2E
        c_scr[...] = b_scr[...] - jnp.log2(k_ref[...].astype(F32))
        env["states"] = [st_ref[h] for h in range(HG_HEADS)]

    def scores(c):
        r0 = c * CHUNK
        rows = pl.ds(r0, CHUNK)
        b2 = b_scr[rows, :]
        c2 = c_scr[rows, :]
        qf = q_ref[rows, :].astype(F32)

        bd = [jnp.zeros((1, HG_WIDTH), F32)]
        bd += [b_scr[pl.ds(r0 + EX * m - 1, 1), :] for m in range(1, N_EX + 1)]
        per_sub = EX_PER_SUB
        sub_start = blocks([bd[(m // per_sub) * per_sub] for m in range(N_EX)])
        q_st = qf * jnp.exp2(b2 - sub_start)
        q_in = (q_st * jnp.exp2(sub_start)).astype(BF16)
        k_dec = jnp.exp2(bd[N_EX] - c2).astype(BF16)
        chunk_decay = jnp.exp2(bd[N_EX])
        q_half = qf * jnp.exp2(b2 - blocks([bd[m] if m % per_sub else jnp.inf for m in range(N_EX)]))
        k_half = jnp.exp2(blocks([-jnp.inf if m % per_sub else bd[m + 1] for m in range(N_EX)]) - c2)

        prods = []
        for h in range(HG_HEADS):
            hs = slice(h * HG_DK, (h + 1) * HG_DK)
            per_key = []
            for j in range(EX):
                c_j = _rows([jnp.broadcast_to(c_scr[pl.ds(r0 + m * EX + j, 1), hs], (EX, HG_DK))
                            for m in range(N_EX)])
                per_key.append((qf[:, hs] * jnp.exp2(jnp.minimum(b2[:, hs] - c_j, 0.0))).astype(BF16))
            prods.append(jnp.concatenate(per_key, axis=1))
        exact_all = _dot(_rows(prods), sel_ref[...])

        a_heads = []
        for h in range(HG_HEADS):
            hs = slice(h * HG_DK, (h + 1) * HG_DK)
            exact = exact_all[h * CHUNK:(h + 1) * CHUNK, :]
            half = _dot_nt(q_half[:, hs].astype(BF16), k_half[:, hs].astype(BF16))
            off = [jnp.zeros((SUB, CHUNK), F32)]
            for i in range(1, N_SUB):
                k_i = jnp.exp2(bd[i * per_sub][:, hs] - c2[0:i * SUB, hs])
                k_i = _rows([k_i, jnp.zeros((CHUNK - i * SUB, HG_DK), F32)])
                off.append(_dot_nt(q_st[i * SUB:(i + 1) * SUB, hs].astype(BF16), k_i.astype(BF16)))
            a = jnp.where(exact_mask, exact, jnp.where(pair_mask, half, 0.0)) + _rows(off)
            a_heads.append(a.astype(BF16))
        env[c] = (a_heads, q_in, k_dec, chunk_decay)

    def outputs(c):
        a_heads, q_in, k_dec, chunk_decay = env.pop(c)
        states = env["states"]
        rows = pl.ds(c * CHUNK, CHUNK)
        v = v_ref[rows, :]
        outs = []
        for h in range(HG_HEADS):
            hs = slice(h * HG_DK, (h + 1) * HG_DK)
            st = states[h]
            o_h = _dot(a_heads[h], v[:, hs]) + _dot_nt(q_in[:, hs], st.astype(BF16))
            states[h] = st * chunk_decay[:, hs] + _dot_tn(v[:, hs], k_dec[:, hs])
            outs.append(_head_rms(o_h, HG_DV))
        o = jnp.concatenate(outs, axis=-1)
        o_ref[rows, :] = (o * gn_ref[...] * og_ref[rows, :].astype(F32)).astype(o_ref.dtype)

    def finish():
        for h in range(HG_HEADS):
            st_ref[h] = env["states"][h]

    phases = [setup, functools.partial(scores, 0)]
    for c in range(n_chunks):
        step = [functools.partial(scores, c + 1)] if c + 1 < n_chunks else []
        step.append(functools.partial(outputs, c))
        phases.append(lambda step=step: [f() for f in step])
    phases.append(finish)
    return phases


CONV_PAD = 8
SEL_ROWS = 128
GRP_M, GRP_WI, GRP_EN, GRP_ONE = 0, 32, 64, 96
ML_AUG = 2 * ML_DV


def _mlstm_constants(tl):
    ident = np.eye(tl, dtype=np.float32)
    triu = np.triu(np.ones((tl, tl), np.float32))
    bias = np.where(np.tril(np.ones((tl, tl), bool)), 0.0, -np.inf).astype(np.float32)
    sel_d = np.zeros((GRP_ONE, ML_HEADS * tl), np.float32)
    sel_w = np.zeros((SEL_ROWS, ML_HEADS * ML_AUG), np.float32)
    for h in range(ML_HEADS):
        for k in range(3):
            sel_d[GRP_M + GATE_TILE * k + h, h * tl:(h + 1) * tl] = -1.0
            sel_w[GRP_WI + GATE_TILE * k + h, h * ML_AUG:h * ML_AUG + ML_DV] = 1.0
            sel_w[GRP_EN + GATE_TILE * k + h, h * ML_AUG + ML_DV:(h + 1) * ML_AUG] = 1.0
    return (jnp.asarray(ident, BF16), jnp.asarray(triu, BF16), jnp.asarray(bias),
            jnp.asarray(sel_d, BF16), jnp.asarray(sel_w, BF16))


def _mlstm_phases(mq_ref, mk_ref, mv_ref, og_ref, grow_ref, cw_ref, cb_ref, gn_ref,
                  ident_ref, triu_ref, bias_ref, seld_ref, selw_ref, o_ref,
                  cbuf, c_ref, m_ref, tl):
    env = {}

    def scalars():
        cbuf[CONV_PAD:CONV_PAD + tl, 0:ML_QK_WIDTH] = mq_ref[...].astype(F32)
        cbuf[CONV_PAD:CONV_PAD + tl, ML_QK_WIDTH:] = mk_ref[...].astype(F32)
        acc = cb_ref[...] + cw_ref[0:1, :] * cbuf[pl.ds(CONV_PAD - (CONV_K - 1), tl), :]
        for tap in range(1, CONV_K):
            acc = acc + cw_ref[tap:tap + 1, :] * cbuf[pl.ds(CONV_PAD - (CONV_K - 1) + tap, tl), :]
        qk = _silu(acc)
        cbuf[0:CONV_PAD, :] = cbuf[tl:tl + CONV_PAD, :]
        env["q"] = (qk[:, 0:ML_QK_WIDTH] * (ML_DQK ** -0.5)).astype(BF16)
        k = qk[:, ML_QK_WIDTH:]
        env["k"] = k.astype(BF16)
        env["k_t"] = k.T

        gates = grow_ref[...]
        i_g = gates[0:GATE_TILE, :]
        hi, mid, lo = _split3(gates[GATE_TILE:, :])
        triu = triu_ref[...]
        part = _dot(_rows([hi, mid]), triu)
        g = part[0:GATE_TILE, :] + part[GATE_TILE:, :] + _dot(_rows([lo, lo]), triu)[0:GATE_TILE, :]
        u = i_g - g
        lane = lax.broadcasted_iota(jnp.int32, (GATE_TILE, tl), 1)
        cm = u
        shift = 1
        while shift < tl:
            cm = jnp.maximum(cm, jnp.where(lane >= shift, pltpu.roll(cm, shift, axis=1), -jnp.inf))
            shift *= 2
        m_prev = m_ref[...]
        m_run = jnp.maximum(m_prev, cm)
        w_inter = jnp.exp(m_prev - m_run)
        e_negm = jnp.exp(-(g + m_run))
        m_last = jnp.broadcast_to(m_run[:, tl - 1:tl], (GATE_TILE, tl))
        g_last = jnp.broadcast_to(g[:, tl - 1:tl], (GATE_TILE, tl))
        env["w_s"] = jnp.exp(u - m_last)
        env["w_old"] = jnp.exp(m_prev - m_last)
        m_ref[...] = g_last + m_last

        zeros8 = jnp.zeros((GATE_TILE, tl), F32)
        ones8 = jnp.ones((GATE_TILE, tl), F32)
        env["pack"] = _rows(_split3_f32(m_run) + [zeros8] + _split3_f32(w_inter) + [zeros8]
                            + _split3_f32(e_negm) + [zeros8] + [ones8, ones8, ones8, zeros8]).astype(BF16)
        u_rows = _rows(_split3_f32(u) + [zeros8])
        head_of_row = lax.broadcasted_iota(jnp.int32, u_rows.shape, 0) % GATE_TILE
        env["u_blocks"] = jnp.concatenate(
            [jnp.where(head_of_row == h, u_rows, 0.0) for h in range(ML_HEADS)], axis=1).astype(BF16)

    def copies():
        pack_t = _dot_nt(ident_ref[...], env["pack"]).astype(BF16)
        env["dmat"] = _dot(pack_t, _rows([seld_ref[...], env["u_blocks"]]))
        env["wrep"] = _dot(pack_t, selw_ref[...])

    def head(h):
        qs = slice(h * ML_DQK, (h + 1) * ML_DQK)
        vs = slice(h * ML_DV, (h + 1) * ML_DV)
        q_bf, k_bf, wrep = env["q"], env["k"], env["wrep"]
        w_intra = jnp.exp(env["dmat"][:, h * tl:(h + 1) * tl] + bias_ref[...])
        p = (_dot_nt(q_bf[:, qs], k_bf[:, qs]) * w_intra).astype(BF16)
        v_aug = jnp.concatenate([mv_ref[:, vs], jnp.ones((tl, ML_DV), BF16)], axis=1)
        c_aug = c_ref[h]
        intra = _dot(p, v_aug)
        inter = _dot(q_bf[:, qs], c_aug.astype(BF16))
        wi_rep = wrep[:, h * ML_AUG:h * ML_AUG + ML_DV]
        en_rep = wrep[:, h * ML_AUG + ML_DV:(h + 1) * ML_AUG]
        num = intra[:, 0:ML_DV] + wi_rep * inter[:, 0:ML_DV]
        den = intra[:, ML_DV:] + wi_rep * inter[:, ML_DV:]
        hh = num / jnp.maximum(jnp.abs(den), en_rep)
        kw_t = (env["k_t"][qs, :] * env["w_s"][h:h + 1, :]).astype(BF16)
        c_ref[h] = env["w_old"][h:h + 1, 0:ML_AUG] * c_aug + _dot(kw_t, v_aug)
        o_ref[:, vs] = (_head_rms(hh, ML_DV) * gn_ref[:, vs] * og_ref[:, vs].astype(F32)).astype(o_ref.dtype)

    return [scalars, copies] + [functools.partial(head, h) for h in range(ML_HEADS)]


FF_TILE = 256


def _layer_norm(x, g, b):
    mu = jnp.mean(x, axis=-1, keepdims=True)
    xc = x - mu
    var = jnp.mean(xc * xc, axis=-1, keepdims=True)
    return xc * lax.rsqrt(var + LN_EPS) * g + b


def _norm_phases(ohg_ref, oml_ref, x_ref, wo_ref, ln1g_ref, ln1b_ref, h1_ref, alpha):
    env = {}

    def project():
        env["mix"] = _dot(ohg_ref[...], wo_ref[0:HG_WIDTH, :]) + _dot(oml_ref[...], wo_ref[HG_WIDTH:, :])

    def norm():
        h1_ref[...] = _layer_norm(alpha * x_ref[...] + env["mix"], ln1g_ref[...], ln1b_ref[...])

    return project, norm


def _ffn_phases(h1_ref, p_ref, wg_ref, wu_ref, wd_ref, ln2g_ref, ln2b_ref, wpp_ref, wpg_ref, bpg_ref,
                out_ref, alpha, d_ff):
    env = {}

    def ffn(lo):
        if lo == 0:
            env["h1b"] = h1_ref[...].astype(BF16)
            env["ffn"] = jnp.zeros(h1_ref.shape, F32)
        h1b = env["h1b"]
        gate = _dot(h1b, wg_ref[:, lo:lo + FF_TILE])
        up = _dot(h1b, wu_ref[:, lo:lo + FF_TILE])
        env["ffn"] = env["ffn"] + _dot((_silu(gate) * up).astype(BF16), wd_ref[lo:lo + FF_TILE, :])

    def norm():
        env["h2"] = _layer_norm(alpha * h1_ref[...] + env["ffn"], ln2g_ref[...], ln2b_ref[...])

    def embed():
        h2 = env["h2"]
        pgate = _sigmoid(_dot(h2.astype(BF16), wpg_ref[...]) + bpg_ref[...])
        pemb = _dot(p_ref[...].astype(BF16), wpp_ref[...])
        out_ref[...] = h2 + pgate * pemb

    return [functools.partial(ffn, lo) for lo in range(0, d_ff, FF_TILE)], norm, embed


N_MIX_IN = 10
N_MIX_CONST = 11
N_TAIL_CONST = 11


def _mixer_groups(hg, ml):
    hg_setup, *hg_chunks, hg_finish = hg
    ml_scalars, ml_copies, *ml_heads = ml
    groups = [[hg_setup, ml_scalars], hg_chunks[:1], [ml_copies] + hg_chunks[1:2]]
    rest_hg, rest_ml = hg_chunks[2:], ml_heads
    for j in range(max(len(rest_hg), len(rest_ml))):
        groups.append(rest_ml[j:j + 1] + rest_hg[j:j + 1])
    groups[-1].append(hg_finish)
    return groups


def _issue_order(ffn, groups):
    order = []
    for j, f in enumerate(ffn):
        order.append(f)
        lo = j * len(groups) // len(ffn)
        hi = (j + 1) * len(groups) // len(ffn)
        for group in groups[lo:hi]:
            order.extend(group)
    return order


def _fused_kernel(*refs, tl, sub, tiles_per_seq, alpha, d_ff):
    it = iter(refs)
    take = lambda n: [next(it) for _ in range(n)]
    (logf_ref, hq_ref, hk_ref, hv_ref, hg_ref, mq_ref, mk_ref, mv_ref, mo_ref, grow_ref) = take(N_MIX_IN)
    (hgn_ref, sel_ref, tril_ref, cw_ref, cb_ref, mgn_ref,
     ident_ref, triu_ref, bias_ref, seld_ref, selw_ref) = take(N_MIX_CONST)
    x_ref, p_ref = take(2)
    (wo_ref, ln1g_ref, ln1b_ref, wg_ref, wu_ref, wd_ref,
     ln2g_ref, ln2b_ref, wpp_ref, wpg_ref, bpg_ref) = take(N_TAIL_CONST)
    (out_ref,) = take(1)
    st_ref, b_scr, c_scr, cbuf, c_ref, m_ref, ohg_scr, oml_scr, h1_scr = take(9)

    g = pl.program_id(0)

    @pl.when(g == 0)
    def _():
        ohg_scr[...] = jnp.zeros_like(ohg_scr)
        oml_scr[...] = jnp.zeros_like(oml_scr)
        h1_scr[...] = jnp.zeros_like(h1_scr)

    @pl.when(g % tiles_per_seq == 0)
    def _():
        st_ref[...] = jnp.zeros_like(st_ref)
        cbuf[0:CONV_PAD, :] = jnp.zeros((CONV_PAD, 2 * ML_QK_WIDTH), F32)
        c_ref[...] = jnp.zeros_like(c_ref)
        m_ref[...] = jnp.zeros_like(m_ref)

    cur = g % 2
    prev = 1 - cur
    groups = []
    for off in range(0, tl, sub):
        rows = pl.ds(off, sub)
        at = lambda ref: ref.at[rows]
        hg_phases = _hgrn2_phases(at(logf_ref), at(hq_ref), at(hk_ref), at(hv_ref), at(hg_ref), hgn_ref,
                                  sel_ref, tril_ref, ohg_scr.at[cur, rows], st_ref, b_scr, c_scr, sub // CHUNK)
        ml_phases = _mlstm_phases(at(mq_ref), at(mk_ref), at(mv_ref), at(mo_ref), grow_ref.at[:, rows],
                                  cw_ref, cb_ref, mgn_ref, ident_ref, triu_ref, bias_ref, seld_ref, selw_ref,
                                  oml_scr.at[cur, rows], cbuf, c_ref, m_ref, sub)
        groups += _mixer_groups(hg_phases, ml_phases)
    project, norm1 = _norm_phases(ohg_scr.at[prev], oml_scr.at[prev], x_ref, wo_ref, ln1g_ref, ln1b_ref,
                                  h1_scr.at[prev], alpha)
    ffn, norm2, embed = _ffn_phases(h1_scr.at[cur], p_ref, wg_ref, wu_ref, wd_ref, ln2g_ref, ln2b_ref,
                                    wpp_ref, wpg_ref, bpg_ref, out_ref, alpha, d_ff)
    for f in _issue_order(ffn, groups) + [project, norm2, embed, norm1]:
        f()


def _fused(mix_in, hgn, conv_w, conv_b, mgn, x2, p2, tail_consts, batch, seq, alpha):
    t, d = x2.shape
    tl, sub = TAIL_TILE, MIX_TILE
    assert seq % tl == 0 and tl % sub == 0 and sub % CHUNK == 0 and sub >= ML_AUG
    tiles_per_seq = seq // tl
    n_tiles = t // tl
    d_ff = tail_consts[3].shape[1]
    assert d_ff % FF_TILE == 0
    mix_consts = (hgn, _hgrn2_select(), _chunk_tril(sub), conv_w, conv_b, mgn) + _mlstm_constants(sub)
    assert len(mix_in) == N_MIX_IN and len(mix_consts) == N_MIX_CONST and len(tail_consts) == N_TAIL_CONST

    def tile(lag):
        return lambda g: jnp.clip(g - lag, 0, n_tiles - 1)

    def const(a):
        return pl.BlockSpec(a.shape, lambda g: (0,) * a.ndim, pipeline_mode=pl.Buffered(1))

    mix_specs = [pl.BlockSpec((tl, a.shape[1]), lambda g: (tile(0)(g), 0)) for a in mix_in[:-1]]
    mix_specs.append(pl.BlockSpec((GATE_ROWS, tl), lambda g: (0, tile(0)(g))))
    in_specs = (mix_specs + [const(a) for a in mix_consts]
                + [pl.BlockSpec((tl, d), lambda g: (tile(1)(g), 0)),
                   pl.BlockSpec((tl, p2.shape[1]), lambda g: (tile(2)(g), 0))]
                + [const(a) for a in tail_consts])
    return pl.pallas_call(
        functools.partial(_fused_kernel, tl=tl, sub=sub, tiles_per_seq=tiles_per_seq, alpha=alpha, d_ff=d_ff),
        grid=(n_tiles + 2,),
        in_specs=in_specs,
        out_specs=pl.BlockSpec((tl, d), lambda g: (tile(2)(g), 0)),
        out_shape=jax.ShapeDtypeStruct((t, d), F32),
        scratch_shapes=[pltpu.VMEM((HG_HEADS, HG_DV, HG_DK), F32),
                        pltpu.VMEM((sub, HG_WIDTH), F32),
                        pltpu.VMEM((sub, HG_WIDTH), F32),
                        pltpu.VMEM((sub + CONV_PAD, 2 * ML_QK_WIDTH), F32),
                        pltpu.VMEM((ML_HEADS, ML_DQK, ML_AUG), F32),
                        pltpu.VMEM((GATE_TILE, sub), F32),
                        pltpu.VMEM((2, tl, HG_WIDTH), BF16),
                        pltpu.VMEM((2, tl, ML_WIDTH), BF16),
                        pltpu.VMEM((2, tl, d), F32)],
        compiler_params=pltpu.CompilerParams(dimension_semantics=("arbitrary",),
                                             vmem_limit_bytes=VMEM_LIMIT),
        name="mix_tail",
    )(*mix_in, *mix_consts, x2, p2, *tail_consts)


def _pick(n, candidates):
    for c in candidates:
        if n % c == 0:
            return c
    raise ValueError(f"no tile for {n}")


def kernel(x, p, w_in, b_in, hg_lb_logits, ml_conv_w, ml_conv_b, hg_norm_g, ml_norm_g, w_out, ln1_g, ln1_b,
           w_ffn_gate, w_ffn_up, w_ffn_down, ln2_g, ln2_b, ple_w_proj, ple_w_gate, ple_b_gate):
    batch, seq, d = x.shape
    depth = w_in.shape[0]
    t = batch * seq
    alpha = float((2 * depth) ** 0.25)
    tm = _pick(t, (512, 256))
    assert w_in.shape[2] == OFF_GATES + 2 * ML_HEADS
    assert depth == 1, "lower-bound cumsum is specialised to a single layer"

    x2 = x.reshape(t, d)
    for i in range(depth):
        w_i = w_in[i]
        w_bf = w_i[:, :OFF_GATES].astype(BF16)
        b_row = b_in[i, :OFF_GATES].reshape(1, OFF_GATES)
        wg_t = w_i[:, OFF_GATES:].T
        gate_pad = ((0, GATE_TILE - ML_HEADS), (0, 0))
        wgt = jnp.concatenate([jnp.pad(wg_t[:ML_HEADS], gate_pad), jnp.pad(wg_t[ML_HEADS:], gate_pad)]).astype(BF16)
        bg = b_in[i, OFF_GATES:].reshape(2 * ML_HEADS, 1)
        bgt = jnp.concatenate([jnp.pad(bg[:ML_HEADS], gate_pad), jnp.pad(bg[ML_HEADS:], gate_pad)])

        mix_in = _inproj(x2, w_bf, b_row, wgt, bgt, hg_lb_logits, tm)
        tail_consts = (w_out[i].astype(BF16), ln1_g[i].reshape(1, d), ln1_b[i].reshape(1, d),
                       w_ffn_gate[i].astype(BF16), w_ffn_up[i].astype(BF16), w_ffn_down[i].astype(BF16),
                       ln2_g[i].reshape(1, d), ln2_b[i].reshape(1, d),
                       ple_w_proj[i].astype(BF16), ple_w_gate[i].astype(BF16), ple_b_gate[i].reshape(1, d))
        x2 = _fused(mix_in, hg_norm_g[i].reshape(1, HG_WIDTH), ml_conv_w[i], ml_conv_b[i].reshape(1, -1),
                    ml_norm_g[i].reshape(1, ML_WIDTH), x2, p[i].reshape(t, -1), tail_consts, batch, seq, alpha)
    return x2.reshape(batch, seq, d)
```

```python
import functools

import numpy as np
import jax
import jax.numpy as jnp
from jax import lax
from jax.experimental import pallas as pl
from jax.experimental.pallas import tpu as pltpu

F32 = jnp.float32
BF16 = jnp.bfloat16

CHUNK = 64
SUB = 16
N_SUB = CHUNK // SUB
EX = 8
N_EX = CHUNK // EX
EX_PER_SUB = SUB // EX
LOG2E = 1.4426950408889634
HG_HEADS = 4
HG_DK = 128
HG_DV = 128
HG_WIDTH = HG_HEADS * HG_DV
ML_HEADS = 4
ML_DQK = 64
ML_DV = 128
ML_WIDTH = ML_HEADS * ML_DV
ML_QK_WIDTH = ML_HEADS * ML_DQK
CONV_K = 4
LN_EPS = 1e-5
RMS_EPS = 1e-6

OFF_HQ = 0
OFF_HF = OFF_HQ + HG_HEADS * HG_DK
OFF_HV = OFF_HF + HG_HEADS * HG_DK
OFF_HG = OFF_HV + HG_WIDTH
OFF_MQ = OFF_HG + HG_WIDTH
OFF_MK = OFF_MQ + ML_QK_WIDTH
OFF_MV = OFF_MK + ML_QK_WIDTH
OFF_MO = OFF_MV + ML_WIDTH
OFF_GATES = OFF_MO + ML_WIDTH
LANE = 128
GATE_TILE = 8
GATE_ROWS = 2 * GATE_TILE

VMEM_LIMIT = 60 * 1024 * 1024
MIX_TILE = 256
TAIL_TILE = 512


def _sigmoid(x):
    return 1.0 / (1.0 + jnp.exp(-x))


def _silu(x):
    return x * _sigmoid(x)


def _log_sigmoid(x):
    return jnp.minimum(x, 0.0) - jnp.log(1.0 + jnp.exp(-jnp.abs(x)))


def _split3(x):
    hi = x.astype(BF16)
    r1 = x - hi.astype(F32)
    mid = r1.astype(BF16)
    lo = (r1 - mid.astype(F32)).astype(BF16)
    return hi, mid, lo


def _split3_f32(x):
    return [s.astype(F32) for s in _split3(x)]


def _dot(a, b):
    return jnp.dot(a, b, preferred_element_type=F32)


def _dot_nt(a, b):
    return lax.dot_general(a, b, (((1,), (1,)), ((), ())), preferred_element_type=F32)


def _dot_tn(a, b):
    return lax.dot_general(a, b, (((0,), (0,)), ((), ())), preferred_element_type=F32)


def _rows(blocks):
    return jnp.concatenate(blocks, axis=0)


def _cumsum_rows(tril_bf, x):
    hi, mid, lo = _split3(x)
    return _dot(tril_bf, hi) + _dot(tril_bf, mid) + _dot(tril_bf, lo)


def _head_rms(o, width):
    ms = jnp.sum(o * o, axis=-1, keepdims=True) * (1.0 / width)
    return o * lax.rsqrt(ms + RMS_EPS)


def _inproj_kernel(x_ref, w_ref, b_ref, wgt_ref, bgt_ref, lbl_ref,
                   logf_ref, hq_ref, hk_ref, hv_ref, hg_ref,
                   mq_ref, mk_ref, mv_ref, mo_ref, grow_ref):
    xb = x_ref[...].astype(BF16)

    def proj(lo, width):
        return _dot(xb, w_ref[:, lo:lo + width]) + b_ref[:, lo:lo + width]

    hq_ref[...] = _silu(proj(OFF_HQ, HG_WIDTH)).astype(BF16)

    logits = lbl_ref[...]
    mx = jnp.max(logits, axis=0, keepdims=True)
    ex = jnp.exp(logits - mx)
    den = jnp.sum(ex, axis=0, keepdims=True)
    lb = ex[0:1, :] / den
    one_m_lb = (den - ex[0:1, :]) / den
    sig = _sigmoid(proj(OFF_HF, HG_WIDTH))
    logf_ref[...] = jnp.log(lb + one_m_lb * sig)
    hk_ref[...] = (one_m_lb * (1.0 - sig)).astype(BF16)

    hv_ref[...] = proj(OFF_HV, HG_WIDTH).astype(BF16)
    hg_ref[...] = _silu(proj(OFF_HG, HG_WIDTH)).astype(BF16)
    mq_ref[...] = proj(OFF_MQ, ML_QK_WIDTH).astype(BF16)
    mk_ref[...] = proj(OFF_MK, ML_QK_WIDTH).astype(BF16)
    mv_ref[...] = proj(OFF_MV, ML_WIDTH).astype(BF16)
    mo_ref[...] = _sigmoid(proj(OFF_MO, ML_WIDTH)).astype(BF16)

    gt = _dot_nt(wgt_ref[...], xb) + bgt_ref[...]
    sub = lax.broadcasted_iota(jnp.int32, gt.shape, 0)
    is_fgate = (sub >= GATE_TILE) & (sub < GATE_TILE + ML_HEADS)
    grow_ref[...] = jnp.where(is_fgate, _log_sigmoid(gt), gt)


def _inproj(x2, w_bf, b_row, wgt, bgt, lb_logits, tm):
    t, d = x2.shape
    grid = (t // tm,)
    row = lambda i: (i, 0)
    const = lambda i: (0, 0)
    out_shapes = (
        jax.ShapeDtypeStruct((t, HG_WIDTH), F32),
        jax.ShapeDtypeStruct((t, HG_WIDTH), BF16),
        jax.ShapeDtypeStruct((t, HG_WIDTH), BF16),
        jax.ShapeDtypeStruct((t, HG_WIDTH), BF16),
        jax.ShapeDtypeStruct((t, HG_WIDTH), BF16),
        jax.ShapeDtypeStruct((t, ML_QK_WIDTH), BF16),
        jax.ShapeDtypeStruct((t, ML_QK_WIDTH), BF16),
        jax.ShapeDtypeStruct((t, ML_WIDTH), BF16),
        jax.ShapeDtypeStruct((t, ML_WIDTH), BF16),
        jax.ShapeDtypeStruct((GATE_ROWS, t), F32),
    )
    out_specs = (
        pl.BlockSpec((tm, HG_WIDTH), row), pl.BlockSpec((tm, HG_WIDTH), row),
        pl.BlockSpec((tm, HG_WIDTH), row), pl.BlockSpec((tm, HG_WIDTH), row),
        pl.BlockSpec((tm, HG_WIDTH), row), pl.BlockSpec((tm, ML_QK_WIDTH), row),
        pl.BlockSpec((tm, ML_QK_WIDTH), row), pl.BlockSpec((tm, ML_WIDTH), row),
        pl.BlockSpec((tm, ML_WIDTH), row),
        pl.BlockSpec((GATE_ROWS, tm), lambda i: (0, i)),
    )
    in_specs = [
        pl.BlockSpec((tm, d), row),
        pl.BlockSpec(w_bf.shape, const, pipeline_mode=pl.Buffered(1)),
        pl.BlockSpec(b_row.shape, const),
        pl.BlockSpec(wgt.shape, const),
        pl.BlockSpec(bgt.shape, const),
        pl.BlockSpec(lb_logits.shape, const),
    ]
    return pl.pallas_call(
        _inproj_kernel, grid=grid, in_specs=in_specs, out_specs=out_specs, out_shape=out_shapes,
        compiler_params=pltpu.CompilerParams(dimension_semantics=("arbitrary",),
                                             vmem_limit_bytes=VMEM_LIMIT),
        name="inproj",
    )(x2, w_bf, b_row, wgt, bgt, lb_logits)


def _hgrn2_select():
    sel = np.zeros((EX, HG_DK, CHUNK), np.float32)
    for j in range(EX):
        sel[j, :, j::EX] = 1.0
    return jnp.asarray(sel.reshape(EX * HG_DK, CHUNK), BF16)


def _chunk_tril(ts):
    idx = np.arange(ts)
    return jnp.asarray((idx[:, None] >= idx[None, :]) & (idx[:, None] // CHUNK == idx[None, :] // CHUNK), BF16)


def _hgrn2_phases(logf_ref, q_ref, k_ref, v_ref, og_ref, gn_ref, sel_ref, tril_ref, o_ref,
                  st_ref, b_scr, c_scr, n_chunks):
    row = lax.broadcasted_iota(jnp.int32, (CHUNK, CHUNK), 0)
    col = lax.broadcasted_iota(jnp.int32, (CHUNK, CHUNK), 1)
    exact_mask = (col <= row) & (row // EX == col // EX)
    pair_mask = row // SUB == col // SUB
    env = {}

    def blocks(vals):
        return _rows([jnp.broadcast_to(jnp.asarray(x, F32), (EX, HG_WIDTH)) for x in vals])

    def setup():
        b_scr[...] = _cumsum_rows(tril_ref[...], logf_ref[...]) * LOG2E
        c_scr[...] = b_scr[...] - jnp.log2(k_ref[...].astype(F32))
        env["states"] = [st_ref[h] for h in range(HG_HEADS)]

    def scores(c):
        r0 = c * CHUNK
        rows = pl.ds(r0, CHUNK)
        b2 = b_scr[rows, :]
        c2 = c_scr[rows, :]
        qf = q_ref[rows, :].astype(F32)

        bd = [jnp.zeros((1, HG_WIDTH), F32)]
        bd += [b_scr[pl.ds(r0 + EX * m - 1, 1), :] for m in range(1, N_EX + 1)]
        per_sub = EX_PER_SUB
        sub_start = blocks([bd[(m // per_sub) * per_sub] for m in range(N_EX)])
        q_st = qf * jnp.exp2(b2 - sub_start)
        q_in = (q_st * jnp.exp2(sub_start)).astype(BF16)
        k_dec = jnp.exp2(bd[N_EX] - c2).astype(BF16)
        chunk_decay = jnp.exp2(bd[N_EX])
        q_half = qf * jnp.exp2(b2 - blocks([bd[m] if m % per_sub else jnp.inf for m in range(N_EX)]))
        k_half = jnp.exp2(blocks([-jnp.inf if m % per_sub else bd[m + 1] for m in range(N_EX)]) - c2)

        prods = []
        for h in range(HG_HEADS):
            hs = slice(h * HG_DK, (h + 1) * HG_DK)
            per_key = []
            for j in range(EX):
                c_j = _rows([jnp.broadcast_to(c_scr[pl.ds(r0 + m * EX + j, 1), hs], (EX, HG_DK))
                            for m in range(N_EX)])
                per_key.append((qf[:, hs] * jnp.exp2(jnp.minimum(b2[:, hs] - c_j, 0.0))).astype(BF16))
            prods.append(jnp.concatenate(per_key, axis=1))
        exact_all = _dot(_rows(prods), sel_ref[...])

        a_heads = []
        for h in range(HG_HEADS):
            hs = slice(h * HG_DK, (h + 1) * HG_DK)
            exact = exact_all[h * CHUNK:(h + 1) * CHUNK, :]
            half = _dot_nt(q_half[:, hs].astype(BF16), k_half[:, hs].astype(BF16))
            off = [jnp.zeros((SUB, CHUNK), F32)]
            for i in range(1, N_SUB):
                k_i = jnp.exp2(bd[i * per_sub][:, hs] - c2[0:i * SUB, hs])
                k_i = _rows([k_i, jnp.zeros((CHUNK - i * SUB, HG_DK), F32)])
                off.append(_dot_nt(q_st[i * SUB:(i + 1) * SUB, hs].astype(BF16), k_i.astype(BF16)))
            a = jnp.where(exact_mask, exact, jnp.where(pair_mask, half, 0.0)) + _rows(off)
            a_heads.append(a.astype(BF16))
        env[c] = (a_heads, q_in, k_dec, chunk_decay)

    def outputs(c):
        a_heads, q_in, k_dec, chunk_decay = env.pop(c)
        states = env["states"]
        rows = pl.ds(c * CHUNK, CHUNK)
        v = v_ref[rows, :]
        outs = []
        for h in range(HG_HEADS):
            hs = slice(h * HG_DK, (h + 1) * HG_DK)
            st = states[h]
            o_h = _dot(a_heads[h], v[:, hs]) + _dot_nt(q_in[:, hs], st.astype(BF16))
            states[h] = st * chunk_decay[:, hs] + _dot_tn(v[:, hs], k_dec[:, hs])
            outs.append(_head_rms(o_h, HG_DV))
        o = jnp.concatenate(outs, axis=-1)
        o_ref[rows, :] = (o * gn_ref[...] * og_ref[rows, :].astype(F32)).astype(o_ref.dtype)

    def finish():
        for h in range(HG_HEADS):
            st_ref[h] = env["states"][h]

    phases = [setup, functools.partial(scores, 0)]
    for c in range(n_chunks):
        step = [functools.partial(scores, c + 1)] if c + 1 < n_chunks else []
        step.append(functools.partial(outputs, c))
        phases.append(lambda step=step: [f() for f in step])
    phases.append(finish)
    return phases


CONV_PAD = 8
SEL_ROWS = 128
GRP_M, GRP_WI, GRP_EN, GRP_ONE = 0, 32, 64, 96
ML_AUG = 2 * ML_DV


def _mlstm_constants(tl):
    ident = np.eye(tl, dtype=np.float32)
    triu = np.triu(np.ones((tl, tl), np.float32))
    bias = np.where(np.tril(np.ones((tl, tl), bool)), 0.0, -np.inf).astype(np.float32)
    sel_d = np.zeros((GRP_ONE, ML_HEADS * tl), np.float32)
    sel_w = np.zeros((SEL_ROWS, ML_HEADS * ML_AUG), np.float32)
    for h in range(ML_HEADS):
        for k in range(3):
            sel_d[GRP_M + GATE_TILE * k + h, h * tl:(h + 1) * tl] = -1.0
            sel_w[GRP_WI + GATE_TILE * k + h, h * ML_AUG:h * ML_AUG + ML_DV] = 1.0
            sel_w[GRP_EN + GATE_TILE * k + h, h * ML_AUG + ML_DV:(h + 1) * ML_AUG] = 1.0
    return (jnp.asarray(ident, BF16), jnp.asarray(triu, BF16), jnp.asarray(bias),
            jnp.asarray(sel_d, BF16), jnp.asarray(sel_w, BF16))


def _mlstm_phases(mq_ref, mk_ref, mv_ref, og_ref, grow_ref, cw_ref, cb_ref, gn_ref,
                  ident_ref, triu_ref, bias_ref, seld_ref, selw_ref, o_ref,
                  cbuf, c_ref, m_ref, tl):
    env = {}

    def scalars():
        cbuf[CONV_PAD:CONV_PAD + tl, 0:ML_QK_WIDTH] = mq_ref[...].astype(F32)
        cbuf[CONV_PAD:CONV_PAD + tl, ML_QK_WIDTH:] = mk_ref[...].astype(F32)
        acc = cb_ref[...] + cw_ref[0:1, :] * cbuf[pl.ds(CONV_PAD - (CONV_K - 1), tl), :]
        for tap in range(1, CONV_K):
            acc = acc + cw_ref[tap:tap + 1, :] * cbuf[pl.ds(CONV_PAD - (CONV_K - 1) + tap, tl), :]
        qk = _silu(acc)
        cbuf[0:CONV_PAD, :] = cbuf[tl:tl + CONV_PAD, :]
        env["q"] = (qk[:, 0:ML_QK_WIDTH] * (ML_DQK ** -0.5)).astype(BF16)
        k = qk[:, ML_QK_WIDTH:]
        env["k"] = k.astype(BF16)
        env["k_t"] = k.T

        gates = grow_ref[...]
        i_g = gates[0:GATE_TILE, :]
        hi, mid, lo = _split3(gates[GATE_TILE:, :])
        triu = triu_ref[...]
        part = _dot(_rows([hi, mid]), triu)
        g = part[0:GATE_TILE, :] + part[GATE_TILE:, :] + _dot(_rows([lo, lo]), triu)[0:GATE_TILE, :]
        u = i_g - g
        lane = lax.broadcasted_iota(jnp.int32, (GATE_TILE, tl), 1)
        cm = u
        shift = 1
        while shift < tl:
            cm = jnp.maximum(cm, jnp.where(lane >= shift, pltpu.roll(cm, shift, axis=1), -jnp.inf))
            shift *= 2
        m_prev = m_ref[...]
        m_run = jnp.maximum(m_prev, cm)
        w_inter = jnp.exp(m_prev - m_run)
        e_negm = jnp.exp(-(g + m_run))
        m_last = jnp.broadcast_to(m_run[:, tl - 1:tl], (GATE_TILE, tl))
        g_last = jnp.broadcast_to(g[:, tl - 1:tl], (GATE_TILE, tl))
        env["w_s"] = jnp.exp(u - m_last)
        env["w_old"] = jnp.exp(m_prev - m_last)
        m_ref[...] = g_last + m_last

        zeros8 = jnp.zeros((GATE_TILE, tl), F32)
        ones8 = jnp.ones((GATE_TILE, tl), F32)
        env["pack"] = _rows(_split3_f32(m_run) + [zeros8] + _split3_f32(w_inter) + [zeros8]
                            + _split3_f32(e_negm) + [zeros8] + [ones8, ones8, ones8, zeros8]).astype(BF16)
        u_rows = _rows(_split3_f32(u) + [zeros8])
        head_of_row = lax.broadcasted_iota(jnp.int32, u_rows.shape, 0) % GATE_TILE
        env["u_blocks"] = jnp.concatenate(
            [jnp.where(head_of_row == h, u_rows, 0.0) for h in range(ML_HEADS)], axis=1).astype(BF16)

    def copies():
        pack_t = _dot_nt(ident_ref[...], env["pack"]).astype(BF16)
        env["dmat"] = _dot(pack_t, _rows([seld_ref[...], env["u_blocks"]]))
        env["wrep"] = _dot(pack_t, selw_ref[...])

    def head(h):
        qs = slice(h * ML_DQK, (h + 1) * ML_DQK)
        vs = slice(h * ML_DV, (h + 1) * ML_DV)
        q_bf, k_bf, wrep = env["q"], env["k"], env["wrep"]
        w_intra = jnp.exp(env["dmat"][:, h * tl:(h + 1) * tl] + bias_ref[...])
        p = (_dot_nt(q_bf[:, qs], k_bf[:, qs]) * w_intra).astype(BF16)
        v_aug = jnp.concatenate([mv_ref[:, vs], jnp.ones((tl, ML_DV), BF16)], axis=1)
        c_aug = c_ref[h]
        intra = _dot(p, v_aug)
        inter = _dot(q_bf[:, qs], c_aug.astype(BF16))
        wi_rep = wrep[:, h * ML_AUG:h * ML_AUG + ML_DV]
        en_rep = wrep[:, h * ML_AUG + ML_DV:(h + 1) * ML_AUG]
        num = intra[:, 0:ML_DV] + wi_rep * inter[:, 0:ML_DV]
        den = intra[:, ML_DV:] + wi_rep * inter[:, ML_DV:]
        hh = num / jnp.maximum(jnp.abs(den), en_rep)
        kw_t = (env["k_t"][qs, :] * env["w_s"][h:h + 1, :]).astype(BF16)
        c_ref[h] = env["w_old"][h:h + 1, 0:ML_AUG] * c_aug + _dot(kw_t, v_aug)
        o_ref[:, vs] = (_head_rms(hh, ML_DV) * gn_ref[:, vs] * og_ref[:, vs].astype(F32)).astype(o_ref.dtype)

    return [scalars, copies] + [functools.partial(head, h) for h in range(ML_HEADS)]


FF_TILE = 256


def _layer_norm(x, g, b):
    mu = jnp.mean(x, axis=-1, keepdims=True)
    xc = x - mu
    var = jnp.mean(xc * xc, axis=-1, keepdims=True)
    return xc * lax.rsqrt(var + LN_EPS) * g + b


def _norm_phases(ohg_ref, oml_ref, x_ref, wo_ref, ln1g_ref, ln1b_ref, h1_ref, alpha):
    env = {}

    def project():
        env["mix"] = _dot(ohg_ref[...], wo_ref[0:HG_WIDTH, :]) + _dot(oml_ref[...], wo_ref[HG_WIDTH:, :])

    def norm():
        h1_ref[...] = _layer_norm(alpha * x_ref[...] + env["mix"], ln1g_ref[...], ln1b_ref[...])

    return project, norm


def _ffn_phases(h1_ref, p_ref, wg_ref, wu_ref, wd_ref, ln2g_ref, ln2b_ref, wpp_ref, wpg_ref, bpg_ref,
                out_ref, act_scr, ffn_scr, alpha, d_ff):
    env = {}
    d = h1_ref.shape[-1]

    def hidden(lo):
        if lo == 0:
            env["h1b"] = h1_ref[...].astype(BF16)
        h1b = env["h1b"]
        gate = _dot(h1b, wg_ref[:, lo:lo + FF_TILE])
        up = _dot(h1b, wu_ref[:, lo:lo + FF_TILE])
        act_scr[:, lo:lo + FF_TILE] = (_silu(gate) * up).astype(BF16)

    def down(lo):
        ffn_scr[:, lo:lo + FF_TILE] = _dot(act_scr[...], wd_ref[:, lo:lo + FF_TILE])

    def norm():
        env["h2"] = _layer_norm(alpha * h1_ref[...] + ffn_scr[...], ln2g_ref[...], ln2b_ref[...])

    def embed():
        h2 = env["h2"]
        pgate = _sigmoid(_dot(h2.astype(BF16), wpg_ref[...]) + bpg_ref[...])
        pemb = _dot(p_ref[...].astype(BF16), wpp_ref[...])
        out_ref[...] = h2 + pgate * pemb

    tiles = ([functools.partial(hidden, lo) for lo in range(0, d_ff, FF_TILE)]
             + [functools.partial(down, lo) for lo in range(0, d, FF_TILE)])
    return tiles, norm, embed


N_MIX_IN = 10
N_MIX_CONST = 11
N_TAIL_CONST = 11


def _mixer_groups(hg, ml):
    hg_setup, *hg_chunks, hg_finish = hg
    ml_scalars, ml_copies, *ml_heads = ml
    groups = [[hg_setup, ml_scalars], hg_chunks[:1], [ml_copies] + hg_chunks[1:2]]
    rest_hg, rest_ml = hg_chunks[2:], ml_heads
    for j in range(max(len(rest_hg), len(rest_ml))):
        groups.append(rest_ml[j:j + 1] + rest_hg[j:j + 1])
    groups[-1].append(hg_finish)
    return groups


def _issue_order(ffn, groups):
    order = []
    for j, f in enumerate(ffn):
        order.append(f)
        lo = j * len(groups) // len(ffn)
        hi = (j + 1) * len(groups) // len(ffn)
        for group in groups[lo:hi]:
            order.extend(group)
    return order


def _fused_kernel(*refs, tl, sub, tiles_per_seq, alpha, d_ff):
    it = iter(refs)
    take = lambda n: [next(it) for _ in range(n)]
    (logf_ref, hq_ref, hk_ref, hv_ref, hg_ref, mq_ref, mk_ref, mv_ref, mo_ref, grow_ref) = take(N_MIX_IN)
    (hgn_ref, sel_ref, tril_ref, cw_ref, cb_ref, mgn_ref,
     ident_ref, triu_ref, bias_ref, seld_ref, selw_ref) = take(N_MIX_CONST)
    x_ref, p_ref = take(2)
    (wo_ref, ln1g_ref, ln1b_ref, wg_ref, wu_ref, wd_ref,
     ln2g_ref, ln2b_ref, wpp_ref, wpg_ref, bpg_ref) = take(N_TAIL_CONST)
    (out_ref,) = take(1)
    st_ref, b_scr, c_scr, cbuf, c_ref, m_ref, ohg_scr, oml_scr, h1_scr, act_scr, ffn_scr = take(11)

    g = pl.program_id(0)

    @pl.when(g == 0)
    def _():
        ohg_scr[...] = jnp.zeros_like(ohg_scr)
        oml_scr[...] = jnp.zeros_like(oml_scr)
        h1_scr[...] = jnp.zeros_like(h1_scr)

    @pl.when(g % tiles_per_seq == 0)
    def _():
        st_ref[...] = jnp.zeros_like(st_ref)
        cbuf[0:CONV_PAD, :] = jnp.zeros((CONV_PAD, 2 * ML_QK_WIDTH), F32)
        c_ref[...] = jnp.zeros_like(c_ref)
        m_ref[...] = jnp.zeros_like(m_ref)

    cur = g % 2
    prev = 1 - cur
    groups = []
    for off in range(0, tl, sub):
        rows = pl.ds(off, sub)
        at = lambda ref: ref.at[rows]
        hg_phases = _hgrn2_phases(at(logf_ref), at(hq_ref), at(hk_ref), at(hv_ref), at(hg_ref), hgn_ref,
                                  sel_ref, tril_ref, ohg_scr.at[cur, rows], st_ref, b_scr, c_scr, sub // CHUNK)
        ml_phases = _mlstm_phases(at(mq_ref), at(mk_ref), at(mv_ref), at(mo_ref), grow_ref.at[:, rows],
                                  cw_ref, cb_ref, mgn_ref, ident_ref, triu_ref, bias_ref, seld_ref, selw_ref,
                                  oml_scr.at[cur, rows], cbuf, c_ref, m_ref, sub)
        groups += _mixer_groups(hg_phases, ml_phases)
    project, norm1 = _norm_phases(ohg_scr.at[prev], oml_scr.at[prev], x_ref, wo_ref, ln1g_ref, ln1b_ref,
                                  h1_scr.at[prev], alpha)
    ffn, norm2, embed = _ffn_phases(h1_scr.at[cur], p_ref, wg_ref, wu_ref, wd_ref, ln2g_ref, ln2b_ref,
                                    wpp_ref, wpg_ref, bpg_ref, out_ref, act_scr, ffn_scr, alpha, d_ff)
    for f in _issue_order(ffn, groups) + [project, norm2, embed, norm1]:
        f()


def _fused(mix_in, hgn, conv_w, conv_b, mgn, x2, p2, tail_consts, batch, seq, alpha):
    t, d = x2.shape
    tl, sub = TAIL_TILE, MIX_TILE
    assert seq % tl == 0 and tl % sub == 0 and sub % CHUNK == 0 and sub >= ML_AUG
    tiles_per_seq = seq // tl
    n_tiles = t // tl
    d_ff = tail_consts[3].shape[1]
    assert d_ff % FF_TILE == 0
    mix_consts = (hgn, _hgrn2_select(), _chunk_tril(sub), conv_w, conv_b, mgn) + _mlstm_constants(sub)
    assert len(mix_in) == N_MIX_IN and len(mix_consts) == N_MIX_CONST and len(tail_consts) == N_TAIL_CONST

    def tile(lag):
        return lambda g: jnp.clip(g - lag, 0, n_tiles - 1)

    def const(a):
        return pl.BlockSpec(a.shape, lambda g: (0,) * a.ndim, pipeline_mode=pl.Buffered(1))

    mix_specs = [pl.BlockSpec((tl, a.shape[1]), lambda g: (tile(0)(g), 0)) for a in mix_in[:-1]]
    mix_specs.append(pl.BlockSpec((GATE_ROWS, tl), lambda g: (0, tile(0)(g))))
    in_specs = (mix_specs + [const(a) for a in mix_consts]
                + [pl.BlockSpec((tl, d), lambda g: (tile(1)(g), 0)),
                   pl.BlockSpec((tl, p2.shape[1]), lambda g: (tile(2)(g), 0))]
                + [const(a) for a in tail_consts])
    return pl.pallas_call(
        functools.partial(_fused_kernel, tl=tl, sub=sub, tiles_per_seq=tiles_per_seq, alpha=alpha, d_ff=d_ff),
        grid=(n_tiles + 2,),
        in_specs=in_specs,
        out_specs=pl.BlockSpec((tl, d), lambda g: (tile(2)(g), 0)),
        out_shape=jax.ShapeDtypeStruct((t, d), F32),
        scratch_shapes=[pltpu.VMEM((HG_HEADS, HG_DV, HG_DK), F32),
                        pltpu.VMEM((sub, HG_WIDTH), F32),
                        pltpu.VMEM((sub, HG_WIDTH), F32),
                        pltpu.VMEM((sub + CONV_PAD, 2 * ML_QK_WIDTH), F32),
                        pltpu.VMEM((ML_HEADS, ML_DQK, ML_AUG), F32),
                        pltpu.VMEM((GATE_TILE, sub), F32),
                        pltpu.VMEM((2, tl, HG_WIDTH), BF16),
                        pltpu.VMEM((2, tl, ML_WIDTH), BF16),
                        pltpu.VMEM((2, tl, d), F32),
                        pltpu.VMEM((tl, d_ff), BF16),
                        pltpu.VMEM((tl, d), F32)],
        compiler_params=pltpu.CompilerParams(dimension_semantics=("arbitrary",),
                                             vmem_limit_bytes=VMEM_LIMIT),
        name="mix_tail",
    )(*mix_in, *mix_consts, x2, p2, *tail_consts)


def _pick(n, candidates):
    for c in candidates:
        if n % c == 0:
            return c
    raise ValueError(f"no tile for {n}")


def kernel(x, p, w_in, b_in, hg_lb_logits, ml_conv_w, ml_conv_b, hg_norm_g, ml_norm_g, w_out, ln1_g, ln1_b,
           w_ffn_gate, w_ffn_up, w_ffn_down, ln2_g, ln2_b, ple_w_proj, ple_w_gate, ple_b_gate):
    batch, seq, d = x.shape
    depth = w_in.shape[0]
    t = batch * seq
    alpha = float((2 * depth) ** 0.25)
    tm = _pick(t, (1024, 512, 256))
    assert w_in.shape[2] == OFF_GATES + 2 * ML_HEADS
    assert depth == 1, "lower-bound cumsum is specialised to a single layer"

    x2 = x.reshape(t, d)
    for i in range(depth):
        w_i = w_in[i]
        w_bf = w_i[:, :OFF_GATES].astype(BF16)
        b_row = b_in[i, :OFF_GATES].reshape(1, OFF_GATES)
        wg_t = w_i[:, OFF_GATES:].T
        gate_pad = ((0, GATE_TILE - ML_HEADS), (0, 0))
        wgt = jnp.concatenate([jnp.pad(wg_t[:ML_HEADS], gate_pad), jnp.pad(wg_t[ML_HEADS:], gate_pad)]).astype(BF16)
        bg = b_in[i, OFF_GATES:].reshape(2 * ML_HEADS, 1)
        bgt = jnp.concatenate([jnp.pad(bg[:ML_HEADS], gate_pad), jnp.pad(bg[ML_HEADS:], gate_pad)])

        mix_in = _inproj(x2, w_bf, b_row, wgt, bgt, hg_lb_logits, tm)
        tail_consts = (w_out[i].astype(BF16), ln1_g[i].reshape(1, d), ln1_b[i].reshape(1, d),
                       w_ffn_gate[i].astype(BF16), w_ffn_up[i].astype(BF16), w_ffn_down[i].astype(BF16),
                       ln2_g[i].reshape(1, d), ln2_b[i].reshape(1, d),
                       ple_w_proj[i].astype(BF16), ple_w_gate[i].astype(BF16), ple_b_gate[i].reshape(1, d))
        x2 = _fused(mix_in, hg_norm_g[i].reshape(1, HG_WIDTH), ml_conv_w[i], ml_conv_b[i].reshape(1, -1),
                    ml_norm_g[i].reshape(1, ML_WIDTH), x2, p[i].reshape(t, -1), tail_consts, batch, seq, alpha)
    return x2.reshape(batch, seq, d)
```

```python
import functools

import numpy as np
import jax
import jax.numpy as jnp
from jax import lax
from jax.experimental import pallas as pl
from jax.experimental.pallas import tpu as pltpu

F32 = jnp.float32
BF16 = jnp.bfloat16

CHUNK = 64
SUB = 16
N_SUB = CHUNK // SUB
EX = 8
N_EX = CHUNK // EX
EX_PER_SUB = SUB // EX
LOG2E = 1.4426950408889634
HG_HEADS = 4
HG_DK = 128
HG_DV = 128
HG_WIDTH = HG_HEADS * HG_DV
ML_HEADS = 4
ML_DQK = 64
ML_DV = 128
ML_WIDTH = ML_HEADS * ML_DV
ML_QK_WIDTH = ML_HEADS * ML_DQK
CONV_K = 4
LN_EPS = 1e-5
RMS_EPS = 1e-6

OFF_HQ = 0
OFF_HF = OFF_HQ + HG_HEADS * HG_DK
OFF_HV = OFF_HF + HG_HEADS * HG_DK
OFF_HG = OFF_HV + HG_WIDTH
OFF_MQ = OFF_HG + HG_WIDTH
OFF_MK = OFF_MQ + ML_QK_WIDTH
OFF_MV = OFF_MK + ML_QK_WIDTH
OFF_MO = OFF_MV + ML_WIDTH
OFF_GATES = OFF_MO + ML_WIDTH
LANE = 128
GATE_TILE = 8
GATE_ROWS = 2 * GATE_TILE

VMEM_LIMIT = 60 * 1024 * 1024
MIX_TILE = 256
TAIL_TILE = 512


def _sigmoid(x):
    return 0.5 * jnp.tanh(0.5 * x) + 0.5


def _silu(x):
    return x * _sigmoid(x)


def _log_sigmoid(x):
    return jnp.minimum(x, 0.0) - jnp.log(1.0 + jnp.exp(-jnp.abs(x)))


def _split3(x):
    hi = x.astype(BF16)
    r1 = x - hi.astype(F32)
    mid = r1.astype(BF16)
    lo = (r1 - mid.astype(F32)).astype(BF16)
    return hi, mid, lo


def _split3_f32(x):
    return [s.astype(F32) for s in _split3(x)]


def _dot(a, b):
    return jnp.dot(a, b, preferred_element_type=F32)


def _dot_nt(a, b):
    return lax.dot_general(a, b, (((1,), (1,)), ((), ())), preferred_element_type=F32)


def _dot_tn(a, b):
    return lax.dot_general(a, b, (((0,), (0,)), ((), ())), preferred_element_type=F32)


def _rows(blocks):
    return jnp.concatenate(blocks, axis=0)


def _cumsum_rows(tril_bf, x):
    hi, mid, lo = _split3(x)
    return _dot(tril_bf, hi) + _dot(tril_bf, mid) + _dot(tril_bf, lo)


def _head_rms(o, width):
    ms = jnp.sum(o * o, axis=-1, keepdims=True) * (1.0 / width)
    return o * lax.rsqrt(ms + RMS_EPS)


def _inproj_kernel(x_ref, w_ref, b_ref, wgt_ref, bgt_ref, lbl_ref,
                   logf_ref, hq_ref, hk_ref, hv_ref, hg_ref,
                   mq_ref, mk_ref, mv_ref, mo_ref, grow_ref):
    xb = x_ref[...].astype(BF16)

    def proj(lo, width):
        return _dot(xb, w_ref[:, lo:lo + width]) + b_ref[:, lo:lo + width]

    hq_ref[...] = _silu(proj(OFF_HQ, HG_WIDTH)).astype(BF16)
    hv_ref[...] = proj(OFF_HV, HG_WIDTH).astype(BF16)

    logits = lbl_ref[...]
    mx = jnp.max(logits, axis=0, keepdims=True)
    ex = jnp.exp(logits - mx)
    den = jnp.sum(ex, axis=0, keepdims=True)
    lb = ex[0:1, :] / den
    one_m_lb = (den - ex[0:1, :]) / den
    sig = _sigmoid(proj(OFF_HF, HG_WIDTH))
    logf_ref[...] = jnp.log(lb + one_m_lb * sig)
    hk_ref[...] = (one_m_lb * (1.0 - sig)).astype(BF16)

    mq_ref[...] = proj(OFF_MQ, ML_QK_WIDTH).astype(BF16)
    mk_ref[...] = proj(OFF_MK, ML_QK_WIDTH).astype(BF16)
    hg_ref[...] = _silu(proj(OFF_HG, HG_WIDTH)).astype(BF16)
    mv_ref[...] = proj(OFF_MV, ML_WIDTH).astype(BF16)
    mo_ref[...] = _sigmoid(proj(OFF_MO, ML_WIDTH)).astype(BF16)

    gt = _dot_nt(wgt_ref[...], xb) + bgt_ref[...]
    sub = lax.broadcasted_iota(jnp.int32, gt.shape, 0)
    is_fgate = (sub >= GATE_TILE) & (sub < GATE_TILE + ML_HEADS)
    grow_ref[...] = jnp.where(is_fgate, _log_sigmoid(gt), gt)


def _inproj(x2, w_bf, b_row, wgt, bgt, lb_logits, tm):
    t, d = x2.shape
    grid = (t // tm,)
    row = lambda i: (i, 0)
    const = lambda i: (0, 0)
    out_shapes = (
        jax.ShapeDtypeStruct((t, HG_WIDTH), F32),
        jax.ShapeDtypeStruct((t, HG_WIDTH), BF16),
        jax.ShapeDtypeStruct((t, HG_WIDTH), BF16),
        jax.ShapeDtypeStruct((t, HG_WIDTH), BF16),
        jax.ShapeDtypeStruct((t, HG_WIDTH), BF16),
        jax.ShapeDtypeStruct((t, ML_QK_WIDTH), BF16),
        jax.ShapeDtypeStruct((t, ML_QK_WIDTH), BF16),
        jax.ShapeDtypeStruct((t, ML_WIDTH), BF16),
        jax.ShapeDtypeStruct((t, ML_WIDTH), BF16),
        jax.ShapeDtypeStruct((GATE_ROWS, t), F32),
    )
    out_specs = (
        pl.BlockSpec((tm, HG_WIDTH), row), pl.BlockSpec((tm, HG_WIDTH), row),
        pl.BlockSpec((tm, HG_WIDTH), row), pl.BlockSpec((tm, HG_WIDTH), row),
        pl.BlockSpec((tm, HG_WIDTH), row), pl.BlockSpec((tm, ML_QK_WIDTH), row),
        pl.BlockSpec((tm, ML_QK_WIDTH), row), pl.BlockSpec((tm, ML_WIDTH), row),
        pl.BlockSpec((tm, ML_WIDTH), row),
        pl.BlockSpec((GATE_ROWS, tm), lambda i: (0, i)),
    )
    in_specs = [
        pl.BlockSpec((tm, d), row),
        pl.BlockSpec(w_bf.shape, const, pipeline_mode=pl.Buffered(1)),
        pl.BlockSpec(b_row.shape, const),
        pl.BlockSpec(wgt.shape, const),
        pl.BlockSpec(bgt.shape, const),
        pl.BlockSpec(lb_logits.shape, const),
    ]
    return pl.pallas_call(
        _inproj_kernel, grid=grid, in_specs=in_specs, out_specs=out_specs, out_shape=out_shapes,
        compiler_params=pltpu.CompilerParams(dimension_semantics=("arbitrary",),
                                             vmem_limit_bytes=VMEM_LIMIT),
        name="inproj",
    )(x2, w_bf, b_row, wgt, bgt, lb_logits)


def _hgrn2_select():
    sel = np.zeros((EX, HG_DK, CHUNK), np.float32)
    for j in range(EX):
        sel[j, :, j::EX] = 1.0
    return jnp.asarray(sel.reshape(EX * HG_DK, CHUNK), BF16)


def _chunk_tril(ts):
    idx = np.arange(ts)
    return jnp.asarray((idx[:, None] >= idx[None, :]) & (idx[:, None] // CHUNK == idx[None, :] // CHUNK), BF16)


def _hgrn2_phases(logf_ref, q_ref, k_ref, v_ref, og_ref, gn_ref, sel_ref, tril_ref, o_ref,
                  st_ref, b_scr, c_scr, n_chunks):
    row = lax.broadcasted_iota(jnp.int32, (CHUNK, CHUNK), 0)
    col = lax.broadcasted_iota(jnp.int32, (CHUNK, CHUNK), 1)
    exact_mask = (col <= row) & (row // EX == col // EX)
    pair_mask = row // SUB == col // SUB
    env = {}

    def blocks(vals):
        return _rows([jnp.broadcast_to(jnp.asarray(x, F32), (EX, HG_WIDTH)) for x in vals])

    def setup():
        b_scr[...] = _cumsum_rows(tril_ref[...], logf_ref[...]) * LOG2E
        c_scr[...] = b_scr[...] - jnp.log2(k_ref[...].astype(F32))
        env["states"] = [st_ref[h] for h in range(HG_HEADS)]

    def scores(c):
        r0 = c * CHUNK
        rows = pl.ds(r0, CHUNK)
        b2 = b_scr[rows, :]
        c2 = c_scr[rows, :]
        qf = q_ref[rows, :].astype(F32)

        bd = [jnp.zeros((1, HG_WIDTH), F32)]
        bd += [b_scr[pl.ds(r0 + EX * m - 1, 1), :] for m in range(1, N_EX + 1)]
        per_sub = EX_PER_SUB
        sub_start = blocks([bd[(m // per_sub) * per_sub] for m in range(N_EX)])
        q_st = qf * jnp.exp2(b2 - sub_start)
        q_in = (q_st * jnp.exp2(sub_start)).astype(BF16)
        k_dec = jnp.exp2(bd[N_EX] - c2).astype(BF16)
        chunk_decay = jnp.exp2(bd[N_EX])
        q_half = qf * jnp.exp2(b2 - blocks([bd[m] if m % per_sub else jnp.inf for m in range(N_EX)]))
        k_half = jnp.exp2(blocks([-jnp.inf if m % per_sub else bd[m + 1] for m in range(N_EX)]) - c2)

        prods = []
        for h in range(HG_HEADS):
            hs = slice(h * HG_DK, (h + 1) * HG_DK)
            per_key = []
            for j in range(EX):
                c_j = _rows([jnp.broadcast_to(c_scr[pl.ds(r0 + m * EX + j, 1), hs], (EX, HG_DK))
                            for m in range(N_EX)])
                per_key.append((qf[:, hs] * jnp.exp2(jnp.minimum(b2[:, hs] - c_j, 0.0))).astype(BF16))
            prods.append(jnp.concatenate(per_key, axis=1))
        exact_all = _dot(_rows(prods), sel_ref[...])

        a_heads = []
        for h in range(HG_HEADS):
            hs = slice(h * HG_DK, (h + 1) * HG_DK)
            exact = exact_all[h * CHUNK:(h + 1) * CHUNK, :]
            half = _dot_nt(q_half[:, hs].astype(BF16), k_half[:, hs].astype(BF16))
            off = [jnp.zeros((SUB, CHUNK), F32)]
            for i in range(1, N_SUB):
                k_i = jnp.exp2(bd[i * per_sub][:, hs] - c2[0:i * SUB, hs])
                k_i = _rows([k_i, jnp.zeros((CHUNK - i * SUB, HG_DK), F32)])
                off.append(_dot_nt(q_st[i * SUB:(i + 1) * SUB, hs].astype(BF16), k_i.astype(BF16)))
            a = jnp.where(exact_mask, exact, jnp.where(pair_mask, half, 0.0)) + _rows(off)
            a_heads.append(a.astype(BF16))
        env[c] = (a_heads, q_in, k_dec, chunk_decay)

    def outputs(c):
        a_heads, q_in, k_dec, chunk_decay = env.pop(c)
        states = env["states"]
        rows = pl.ds(c * CHUNK, CHUNK)
        v = v_ref[rows, :]
        outs = []
        for h in range(HG_HEADS):
            hs = slice(h * HG_DK, (h + 1) * HG_DK)
            st = states[h]
            o_h = _dot(a_heads[h], v[:, hs]) + _dot_nt(q_in[:, hs], st.astype(BF16))
            states[h] = st * chunk_decay[:, hs] + _dot_tn(v[:, hs], k_dec[:, hs])
            outs.append(_head_rms(o_h, HG_DV))
        o = jnp.concatenate(outs, axis=-1)
        o_ref[rows, :] = (o * gn_ref[...] * og_ref[rows, :].astype(F32)).astype(o_ref.dtype)

    def finish():
        for h in range(HG_HEADS):
            st_ref[h] = env["states"][h]

    phases = [setup, functools.partial(scores, 0)]
    for c in range(n_chunks):
        step = [functools.partial(scores, c + 1)] if c + 1 < n_chunks else []
        step.append(functools.partial(outputs, c))
        phases.append(lambda step=step: [f() for f in step])
    phases.append(finish)
    return phases


CONV_PAD = 8
SEL_ROWS = 128
GRP_M, GRP_WI, GRP_EN, GRP_ONE = 0, 32, 64, 96
ML_AUG = 2 * ML_DV


def _mlstm_constants(tl):
    ident = np.eye(tl, dtype=np.float32)
    triu = np.triu(np.ones((tl, tl), np.float32))
    bias = np.where(np.tril(np.ones((tl, tl), bool)), 0.0, -np.inf).astype(np.float32)
    sel_d = np.zeros((GRP_ONE, ML_HEADS * tl), np.float32)
    sel_w = np.zeros((SEL_ROWS, ML_HEADS * ML_AUG), np.float32)
    for h in range(ML_HEADS):
        for k in range(3):
            sel_d[GRP_M + GATE_TILE * k + h, h * tl:(h + 1) * tl] = -1.0
            sel_w[GRP_WI + GATE_TILE * k + h, h * ML_AUG:h * ML_AUG + ML_DV] = 1.0
            sel_w[GRP_EN + GATE_TILE * k + h, h * ML_AUG + ML_DV:(h + 1) * ML_AUG] = 1.0
    return (jnp.asarray(ident, BF16), jnp.asarray(triu, BF16), jnp.asarray(bias),
            jnp.asarray(sel_d, BF16), jnp.asarray(sel_w, BF16))


def _mlstm_phases(mq_ref, mk_ref, mv_ref, og_ref, grow_ref, cw_ref, cb_ref, gn_ref,
                  ident_ref, triu_ref, bias_ref, seld_ref, selw_ref, o_ref,
                  cbuf, c_ref, m_ref, tl):
    env = {}

    def scalars():
        cbuf[CONV_PAD:CONV_PAD + tl, 0:ML_QK_WIDTH] = mq_ref[...].astype(F32)
        cbuf[CONV_PAD:CONV_PAD + tl, ML_QK_WIDTH:] = mk_ref[...].astype(F32)
        acc = cb_ref[...] + cw_ref[0:1, :] * cbuf[pl.ds(CONV_PAD - (CONV_K - 1), tl), :]
        for tap in range(1, CONV_K):
            acc = acc + cw_ref[tap:tap + 1, :] * cbuf[pl.ds(CONV_PAD - (CONV_K - 1) + tap, tl), :]
        qk = _silu(acc)
        cbuf[0:CONV_PAD, :] = cbuf[tl:tl + CONV_PAD, :]
        env["q"] = (qk[:, 0:ML_QK_WIDTH] * (ML_DQK ** -0.5)).astype(BF16)
        k = qk[:, ML_QK_WIDTH:]
        env["k"] = k.astype(BF16)
        env["k_t"] = k.T

        gates = grow_ref[...]
        i_g = gates[0:GATE_TILE, :]
        hi, mid, lo = _split3(gates[GATE_TILE:, :])
        triu = triu_ref[...]
        part = _dot(_rows([hi, mid]), triu)
        g = part[0:GATE_TILE, :] + part[GATE_TILE:, :] + _dot(_rows([lo, lo]), triu)[0:GATE_TILE, :]
        u = i_g - g
        lane = lax.broadcasted_iota(jnp.int32, (GATE_TILE, tl), 1)
        cm = u
        shift = 1
        while shift < tl:
            cm = jnp.maximum(cm, jnp.where(lane >= shift, pltpu.roll(cm, shift, axis=1), -jnp.inf))
            shift *= 2
        m_prev = m_ref[...]
        m_run = jnp.maximum(m_prev, cm)
        w_inter = jnp.exp(m_prev - m_run)
        e_negm = jnp.exp(-(g + m_run))
        m_last = jnp.broadcast_to(m_run[:, tl - 1:tl], (GATE_TILE, tl))
        g_last = jnp.broadcast_to(g[:, tl - 1:tl], (GATE_TILE, tl))
        env["w_s"] = jnp.exp(u - m_last)
        env["w_old"] = jnp.exp(m_prev - m_last)
        m_ref[...] = g_last + m_last

        zeros8 = jnp.zeros((GATE_TILE, tl), F32)
        ones8 = jnp.ones((GATE_TILE, tl), F32)
        env["pack"] = _rows(_split3_f32(m_run) + [zeros8] + _split3_f32(w_inter) + [zeros8]
                            + _split3_f32(e_negm) + [zeros8] + [ones8, ones8, ones8, zeros8]).astype(BF16)
        u_rows = _rows(_split3_f32(u) + [zeros8])
        head_of_row = lax.broadcasted_iota(jnp.int32, u_rows.shape, 0) % GATE_TILE
        env["u_blocks"] = jnp.concatenate(
            [jnp.where(head_of_row == h, u_rows, 0.0) for h in range(ML_HEADS)], axis=1).astype(BF16)

    def copies():
        pack_t = _dot_nt(ident_ref[...], env["pack"]).astype(BF16)
        env["dmat"] = _dot(pack_t, _rows([seld_ref[...], env["u_blocks"]]))
        env["wrep"] = _dot(pack_t, selw_ref[...])

    def head(h):
        qs = slice(h * ML_DQK, (h + 1) * ML_DQK)
        vs = slice(h * ML_DV, (h + 1) * ML_DV)
        q_bf, k_bf, wrep = env["q"], env["k"], env["wrep"]
        w_intra = jnp.exp(env["dmat"][:, h * tl:(h + 1) * tl] + bias_ref[...])
        p = (_dot_nt(q_bf[:, qs], k_bf[:, qs]) * w_intra).astype(BF16)
        v_aug = jnp.concatenate([mv_ref[:, vs], jnp.ones((tl, ML_DV), BF16)], axis=1)
        c_aug = c_ref[h]
        intra = _dot(p, v_aug)
        inter = _dot(q_bf[:, qs], c_aug.astype(BF16))
        wi_rep = wrep[:, h * ML_AUG:h * ML_AUG + ML_DV]
        en_rep = wrep[:, h * ML_AUG + ML_DV:(h + 1) * ML_AUG]
        num = intra[:, 0:ML_DV] + wi_rep * inter[:, 0:ML_DV]
        den = intra[:, ML_DV:] + wi_rep * inter[:, ML_DV:]
        hh = num / jnp.maximum(jnp.abs(den), en_rep)
        kw_t = (env["k_t"][qs, :] * env["w_s"][h:h + 1, :]).astype(BF16)
        c_ref[h] = env["w_old"][h:h + 1, 0:ML_AUG] * c_aug + _dot(kw_t, v_aug)
        o_ref[:, vs] = (_head_rms(hh, ML_DV) * gn_ref[:, vs] * og_ref[:, vs].astype(F32)).astype(o_ref.dtype)

    return [scalars, copies] + [functools.partial(head, h) for h in range(ML_HEADS)]


FF_TILE = 256


def _layer_norm(x, g, b):
    mu = jnp.mean(x, axis=-1, keepdims=True)
    xc = x - mu
    var = jnp.mean(xc * xc, axis=-1, keepdims=True)
    return xc * lax.rsqrt(var + LN_EPS) * g + b


def _norm_phases(ohg_ref, oml_ref, x_ref, wo_ref, ln1g_ref, ln1b_ref, h1_ref, alpha):
    env = {}

    def project():
        env["mix"] = _dot(ohg_ref[...], wo_ref[0:HG_WIDTH, :]) + _dot(oml_ref[...], wo_ref[HG_WIDTH:, :])

    def norm():
        h1_ref[...] = _layer_norm(alpha * x_ref[...] + env["mix"], ln1g_ref[...], ln1b_ref[...])

    return project, norm


def _ffn_phases(h1_ref, p_ref, wg_ref, wu_ref, wd_ref, ln2g_ref, ln2b_ref, wpp_ref, wpg_ref, bpg_ref,
                out_ref, act_scr, ffn_scr, alpha, d_ff):
    env = {}
    d = h1_ref.shape[-1]

    def hidden(lo):
        if lo == 0:
            env["h1b"] = h1_ref[...].astype(BF16)
        h1b = env["h1b"]
        gate = _dot(h1b, wg_ref[:, lo:lo + FF_TILE])
        up = _dot(h1b, wu_ref[:, lo:lo + FF_TILE])
        act_scr[:, lo:lo + FF_TILE] = (_silu(gate) * up).astype(BF16)

    def down(lo):
        ffn_scr[:, lo:lo + FF_TILE] = _dot(act_scr[...], wd_ref[:, lo:lo + FF_TILE])

    def norm():
        env["h2"] = _layer_norm(alpha * h1_ref[...] + ffn_scr[...], ln2g_ref[...], ln2b_ref[...])

    def embed():
        h2 = env["h2"]
        pgate = _sigmoid(_dot(h2.astype(BF16), wpg_ref[...]) + bpg_ref[...])
        pemb = _dot(p_ref[...].astype(BF16), wpp_ref[...])
        out_ref[...] = h2 + pgate * pemb

    tiles = ([functools.partial(hidden, lo) for lo in range(0, d_ff, FF_TILE)]
             + [functools.partial(down, lo) for lo in range(0, d, FF_TILE)])
    return tiles, norm, embed


N_MIX_IN = 10
N_MIX_CONST = 11
N_TAIL_CONST = 11


def _mixer_groups(hg, ml):
    hg_setup, *hg_chunks, hg_finish = hg
    ml_scalars, ml_copies, *ml_heads = ml
    groups = [[hg_setup, ml_scalars], hg_chunks[:1], [ml_copies] + hg_chunks[1:2]]
    rest_hg, rest_ml = hg_chunks[2:], ml_heads
    for j in range(max(len(rest_hg), len(rest_ml))):
        groups.append(rest_ml[j:j + 1] + rest_hg[j:j + 1])
    groups[-1].append(hg_finish)
    return groups


def _issue_order(ffn, groups):
    order = []
    for j, f in enumerate(ffn):
        order.append(f)
        lo = j * len(groups) // len(ffn)
        hi = (j + 1) * len(groups) // len(ffn)
        for group in groups[lo:hi]:
            order.extend(group)
    return order


def _fused_kernel(*refs, tl, sub, tiles_per_seq, alpha, d_ff):
    it = iter(refs)
    take = lambda n: [next(it) for _ in range(n)]
    (logf_ref, hq_ref, hk_ref, hv_ref, hg_ref, mq_ref, mk_ref, mv_ref, mo_ref, grow_ref) = take(N_MIX_IN)
    (hgn_ref, sel_ref, tril_ref, cw_ref, cb_ref, mgn_ref,
     ident_ref, triu_ref, bias_ref, seld_ref, selw_ref) = take(N_MIX_CONST)
    x_ref, p_ref = take(2)
    (wo_ref, ln1g_ref, ln1b_ref, wg_ref, wu_ref, wd_ref,
     ln2g_ref, ln2b_ref, wpp_ref, wpg_ref, bpg_ref) = take(N_TAIL_CONST)
    (out_ref,) = take(1)
    st_ref, b_scr, c_scr, cbuf, c_ref, m_ref, ohg_scr, oml_scr, h1_scr, act_scr, ffn_scr = take(11)

    g = pl.program_id(0)

    @pl.when(g == 0)
    def _():
        h1_scr[...] = jnp.zeros_like(h1_scr)

    @pl.when(g % tiles_per_seq == 0)
    def _():
        st_ref[...] = jnp.zeros_like(st_ref)
        cbuf[0:CONV_PAD, :] = jnp.zeros((CONV_PAD, 2 * ML_QK_WIDTH), F32)
        c_ref[...] = jnp.zeros_like(c_ref)
        m_ref[...] = jnp.zeros_like(m_ref)

    cur = g % 2
    prev = 1 - cur
    groups = []
    for off in range(0, tl, sub):
        rows = pl.ds(off, sub)
        at = lambda ref: ref.at[rows]
        hg_phases = _hgrn2_phases(at(logf_ref), at(hq_ref), at(hk_ref), at(hv_ref), at(hg_ref), hgn_ref,
                                  sel_ref, tril_ref, ohg_scr.at[rows], st_ref, b_scr, c_scr, sub // CHUNK)
        ml_phases = _mlstm_phases(at(mq_ref), at(mk_ref), at(mv_ref), at(mo_ref), grow_ref.at[:, rows],
                                  cw_ref, cb_ref, mgn_ref, ident_ref, triu_ref, bias_ref, seld_ref, selw_ref,
                                  oml_scr.at[rows], cbuf, c_ref, m_ref, sub)
        groups += _mixer_groups(hg_phases, ml_phases)
    project, norm1 = _norm_phases(ohg_scr, oml_scr, x_ref, wo_ref, ln1g_ref, ln1b_ref, h1_scr.at[cur], alpha)
    ffn, norm2, embed = _ffn_phases(h1_scr.at[prev], p_ref, wg_ref, wu_ref, wd_ref, ln2g_ref, ln2b_ref,
                                    wpp_ref, wpg_ref, bpg_ref, out_ref, act_scr, ffn_scr, alpha, d_ff)
    for f in _issue_order(ffn, groups) + [project, norm2, embed, norm1]:
        f()


def _fused(mix_in, hgn, conv_w, conv_b, mgn, x2, p2, tail_consts, batch, seq, alpha):
    t, d = x2.shape
    tl, sub = TAIL_TILE, MIX_TILE
    assert seq % tl == 0 and tl % sub == 0 and sub % CHUNK == 0 and sub >= ML_AUG
    tiles_per_seq = seq // tl
    n_tiles = t // tl
    d_ff = tail_consts[3].shape[1]
    assert d_ff % FF_TILE == 0
    mix_consts = (hgn, _hgrn2_select(), _chunk_tril(sub), conv_w, conv_b, mgn) + _mlstm_constants(sub)
    assert len(mix_in) == N_MIX_IN and len(mix_consts) == N_MIX_CONST and len(tail_consts) == N_TAIL_CONST

    def tile(lag):
        return lambda g: jnp.clip(g - lag, 0, n_tiles - 1)

    def const(a):
        return pl.BlockSpec(a.shape, lambda g: (0,) * a.ndim, pipeline_mode=pl.Buffered(1))

    mix_specs = [pl.BlockSpec((tl, a.shape[1]), lambda g: (tile(0)(g), 0)) for a in mix_in[:-1]]
    mix_specs.append(pl.BlockSpec((GATE_ROWS, tl), lambda g: (0, tile(0)(g))))
    in_specs = (mix_specs + [const(a) for a in mix_consts]
                + [pl.BlockSpec((tl, d), lambda g: (tile(0)(g), 0)),
                   pl.BlockSpec((tl, p2.shape[1]), lambda g: (tile(1)(g), 0))]
                + [const(a) for a in tail_consts])
    return pl.pallas_call(
        functools.partial(_fused_kernel, tl=tl, sub=sub, tiles_per_seq=tiles_per_seq, alpha=alpha, d_ff=d_ff),
        grid=(n_tiles + 1,),
        in_specs=in_specs,
        out_specs=pl.BlockSpec((tl, d), lambda g: (tile(1)(g), 0)),
        out_shape=jax.ShapeDtypeStruct((t, d), F32),
        scratch_shapes=[pltpu.VMEM((HG_HEADS, HG_DV, HG_DK), F32),
                        pltpu.VMEM((sub, HG_WIDTH), F32),
                        pltpu.VMEM((sub, HG_WIDTH), F32),
                        pltpu.VMEM((sub + CONV_PAD, 2 * ML_QK_WIDTH), F32),
                        pltpu.VMEM((ML_HEADS, ML_DQK, ML_AUG), F32),
                        pltpu.VMEM((GATE_TILE, sub), F32),
                        pltpu.VMEM((tl, HG_WIDTH), BF16),
                        pltpu.VMEM((tl, ML_WIDTH), BF16),
                        pltpu.VMEM((2, tl, d), F32),
                        pltpu.VMEM((tl, d_ff), BF16),
                        pltpu.VMEM((tl, d), F32)],
        compiler_params=pltpu.CompilerParams(dimension_semantics=("arbitrary",),
                                             vmem_limit_bytes=VMEM_LIMIT),
        name="mix_tail",
    )(*mix_in, *mix_consts, x2, p2, *tail_consts)


def _pick(n, candidates):
    for c in candidates:
        if n % c == 0:
            return c
    raise ValueError(f"no tile for {n}")


def kernel(x, p, w_in, b_in, hg_lb_logits, ml_conv_w, ml_conv_b, hg_norm_g, ml_norm_g, w_out, ln1_g, ln1_b,
           w_ffn_gate, w_ffn_up, w_ffn_down, ln2_g, ln2_b, ple_w_proj, ple_w_gate, ple_b_gate):
    batch, seq, d = x.shape
    depth = w_in.shape[0]
    t = batch * seq
    alpha = float((2 * depth) ** 0.25)
    tm = _pick(t, (1024, 512, 256))
    assert w_in.shape[2] == OFF_GATES + 2 * ML_HEADS
    assert depth == 1, "lower-bound cumsum is specialised to a single layer"

    x2 = x.reshape(t, d)
    for i in range(depth):
        w_i = w_in[i]
        w_bf = w_i[:, :OFF_GATES].astype(BF16)
        b_row = b_in[i, :OFF_GATES].reshape(1, OFF_GATES)
        wg_t = w_i[:, OFF_GATES:].T
        gate_pad = ((0, GATE_TILE - ML_HEADS), (0, 0))
        wgt = jnp.concatenate([jnp.pad(wg_t[:ML_HEADS], gate_pad), jnp.pad(wg_t[ML_HEADS:], gate_pad)]).astype(BF16)
        bg = b_in[i, OFF_GATES:].reshape(2 * ML_HEADS, 1)
        bgt = jnp.concatenate([jnp.pad(bg[:ML_HEADS], gate_pad), jnp.pad(bg[ML_HEADS:], gate_pad)])

        mix_in = _inproj(x2, w_bf, b_row, wgt, bgt, hg_lb_logits, tm)
        tail_consts = (w_out[i].astype(BF16), ln1_g[i].reshape(1, d), ln1_b[i].reshape(1, d),
                       w_ffn_gate[i].astype(BF16), w_ffn_up[i].astype(BF16), w_ffn_down[i].astype(BF16),
                       ln2_g[i].reshape(1, d), ln2_b[i].reshape(1, d),
                       ple_w_proj[i].astype(BF16), ple_w_gate[i].astype(BF16), ple_b_gate[i].reshape(1, d))
        x2 = _fused(mix_in, hg_norm_g[i].reshape(1, HG_WIDTH), ml_conv_w[i], ml_conv_b[i].reshape(1, -1),
                    ml_norm_g[i].reshape(1, ML_WIDTH), x2, p[i].reshape(t, -1), tail_consts, batch, seq, alpha)
    return x2.reshape(batch, seq, d)
```

```python
import functools

import numpy as np
import jax
import jax.numpy as jnp
from jax import lax
from jax.experimental import pallas as pl
from jax.experimental.pallas import tpu as pltpu

F32 = jnp.float32
BF16 = jnp.bfloat16

CHUNK = 64
SUB = 16
N_SUB = CHUNK // SUB
EX = 8
N_EX = CHUNK // EX
EX_PER_SUB = SUB // EX
LOG2E = 1.4426950408889634
HG_HEADS = 4
HG_DK = 128
HG_DV = 128
HG_WIDTH = HG_HEADS * HG_DV
ML_HEADS = 4
ML_DQK = 64
ML_DV = 128
ML_WIDTH = ML_HEADS * ML_DV
ML_QK_WIDTH = ML_HEADS * ML_DQK
CONV_K = 4
LN_EPS = 1e-5
RMS_EPS = 1e-6

OFF_HQ = 0
OFF_HF = OFF_HQ + HG_HEADS * HG_DK
OFF_HV = OFF_HF + HG_HEADS * HG_DK
OFF_HG = OFF_HV + HG_WIDTH
OFF_MQ = OFF_HG + HG_WIDTH
OFF_MK = OFF_MQ + ML_QK_WIDTH
OFF_MV = OFF_MK + ML_QK_WIDTH
OFF_MO = OFF_MV + ML_WIDTH
OFF_GATES = OFF_MO + ML_WIDTH
LANE = 128
GATE_TILE = 8
GATE_ROWS = 2 * GATE_TILE

VMEM_LIMIT = 60 * 1024 * 1024
MIX_TILE = 256
TAIL_TILE = 512


def _sigmoid(x):
    return 0.5 * jnp.tanh(0.5 * x) + 0.5


def _silu(x):
    return x * _sigmoid(x)


def _log_sigmoid(x):
    return jnp.minimum(x, 0.0) - jnp.log(1.0 + jnp.exp(-jnp.abs(x)))


def _split3(x):
    hi = x.astype(BF16)
    r1 = x - hi.astype(F32)
    mid = r1.astype(BF16)
    lo = (r1 - mid.astype(F32)).astype(BF16)
    return hi, mid, lo


def _split3_f32(x):
    return [s.astype(F32) for s in _split3(x)]


def _dot(a, b):
    return jnp.dot(a, b, preferred_element_type=F32)


def _dot_nt(a, b):
    return lax.dot_general(a, b, (((1,), (1,)), ((), ())), preferred_element_type=F32)


def _dot_tn(a, b):
    return lax.dot_general(a, b, (((0,), (0,)), ((), ())), preferred_element_type=F32)


def _rows(blocks):
    return jnp.concatenate(blocks, axis=0)


def _cumsum_rows(tril_bf, x):
    hi, mid, lo = _split3(x)
    return _dot(tril_bf, hi) + _dot(tril_bf, mid) + _dot(tril_bf, lo)


def _head_rms(o, width):
    ms = jnp.sum(o * o, axis=-1, keepdims=True) * (1.0 / width)
    return o * lax.rsqrt(ms + RMS_EPS)


def _inproj_kernel(x_ref, w_ref, b_ref, wgt_ref, bgt_ref, lbl_ref,
                   logf_ref, hq_ref, hk_ref, hv_ref, hg_ref,
                   mq_ref, mk_ref, mv_ref, mo_ref, grow_ref):
    xb = x_ref[...].astype(BF16)

    def proj(lo, width):
        return _dot(xb, w_ref[:, lo:lo + width]) + b_ref[:, lo:lo + width]

    hq_ref[...] = _silu(proj(OFF_HQ, HG_WIDTH)).astype(BF16)
    hv_ref[...] = proj(OFF_HV, HG_WIDTH).astype(BF16)

    logits = lbl_ref[...]
    mx = jnp.max(logits, axis=0, keepdims=True)
    ex = jnp.exp(logits - mx)
    den = jnp.sum(ex, axis=0, keepdims=True)
    lb = ex[0:1, :] / den
    one_m_lb = (den - ex[0:1, :]) / den
    sig = _sigmoid(proj(OFF_HF, HG_WIDTH))
    logf_ref[...] = jnp.log(lb + one_m_lb * sig)
    hk_ref[...] = (one_m_lb * (1.0 - sig)).astype(BF16)

    mq_ref[...] = proj(OFF_MQ, ML_QK_WIDTH).astype(BF16)
    mk_ref[...] = proj(OFF_MK, ML_QK_WIDTH).astype(BF16)
    hg_ref[...] = _silu(proj(OFF_HG, HG_WIDTH)).astype(BF16)
    mv_ref[...] = proj(OFF_MV, ML_WIDTH).astype(BF16)
    mo_ref[...] = _sigmoid(proj(OFF_MO, ML_WIDTH)).astype(BF16)

    gt = _dot_nt(wgt_ref[...], xb) + bgt_ref[...]
    sub = lax.broadcasted_iota(jnp.int32, gt.shape, 0)
    is_fgate = (sub >= GATE_TILE) & (sub < GATE_TILE + ML_HEADS)
    grow_ref[...] = jnp.where(is_fgate, _log_sigmoid(gt), gt)


def _inproj(x2, w_bf, b_row, wgt, bgt, lb_logits, tm):
    t, d = x2.shape
    grid = (t // tm,)
    row = lambda i: (i, 0)
    const = lambda i: (0, 0)
    out_shapes = (
        jax.ShapeDtypeStruct((t, HG_WIDTH), F32),
        jax.ShapeDtypeStruct((t, HG_WIDTH), BF16),
        jax.ShapeDtypeStruct((t, HG_WIDTH), BF16),
        jax.ShapeDtypeStruct((t, HG_WIDTH), BF16),
        jax.ShapeDtypeStruct((t, HG_WIDTH), BF16),
        jax.ShapeDtypeStruct((t, ML_QK_WIDTH), BF16),
        jax.ShapeDtypeStruct((t, ML_QK_WIDTH), BF16),
        jax.ShapeDtypeStruct((t, ML_WIDTH), BF16),
        jax.ShapeDtypeStruct((t, ML_WIDTH), BF16),
        jax.ShapeDtypeStruct((GATE_ROWS, t), F32),
    )
    out_specs = (
        pl.BlockSpec((tm, HG_WIDTH), row), pl.BlockSpec((tm, HG_WIDTH), row),
        pl.BlockSpec((tm, HG_WIDTH), row), pl.BlockSpec((tm, HG_WIDTH), row),
        pl.BlockSpec((tm, HG_WIDTH), row), pl.BlockSpec((tm, ML_QK_WIDTH), row),
        pl.BlockSpec((tm, ML_QK_WIDTH), row), pl.BlockSpec((tm, ML_WIDTH), row),
        pl.BlockSpec((tm, ML_WIDTH), row),
        pl.BlockSpec((GATE_ROWS, tm), lambda i: (0, i)),
    )
    in_specs = [
        pl.BlockSpec((tm, d), row),
        pl.BlockSpec(w_bf.shape, const, pipeline_mode=pl.Buffered(1)),
        pl.BlockSpec(b_row.shape, const),
        pl.BlockSpec(wgt.shape, const),
        pl.BlockSpec(bgt.shape, const),
        pl.BlockSpec(lb_logits.shape, const),
    ]
    return pl.pallas_call(
        _inproj_kernel, grid=grid, in_specs=in_specs, out_specs=out_specs, out_shape=out_shapes,
        compiler_params=pltpu.CompilerParams(dimension_semantics=("arbitrary",),
                                             vmem_limit_bytes=VMEM_LIMIT),
        name="inproj",
    )(x2, w_bf, b_row, wgt, bgt, lb_logits)


def _chunk_tril(ts):
    idx = np.arange(ts)
    return jnp.asarray((idx[:, None] >= idx[None, :]) & (idx[:, None] // CHUNK == idx[None, :] // CHUNK), BF16)


def _hgrn2_phases(logf_ref, q_ref, k_ref, v_ref, og_ref, gn_ref, tril_ref, o_ref,
                  st_ref, b_scr, c_scr, n_chunks):
    row = lax.broadcasted_iota(jnp.int32, (CHUNK, CHUNK), 0)
    col = lax.broadcasted_iota(jnp.int32, (CHUNK, CHUNK), 1)
    exact_mask = (col <= row) & (row // EX == col // EX)
    pair_mask = row // SUB == col // SUB
    key_pos = col % EX
    env = {}

    def blocks(vals):
        return _rows([jnp.broadcast_to(jnp.asarray(x, F32), (EX, HG_WIDTH)) for x in vals])

    def setup():
        b_scr[...] = _cumsum_rows(tril_ref[...], logf_ref[...]) * LOG2E
        c_scr[...] = b_scr[...] - jnp.log2(k_ref[...].astype(F32))
        env["states"] = [st_ref[h] for h in range(HG_HEADS)]

    def scores(c):
        r0 = c * CHUNK
        rows = pl.ds(r0, CHUNK)
        b2 = b_scr[rows, :]
        c2 = c_scr[rows, :]
        qf = q_ref[rows, :].astype(F32)

        bd = [jnp.zeros((1, HG_WIDTH), F32)]
        bd += [b_scr[pl.ds(r0 + EX * m - 1, 1), :] for m in range(1, N_EX + 1)]
        per_sub = EX_PER_SUB
        sub_start = blocks([bd[(m // per_sub) * per_sub] for m in range(N_EX)])
        q_st = qf * jnp.exp2(b2 - sub_start)
        q_in = (q_st * jnp.exp2(sub_start)).astype(BF16)
        k_dec = jnp.exp2(bd[N_EX] - c2).astype(BF16)
        chunk_decay = jnp.exp2(bd[N_EX])
        q_half = qf * jnp.exp2(b2 - blocks([bd[m] if m % per_sub else jnp.inf for m in range(N_EX)]))
        k_half = jnp.exp2(blocks([-jnp.inf if m % per_sub else bd[m + 1] for m in range(N_EX)]) - c2)

        a_heads = []
        for h in range(HG_HEADS):
            hs = slice(h * HG_DK, (h + 1) * HG_DK)
            exact = jnp.zeros((CHUNK, CHUNK), F32)
            for j in range(EX):
                c_j = _rows([jnp.broadcast_to(c_scr[pl.ds(r0 + m * EX + j, 1), hs], (EX, HG_DK))
                            for m in range(N_EX)])
                prod = qf[:, hs] * jnp.exp2(b2[:, hs] - c_j)
                exact = jnp.where(key_pos == j, jnp.sum(prod, axis=-1, keepdims=True), exact)
            half = _dot_nt(q_half[:, hs].astype(BF16), k_half[:, hs].astype(BF16))
            off = [jnp.zeros((SUB, CHUNK), F32)]
            for i in range(1, N_SUB):
                k_i = jnp.exp2(bd[i * per_sub][:, hs] - c2[0:i * SUB, hs])
                k_i = _rows([k_i, jnp.zeros((CHUNK - i * SUB, HG_DK), F32)])
                off.append(_dot_nt(q_st[i * SUB:(i + 1) * SUB, hs].astype(BF16), k_i.astype(BF16)))
            a = jnp.where(exact_mask, exact, jnp.where(pair_mask, half, 0.0)) + _rows(off)
            a_heads.append(a.astype(BF16))
        env[c] = (a_heads, q_in, k_dec, chunk_decay)

    def outputs(c):
        a_heads, q_in, k_dec, chunk_decay = env.pop(c)
        states = env["states"]
        rows = pl.ds(c * CHUNK, CHUNK)
        v = v_ref[rows, :]
        outs = []
        for h in range(HG_HEADS):
            hs = slice(h * HG_DK, (h + 1) * HG_DK)
            st = states[h]
            o_h = _dot(a_heads[h], v[:, hs]) + _dot_nt(q_in[:, hs], st.astype(BF16))
            states[h] = st * chunk_decay[:, hs] + _dot_tn(v[:, hs], k_dec[:, hs])
            outs.append(_head_rms(o_h, HG_DV))
        o = jnp.concatenate(outs, axis=-1)
        o_ref[rows, :] = (o * gn_ref[...] * og_ref[rows, :].astype(F32)).astype(o_ref.dtype)

    def finish():
        for h in range(HG_HEADS):
            st_ref[h] = env["states"][h]

    phases = [setup, functools.partial(scores, 0)]
    for c in range(n_chunks):
        step = [functools.partial(scores, c + 1)] if c + 1 < n_chunks else []
        step.append(functools.partial(outputs, c))
        phases.append(lambda step=step: [f() for f in step])
    phases.append(finish)
    return phases


CONV_PAD = 8
SEL_ROWS = 128
GRP_M, GRP_WI, GRP_EN = 0, 32, 64
ML_AUG = 2 * ML_DV


def _mlstm_constants(tl):
    ident = np.eye(tl, dtype=np.float32)
    triu = np.triu(np.ones((tl, tl), np.float32))
    bias = np.where(np.tril(np.ones((tl, tl), bool)), 0.0, -np.inf).astype(np.float32)
    sel_sum = np.zeros((SEL_ROWS, LANE), np.float32)
    for q, grp in enumerate((GRP_M, GRP_WI, GRP_EN)):
        for h in range(ML_HEADS):
            for k in range(3):
                sel_sum[grp + GATE_TILE * k + h, q * GATE_TILE + h] = 1.0
    return (jnp.asarray(ident, BF16), jnp.asarray(triu, BF16), jnp.asarray(bias), jnp.asarray(sel_sum, BF16))


def _mlstm_phases(mq_ref, mk_ref, mv_ref, og_ref, grow_ref, cw_ref, cb_ref, gn_ref,
                  ident_ref, triu_ref, bias_ref, selsum_ref, o_ref,
                  cbuf, c_ref, m_ref, tl):
    env = {}

    def scalars():
        cbuf[CONV_PAD:CONV_PAD + tl, 0:ML_QK_WIDTH] = mq_ref[...].astype(F32)
        cbuf[CONV_PAD:CONV_PAD + tl, ML_QK_WIDTH:] = mk_ref[...].astype(F32)
        acc = cb_ref[...] + cw_ref[0:1, :] * cbuf[pl.ds(CONV_PAD - (CONV_K - 1), tl), :]
        for tap in range(1, CONV_K):
            acc = acc + cw_ref[tap:tap + 1, :] * cbuf[pl.ds(CONV_PAD - (CONV_K - 1) + tap, tl), :]
        qk = _silu(acc)
        cbuf[0:CONV_PAD, :] = cbuf[tl:tl + CONV_PAD, :]
        env["q"] = (qk[:, 0:ML_QK_WIDTH] * (ML_DQK ** -0.5)).astype(BF16)
        k = qk[:, ML_QK_WIDTH:]
        env["k"] = k.astype(BF16)
        env["k_t"] = k.T

        gates = grow_ref[...]
        i_g = gates[0:GATE_TILE, :]
        hi, mid, lo = _split3(gates[GATE_TILE:, :])
        triu = triu_ref[...]
        part = _dot(_rows([hi, mid]), triu)
        g = part[0:GATE_TILE, :] + part[GATE_TILE:, :] + _dot(_rows([lo, lo]), triu)[0:GATE_TILE, :]
        u = i_g - g
        lane = lax.broadcasted_iota(jnp.int32, (GATE_TILE, tl), 1)
        cm = u
        shift = 1
        while shift < tl:
            cm = jnp.maximum(cm, jnp.where(lane >= shift, pltpu.roll(cm, shift, axis=1), -jnp.inf))
            shift *= 2
        m_prev = m_ref[...]
        m_run = jnp.maximum(m_prev, cm)
        w_inter = jnp.exp(m_prev - m_run)
        e_negm = jnp.exp(-(g + m_run))
        m_last = jnp.broadcast_to(m_run[:, tl - 1:tl], (GATE_TILE, tl))
        g_last = jnp.broadcast_to(g[:, tl - 1:tl], (GATE_TILE, tl))
        env["w_s"] = jnp.exp(u - m_last)
        env["w_old"] = jnp.exp(m_prev - m_last)
        m_ref[...] = g_last + m_last

        env["u"] = u
        zeros8 = jnp.zeros((GATE_TILE, tl), F32)
        env["pack"] = _rows(_split3_f32(m_run) + [zeros8] + _split3_f32(w_inter) + [zeros8]
                            + _split3_f32(e_negm) + [zeros8] * 5).astype(BF16)

    def copies():
        pack_t = _dot_nt(ident_ref[...], env["pack"]).astype(BF16)
        env["cols"] = _dot(pack_t, selsum_ref[...])

    def head(h):
        qs = slice(h * ML_DQK, (h + 1) * ML_DQK)
        vs = slice(h * ML_DV, (h + 1) * ML_DV)
        q_bf, k_bf, cols = env["q"], env["k"], env["cols"]
        m_col = cols[:, h:h + 1]
        wi_rep = jnp.broadcast_to(cols[:, GATE_TILE + h:GATE_TILE + h + 1], (tl, ML_DV))
        en_rep = jnp.broadcast_to(cols[:, 2 * GATE_TILE + h:2 * GATE_TILE + h + 1], (tl, ML_DV))
        w_intra = jnp.exp((env["u"][h:h + 1, :] - m_col) + bias_ref[...])
        p = (_dot_nt(q_bf[:, qs], k_bf[:, qs]) * w_intra).astype(BF16)
        v_aug = jnp.concatenate([mv_ref[:, vs], jnp.ones((tl, ML_DV), BF16)], axis=1)
        c_aug = c_ref[h]
        intra = _dot(p, v_aug)
        inter = _dot(q_bf[:, qs], c_aug.astype(BF16))
        num = intra[:, 0:ML_DV] + wi_rep * inter[:, 0:ML_DV]
        den = intra[:, ML_DV:] + wi_rep * inter[:, ML_DV:]
        hh = num / jnp.maximum(jnp.abs(den), en_rep)
        kw_t = (env["k_t"][qs, :] * env["w_s"][h:h + 1, :]).astype(BF16)
        c_ref[h] = env["w_old"][h:h + 1, 0:ML_AUG] * c_aug + _dot(kw_t, v_aug)
        o_ref[:, vs] = (_head_rms(hh, ML_DV) * gn_ref[:, vs] * og_ref[:, vs].astype(F32)).astype(o_ref.dtype)

    return [scalars, copies] + [functools.partial(head, h) for h in range(ML_HEADS)]


FF_TILE = 256


def _layer_norm(x, g, b):
    mu = jnp.mean(x, axis=-1, keepdims=True)
    xc = x - mu
    var = jnp.mean(xc * xc, axis=-1, keepdims=True)
    return xc * lax.rsqrt(var + LN_EPS) * g + b


def _norm_phases(ohg_ref, oml_ref, x_ref, wo_ref, ln1g_ref, ln1b_ref, h1_ref, alpha):
    env = {}

    def project():
        env["mix"] = _dot(ohg_ref[...], wo_ref[0:HG_WIDTH, :]) + _dot(oml_ref[...], wo_ref[HG_WIDTH:, :])

    def norm():
        h1_ref[...] = _layer_norm(alpha * x_ref[...] + env["mix"], ln1g_ref[...], ln1b_ref[...])

    return project, norm


def _ffn_phases(h1_ref, p_ref, wg_ref, wu_ref, wd_ref, ln2g_ref, ln2b_ref, wpp_ref, wpg_ref, bpg_ref,
                out_ref, act_scr, ffn_scr, alpha, d_ff):
    env = {}
    d = h1_ref.shape[-1]

    def hidden(lo):
        if lo == 0:
            env["h1b"] = h1_ref[...].astype(BF16)
        h1b = env["h1b"]
        gate = _dot(h1b, wg_ref[:, lo:lo + FF_TILE])
        up = _dot(h1b, wu_ref[:, lo:lo + FF_TILE])
        act_scr[:, lo:lo + FF_TILE] = (_silu(gate) * up).astype(BF16)

    def down(lo):
        ffn_scr[:, lo:lo + FF_TILE] = _dot(act_scr[...], wd_ref[:, lo:lo + FF_TILE])

    def norm():
        env["pemb"] = _dot(p_ref[...].astype(BF16), wpp_ref[...])
        env["h2"] = _layer_norm(alpha * h1_ref[...] + ffn_scr[...], ln2g_ref[...], ln2b_ref[...])

    def embed():
        h2 = env["h2"]
        pgate = _sigmoid(_dot(h2.astype(BF16), wpg_ref[...]) + bpg_ref[...])
        out_ref[...] = h2 + pgate * env["pemb"]

    tiles = ([functools.partial(hidden, lo) for lo in range(0, d_ff, FF_TILE)]
             + [functools.partial(down, lo) for lo in range(0, d, FF_TILE)])
    return tiles, norm, embed


N_MIX_IN = 10
N_MIX_CONST = 9
N_TAIL_CONST = 11


def _mixer_groups(hg, ml):
    hg_setup, *hg_chunks, hg_finish = hg
    ml_scalars, ml_copies, *ml_heads = ml
    groups = [[hg_setup, ml_scalars], hg_chunks[:1], [ml_copies] + hg_chunks[1:2]]
    rest_hg, rest_ml = hg_chunks[2:], ml_heads
    for j in range(max(len(rest_hg), len(rest_ml))):
        groups.append(rest_ml[j:j + 1] + rest_hg[j:j + 1])
    groups[-1].append(hg_finish)
    return groups


def _issue_order(ffn, groups):
    order = []
    for j, f in enumerate(ffn):
        order.append(f)
        lo = j * len(groups) // len(ffn)
        hi = (j + 1) * len(groups) // len(ffn)
        for group in groups[lo:hi]:
            order.extend(group)
    return order


def _fused_kernel(*refs, tl, sub, tiles_per_seq, alpha, d_ff):
    it = iter(refs)
    take = lambda n: [next(it) for _ in range(n)]
    (logf_ref, hq_ref, hk_ref, hv_ref, hg_ref, mq_ref, mk_ref, mv_ref, mo_ref, grow_ref) = take(N_MIX_IN)
    (hgn_ref, tril_ref, cw_ref, cb_ref, mgn_ref,
     ident_ref, triu_ref, bias_ref, selsum_ref) = take(N_MIX_CONST)
    x_ref, p_ref = take(2)
    (wo_ref, ln1g_ref, ln1b_ref, wg_ref, wu_ref, wd_ref,
     ln2g_ref, ln2b_ref, wpp_ref, wpg_ref, bpg_ref) = take(N_TAIL_CONST)
    (out_ref,) = take(1)
    st_ref, b_scr, c_scr, cbuf, c_ref, m_ref, ohg_scr, oml_scr, h1_scr, act_scr, ffn_scr = take(11)

    g = pl.program_id(0)

    @pl.when(g == 0)
    def _():
        h1_scr[...] = jnp.zeros_like(h1_scr)

    @pl.when(g % tiles_per_seq == 0)
    def _():
        st_ref[...] = jnp.zeros_like(st_ref)
        cbuf[0:CONV_PAD, :] = jnp.zeros((CONV_PAD, 2 * ML_QK_WIDTH), F32)
        c_ref[...] = jnp.zeros_like(c_ref)
        m_ref[...] = jnp.zeros_like(m_ref)

    cur = g % 2
    prev = 1 - cur
    groups = []
    for off in range(0, tl, sub):
        rows = pl.ds(off, sub)
        at = lambda ref: ref.at[rows]
        hg_phases = _hgrn2_phases(at(logf_ref), at(hq_ref), at(hk_ref), at(hv_ref), at(hg_ref), hgn_ref,
                                  tril_ref, ohg_scr.at[rows], st_ref, b_scr, c_scr, sub // CHUNK)
        ml_phases = _mlstm_phases(at(mq_ref), at(mk_ref), at(mv_ref), at(mo_ref), grow_ref.at[:, rows],
                                  cw_ref, cb_ref, mgn_ref, ident_ref, triu_ref, bias_ref, selsum_ref,
                                  oml_scr.at[rows], cbuf, c_ref, m_ref, sub)
        groups += _mixer_groups(hg_phases, ml_phases)
    project, norm1 = _norm_phases(ohg_scr, oml_scr, x_ref, wo_ref, ln1g_ref, ln1b_ref, h1_scr.at[cur], alpha)
    ffn, norm2, embed = _ffn_phases(h1_scr.at[prev], p_ref, wg_ref, wu_ref, wd_ref, ln2g_ref, ln2b_ref,
                                    wpp_ref, wpg_ref, bpg_ref, out_ref, act_scr, ffn_scr, alpha, d_ff)
    for f in _issue_order(ffn, groups) + [project, norm2, embed, norm1]:
        f()


def _fused(mix_in, hgn, conv_w, conv_b, mgn, x2, p2, tail_consts, batch, seq, alpha):
    t, d = x2.shape
    tl, sub = TAIL_TILE, MIX_TILE
    assert seq % tl == 0 and tl % sub == 0 and sub % CHUNK == 0 and sub >= ML_AUG
    tiles_per_seq = seq // tl
    n_tiles = t // tl
    d_ff = tail_consts[3].shape[1]
    assert d_ff % FF_TILE == 0
    mix_consts = (hgn, _chunk_tril(sub), conv_w, conv_b, mgn) + _mlstm_constants(sub)
    assert len(mix_in) == N_MIX_IN and len(mix_consts) == N_MIX_CONST and len(tail_consts) == N_TAIL_CONST

    def tile(lag):
        return lambda g: jnp.clip(g - lag, 0, n_tiles - 1)

    def const(a):
        return pl.BlockSpec(a.shape, lambda g: (0,) * a.ndim, pipeline_mode=pl.Buffered(1))

    mix_specs = [pl.BlockSpec((tl, a.shape[1]), lambda g: (tile(0)(g), 0)) for a in mix_in[:-1]]
    mix_specs.append(pl.BlockSpec((GATE_ROWS, tl), lambda g: (0, tile(0)(g))))
    in_specs = (mix_specs + [const(a) for a in mix_consts]
                + [pl.BlockSpec((tl, d), lambda g: (tile(0)(g), 0)),
                   pl.BlockSpec((tl, p2.shape[1]), lambda g: (tile(1)(g), 0))]
                + [const(a) for a in tail_consts])
    return pl.pallas_call(
        functools.partial(_fused_kernel, tl=tl, sub=sub, tiles_per_seq=tiles_per_seq, alpha=alpha, d_ff=d_ff),
        grid=(n_tiles + 1,),
        in_specs=in_specs,
        out_specs=pl.BlockSpec((tl, d), lambda g: (tile(1)(g), 0)),
        out_shape=jax.ShapeDtypeStruct((t, d), F32),
        scratch_shapes=[pltpu.VMEM((HG_HEADS, HG_DV, HG_DK), F32),
                        pltpu.VMEM((sub, HG_WIDTH), F32),
                        pltpu.VMEM((sub, HG_WIDTH), F32),
                        pltpu.VMEM((sub + CONV_PAD, 2 * ML_QK_WIDTH), F32),
                        pltpu.VMEM((ML_HEADS, ML_DQK, ML_AUG), F32),
                        pltpu.VMEM((GATE_TILE, sub), F32),
                        pltpu.VMEM((tl, HG_WIDTH), BF16),
                        pltpu.VMEM((tl, ML_WIDTH), BF16),
                        pltpu.VMEM((2, tl, d), F32),
                        pltpu.VMEM((tl, d_ff), BF16),
                        pltpu.VMEM((tl, d), F32)],
        compiler_params=pltpu.CompilerParams(dimension_semantics=("arbitrary",),
                                             vmem_limit_bytes=VMEM_LIMIT),
        name="mix_tail",
    )(*mix_in, *mix_consts, x2, p2, *tail_consts)


def _pick(n, candidates):
    for c in candidates:
        if n % c == 0:
            return c
    raise ValueError(f"no tile for {n}")


def kernel(x, p, w_in, b_in, hg_lb_logits, ml_conv_w, ml_conv_b, hg_norm_g, ml_norm_g, w_out, ln1_g, ln1_b,
           w_ffn_gate, w_ffn_up, w_ffn_down, ln2_g, ln2_b, ple_w_proj, ple_w_gate, ple_b_gate):
    batch, seq, d = x.shape
    depth = w_in.shape[0]
    t = batch * seq
    alpha = float((2 * depth) ** 0.25)
    tm = _pick(t, (1024, 512, 256))
    assert w_in.shape[2] == OFF_GATES + 2 * ML_HEADS
    assert depth == 1, "lower-bound cumsum is specialised to a single layer"

    x2 = x.reshape(t, d)
    for i in range(depth):
        w_i = w_in[i]
        w_bf = w_i[:, :OFF_GATES].astype(BF16)
        b_row = b_in[i, :OFF_GATES].reshape(1, OFF_GATES)
        wg_t = w_i[:, OFF_GATES:].T
        gate_pad = ((0, GATE_TILE - ML_HEADS), (0, 0))
        wgt = jnp.concatenate([jnp.pad(wg_t[:ML_HEADS], gate_pad), jnp.pad(wg_t[ML_HEADS:], gate_pad)]).astype(BF16)
        bg = b_in[i, OFF_GATES:].reshape(2 * ML_HEADS, 1)
        bgt = jnp.concatenate([jnp.pad(bg[:ML_HEADS], gate_pad), jnp.pad(bg[ML_HEADS:], gate_pad)])

        mix_in = _inproj(x2, w_bf, b_row, wgt, bgt, hg_lb_logits, tm)
        tail_consts = (w_out[i].astype(BF16), ln1_g[i].reshape(1, d), ln1_b[i].reshape(1, d),
                       w_ffn_gate[i].astype(BF16), w_ffn_up[i].astype(BF16), w_ffn_down[i].astype(BF16),
                       ln2_g[i].reshape(1, d), ln2_b[i].reshape(1, d),
                       ple_w_proj[i].astype(BF16), ple_w_gate[i].astype(BF16), ple_b_gate[i].reshape(1, d))
        x2 = _fused(mix_in, hg_norm_g[i].reshape(1, HG_WIDTH), ml_conv_w[i], ml_conv_b[i].reshape(1, -1),
                    ml_norm_g[i].reshape(1, ML_WIDTH), x2, p[i].reshape(t, -1), tail_consts, batch, seq, alpha)
    return x2.reshape(batch, seq, d)
```

```python
import functools

import numpy as np
import jax
import jax.numpy as jnp
from jax import lax
from jax.experimental import pallas as pl
from jax.experimental.pallas import tpu as pltpu

F32 = jnp.float32
BF16 = jnp.bfloat16

CHUNK = 64
SUB = 16
N_SUB = CHUNK // SUB
EX = 8
N_EX = CHUNK // EX
EX_PER_SUB = SUB // EX
LOG2E = 1.4426950408889634
HG_HEADS = 4
HG_DK = 128
HG_DV = 128
HG_WIDTH = HG_HEADS * HG_DV
ML_HEADS = 4
ML_DQK = 64
ML_DV = 128
ML_WIDTH = ML_HEADS * ML_DV
ML_QK_WIDTH = ML_HEADS * ML_DQK
CONV_K = 4
LN_EPS = 1e-5
RMS_EPS = 1e-6

OFF_HQ = 0
OFF_HF = OFF_HQ + HG_HEADS * HG_DK
OFF_HV = OFF_HF + HG_HEADS * HG_DK
OFF_HG = OFF_HV + HG_WIDTH
OFF_MQ = OFF_HG + HG_WIDTH
OFF_MK = OFF_MQ + ML_QK_WIDTH
OFF_MV = OFF_MK + ML_QK_WIDTH
OFF_MO = OFF_MV + ML_WIDTH
OFF_GATES = OFF_MO + ML_WIDTH
LANE = 128
GATE_TILE = 8
GATE_ROWS = 2 * GATE_TILE
PROJ_PIECE = 256

VMEM_LIMIT = 60 * 1024 * 1024
MIX_TILE = 256
TAIL_TILE = 512


def _sigmoid(x):
    return 0.5 * jnp.tanh(0.5 * x) + 0.5


def _silu(x):
    return x * _sigmoid(x)


def _log_sigmoid(x):
    return jnp.minimum(x, 0.0) - jnp.log(1.0 + jnp.exp(-jnp.abs(x)))


def _split3(x):
    hi = x.astype(BF16)
    r1 = x - hi.astype(F32)
    mid = r1.astype(BF16)
    lo = (r1 - mid.astype(F32)).astype(BF16)
    return hi, mid, lo


def _split3_f32(x):
    return [s.astype(F32) for s in _split3(x)]


def _dot(a, b):
    return jnp.dot(a, b, preferred_element_type=F32)


def _dot_nt(a, b):
    return lax.dot_general(a, b, (((1,), (1,)), ((), ())), preferred_element_type=F32)


def _dot_tn(a, b):
    return lax.dot_general(a, b, (((0,), (0,)), ((), ())), preferred_element_type=F32)


def _rows(blocks):
    return jnp.concatenate(blocks, axis=0)


def _cumsum_rows(tril_bf, x):
    hi, mid, lo = _split3(x)
    return _dot(tril_bf, hi) + _dot(tril_bf, mid) + _dot(tril_bf, lo)


def _head_rms(o, width):
    ms = jnp.sum(o * o, axis=-1, keepdims=True) * (1.0 / width)
    return o * lax.rsqrt(ms + RMS_EPS)


def _inproj_kernel(x_ref, w_ref, b_ref, wgt_ref, bgt_ref, lbl_ref,
                   logf_ref, hq_ref, hk_ref, hv_ref, hg_ref,
                   mq_ref, mk_ref, mv_ref, mo_ref, grow_ref):
    xb = x_ref[...].astype(BF16)

    logits = lbl_ref[...]
    mx = jnp.max(logits, axis=0, keepdims=True)
    ex = jnp.exp(logits - mx)
    den = jnp.sum(ex, axis=0, keepdims=True)
    lb = ex[0:1, :] / den
    one_m_lb = (den - ex[0:1, :]) / den

    def plain(out_ref):
        def store(u, cols):
            out_ref[:, cols] = u.astype(BF16)
        return store

    def act(out_ref, fn):
        def store(u, cols):
            out_ref[:, cols] = fn(u).astype(BF16)
        return store

    def forget(u, cols):
        sig = _sigmoid(u)
        logf_ref[:, cols] = jnp.log(lb[:, cols] + one_m_lb[:, cols] * sig)
        hk_ref[:, cols] = (one_m_lb[:, cols] * (1.0 - sig)).astype(BF16)

    n = PROJ_PIECE
    pieces = {"hq": (OFF_HQ, HG_WIDTH, act(hq_ref, _silu)), "hf": (OFF_HF, HG_WIDTH, forget),
              "hv": (OFF_HV, HG_WIDTH, plain(hv_ref)), "hg": (OFF_HG, HG_WIDTH, act(hg_ref, _silu)),
              "mq": (OFF_MQ, ML_QK_WIDTH, plain(mq_ref)), "mk": (OFF_MK, ML_QK_WIDTH, plain(mk_ref)),
              "mv": (OFF_MV, ML_WIDTH, plain(mv_ref)), "mo": (OFF_MO, ML_WIDTH, act(mo_ref, _sigmoid))}
    order = ["hq", "hv", "hf", "mq", "hg", "hv", "mo", "mk", "hq", "mv", "hf", "mv", "hg", "mo"]
    taken = {name: 0 for name in pieces}
    for name in order:
        off, width, store = pieces[name]
        lo = taken[name]
        taken[name] = lo + n
        store(_dot(xb, w_ref[:, off + lo:off + lo + n]) + b_ref[:, off + lo:off + lo + n], slice(lo, lo + n))
    assert all(taken[name] == pieces[name][1] for name in pieces)

    gt = _dot_nt(wgt_ref[...], xb) + bgt_ref[...]
    sub = lax.broadcasted_iota(jnp.int32, gt.shape, 0)
    is_fgate = (sub >= GATE_TILE) & (sub < GATE_TILE + ML_HEADS)
    grow_ref[...] = jnp.where(is_fgate, _log_sigmoid(gt), gt)


def _inproj(x2, w_bf, b_row, wgt, bgt, lb_logits, tm):
    t, d = x2.shape
    grid = (t // tm,)
    row = lambda i: (i, 0)
    const = lambda i: (0, 0)
    out_shapes = (
        jax.ShapeDtypeStruct((t, HG_WIDTH), F32),
        jax.ShapeDtypeStruct((t, HG_WIDTH), BF16),
        jax.ShapeDtypeStruct((t, HG_WIDTH), BF16),
        jax.ShapeDtypeStruct((t, HG_WIDTH), BF16),
        jax.ShapeDtypeStruct((t, HG_WIDTH), BF16),
        jax.ShapeDtypeStruct((t, ML_QK_WIDTH), BF16),
        jax.ShapeDtypeStruct((t, ML_QK_WIDTH), BF16),
        jax.ShapeDtypeStruct((t, ML_WIDTH), BF16),
        jax.ShapeDtypeStruct((t, ML_WIDTH), BF16),
        jax.ShapeDtypeStruct((GATE_ROWS, t), F32),
    )
    out_specs = (
        pl.BlockSpec((tm, HG_WIDTH), row), pl.BlockSpec((tm, HG_WIDTH), row),
        pl.BlockSpec((tm, HG_WIDTH), row), pl.BlockSpec((tm, HG_WIDTH), row),
        pl.BlockSpec((tm, HG_WIDTH), row), pl.BlockSpec((tm, ML_QK_WIDTH), row),
        pl.BlockSpec((tm, ML_QK_WIDTH), row), pl.BlockSpec((tm, ML_WIDTH), row),
        pl.BlockSpec((tm, ML_WIDTH), row),
        pl.BlockSpec((GATE_ROWS, tm), lambda i: (0, i)),
    )
    in_specs = [
        pl.BlockSpec((tm, d), row),
        pl.BlockSpec((d, OFF_GATES), const, pipeline_mode=pl.Buffered(1)),
        pl.BlockSpec((1, OFF_GATES), const),
        pl.BlockSpec(wgt.shape, const),
        pl.BlockSpec(bgt.shape, const),
        pl.BlockSpec(lb_logits.shape, const),
    ]
    return pl.pallas_call(
        _inproj_kernel, grid=grid, in_specs=in_specs, out_specs=out_specs, out_shape=out_shapes,
        compiler_params=pltpu.CompilerParams(dimension_semantics=("arbitrary",),
                                             vmem_limit_bytes=VMEM_LIMIT),
        name="inproj",
    )(x2, w_bf, b_row, wgt, bgt, lb_logits)


def _chunk_tril(ts):
    idx = np.arange(ts)
    return jnp.asarray((idx[:, None] >= idx[None, :]) & (idx[:, None] // CHUNK == idx[None, :] // CHUNK), BF16)


def _hgrn2_phases(logf_ref, q_ref, k_ref, v_ref, og_ref, gn_ref, tril_ref, o_ref,
                  st_ref, b_scr, c_scr, n_chunks):
    row = lax.broadcasted_iota(jnp.int32, (CHUNK, CHUNK), 0)
    col = lax.broadcasted_iota(jnp.int32, (CHUNK, CHUNK), 1)
    exact_mask = (col <= row) & (row // EX == col // EX)
    pair_mask = row // SUB == col // SUB
    key_pos = col % EX
    env = {}

    def blocks(vals):
        return _rows([jnp.broadcast_to(jnp.asarray(x, F32), (EX, HG_WIDTH)) for x in vals])

    def setup():
        b_scr[...] = _cumsum_rows(tril_ref[...], logf_ref[...]) * LOG2E
        c_scr[...] = b_scr[...] - jnp.log2(k_ref[...].astype(F32))
        env["states"] = [st_ref[h] for h in range(HG_HEADS)]

    def scores(c):
        r0 = c * CHUNK
        rows = pl.ds(r0, CHUNK)
        b2 = b_scr[rows, :]
        c2 = c_scr[rows, :]
        qf = q_ref[rows, :].astype(F32)

        bd = [jnp.zeros((1, HG_WIDTH), F32)]
        bd += [b_scr[pl.ds(r0 + EX * m - 1, 1), :] for m in range(1, N_EX + 1)]
        per_sub = EX_PER_SUB
        sub_start = blocks([bd[(m // per_sub) * per_sub] for m in range(N_EX)])
        q_st = qf * jnp.exp2(b2 - sub_start)
        q_in = (q_st * jnp.exp2(sub_start)).astype(BF16)
        k_dec = jnp.exp2(bd[N_EX] - c2).astype(BF16)
        chunk_decay = jnp.exp2(bd[N_EX])
        q_half = qf * jnp.exp2(b2 - blocks([bd[m] if m % per_sub else jnp.inf for m in range(N_EX)]))
        k_half = jnp.exp2(blocks([-jnp.inf if m % per_sub else bd[m + 1] for m in range(N_EX)]) - c2)

        a_heads = []
        for h in range(HG_HEADS):
            hs = slice(h * HG_DK, (h + 1) * HG_DK)
            exact = jnp.zeros((CHUNK, CHUNK), F32)
            for j in range(EX):
                c_j = _rows([jnp.broadcast_to(c_scr[pl.ds(r0 + m * EX + j, 1), hs], (EX, HG_DK))
                            for m in range(N_EX)])
                prod = qf[:, hs] * jnp.exp2(b2[:, hs] - c_j)
                exact = jnp.where(key_pos == j, jnp.sum(prod, axis=-1, keepdims=True), exact)
            half = _dot_nt(q_half[:, hs].astype(BF16), k_half[:, hs].astype(BF16))
            off = [jnp.zeros((SUB, CHUNK), F32)]
            for i in range(1, N_SUB):
                k_i = jnp.exp2(bd[i * per_sub][:, hs] - c2[0:i * SUB, hs])
                k_i = _rows([k_i, jnp.zeros((CHUNK - i * SUB, HG_DK), F32)])
                off.append(_dot_nt(q_st[i * SUB:(i + 1) * SUB, hs].astype(BF16), k_i.astype(BF16)))
            a = jnp.where(exact_mask, exact, jnp.where(pair_mask, half, 0.0)) + _rows(off)
            a_heads.append(a.astype(BF16))
        env[c] = (a_heads, q_in, k_dec, chunk_decay)

    def outputs(c):
        a_heads, q_in, k_dec, chunk_decay = env.pop(c)
        states = env["states"]
        rows = pl.ds(c * CHUNK, CHUNK)
        v = v_ref[rows, :]
        outs = []
        for h in range(HG_HEADS):
            hs = slice(h * HG_DK, (h + 1) * HG_DK)
            st = states[h]
            o_h = _dot(a_heads[h], v[:, hs]) + _dot_nt(q_in[:, hs], st.astype(BF16))
            states[h] = st * chunk_decay[:, hs] + _dot_tn(v[:, hs], k_dec[:, hs])
            outs.append(_head_rms(o_h, HG_DV))
        o = jnp.concatenate(outs, axis=-1)
        o_ref[rows, :] = (o * gn_ref[...] * og_ref[rows, :].astype(F32)).astype(o_ref.dtype)

    def finish():
        for h in range(HG_HEADS):
            st_ref[h] = env["states"][h]

    phases = [setup, functools.partial(scores, 0)]
    for c in range(n_chunks):
        step = [functools.partial(scores, c + 1)] if c + 1 < n_chunks else []
        step.append(functools.partial(outputs, c))
        phases.append(lambda step=step: [f() for f in step])
    phases.append(finish)
    return phases


CONV_PAD = 8
SEL_ROWS = 128
GRP_M, GRP_WI, GRP_EN = 0, 32, 64
ML_AUG = 2 * ML_DV


def _mlstm_constants(tl):
    ident = np.eye(tl, dtype=np.float32)
    triu = np.triu(np.ones((tl, tl), np.float32))
    bias = np.where(np.tril(np.ones((tl, tl), bool)), 0.0, -np.inf).astype(np.float32)
    sel_sum = np.zeros((SEL_ROWS, LANE), np.float32)
    for q, grp in enumerate((GRP_M, GRP_WI, GRP_EN)):
        for h in range(ML_HEADS):
            for k in range(3):
                sel_sum[grp + GATE_TILE * k + h, q * GATE_TILE + h] = 1.0
    return (jnp.asarray(ident, BF16), jnp.asarray(triu, BF16), jnp.asarray(bias), jnp.asarray(sel_sum, BF16))


def _mlstm_phases(mq_ref, mk_ref, mv_ref, og_ref, grow_ref, cw_ref, cb_ref, gn_ref,
                  ident_ref, triu_ref, bias_ref, selsum_ref, o_ref,
                  cbuf, c_ref, m_ref, tl):
    env = {}

    def scalars():
        cbuf[CONV_PAD:CONV_PAD + tl, 0:ML_QK_WIDTH] = mq_ref[...].astype(F32)
        cbuf[CONV_PAD:CONV_PAD + tl, ML_QK_WIDTH:] = mk_ref[...].astype(F32)
        acc = cb_ref[...] + cw_ref[0:1, :] * cbuf[pl.ds(CONV_PAD - (CONV_K - 1), tl), :]
        for tap in range(1, CONV_K):
            acc = acc + cw_ref[tap:tap + 1, :] * cbuf[pl.ds(CONV_PAD - (CONV_K - 1) + tap, tl), :]
        qk = _silu(acc)
        cbuf[0:CONV_PAD, :] = cbuf[tl:tl + CONV_PAD, :]
        env["q"] = (qk[:, 0:ML_QK_WIDTH] * (ML_DQK ** -0.5)).astype(BF16)
        k = qk[:, ML_QK_WIDTH:]
        env["k"] = k.astype(BF16)
        env["k_t"] = k.T

        gates = grow_ref[...]
        i_g = gates[0:GATE_TILE, :]
        hi, mid, lo = _split3(gates[GATE_TILE:, :])
        triu = triu_ref[...]
        part = _dot(_rows([hi, mid]), triu)
        g = part[0:GATE_TILE, :] + part[GATE_TILE:, :] + _dot(_rows([lo, lo]), triu)[0:GATE_TILE, :]
        u = i_g - g
        lane = lax.broadcasted_iota(jnp.int32, (GATE_TILE, tl), 1)
        cm = u
        shift = 1
        while shift < tl:
            cm = jnp.maximum(cm, jnp.where(lane >= shift, pltpu.roll(cm, shift, axis=1), -jnp.inf))
            shift *= 2
        m_prev = m_ref[...]
        m_run = jnp.maximum(m_prev, cm)
        w_inter = jnp.exp(m_prev - m_run)
        e_negm = jnp.exp(-(g + m_run))
        m_last = jnp.broadcast_to(m_run[:, tl - 1:tl], (GATE_TILE, tl))
        g_last = jnp.broadcast_to(g[:, tl - 1:tl], (GATE_TILE, tl))
        env["w_s"] = jnp.exp(u - m_last)
        env["w_old"] = jnp.exp(m_prev - m_last)
        m_ref[...] = g_last + m_last

        env["u"] = u
        zeros8 = jnp.zeros((GATE_TILE, tl), F32)
        env["pack"] = _rows(_split3_f32(m_run) + [zeros8] + _split3_f32(w_inter) + [zeros8]
                            + _split3_f32(e_negm) + [zeros8] * 5).astype(BF16)

    def copies():
        pack_t = _dot_nt(ident_ref[...], env["pack"]).astype(BF16)
        env["cols"] = _dot(pack_t, selsum_ref[...])

    def head(h):
        qs = slice(h * ML_DQK, (h + 1) * ML_DQK)
        vs = slice(h * ML_DV, (h + 1) * ML_DV)
        q_bf, k_bf, cols = env["q"], env["k"], env["cols"]
        m_col = cols[:, h:h + 1]
        wi_rep = jnp.broadcast_to(cols[:, GATE_TILE + h:GATE_TILE + h + 1], (tl, ML_DV))
        en_rep = jnp.broadcast_to(cols[:, 2 * GATE_TILE + h:2 * GATE_TILE + h + 1], (tl, ML_DV))
        w_intra = jnp.exp((env["u"][h:h + 1, :] - m_col) + bias_ref[...])
        p = (_dot_nt(q_bf[:, qs], k_bf[:, qs]) * w_intra).astype(BF16)
        v_aug = jnp.concatenate([mv_ref[:, vs], jnp.ones((tl, ML_DV), BF16)], axis=1)
        c_aug = c_ref[h]
        intra = _dot(p, v_aug)
        inter = _dot(q_bf[:, qs], c_aug.astype(BF16))
        num = intra[:, 0:ML_DV] + wi_rep * inter[:, 0:ML_DV]
        den = intra[:, ML_DV:] + wi_rep * inter[:, ML_DV:]
        hh = num / jnp.maximum(jnp.abs(den), en_rep)
        kw_t = (env["k_t"][qs, :] * env["w_s"][h:h + 1, :]).astype(BF16)
        c_ref[h] = env["w_old"][h:h + 1, 0:ML_AUG] * c_aug + _dot(kw_t, v_aug)
        o_ref[:, vs] = (_head_rms(hh, ML_DV) * gn_ref[:, vs] * og_ref[:, vs].astype(F32)).astype(o_ref.dtype)

    return [scalars, copies] + [functools.partial(head, h) for h in range(ML_HEADS)]


FF_TILE = 256
HIDDEN_ROOM, DOWN_ROOM = 1, 1


def _layer_norm(x, g, b):
    mu = jnp.mean(x, axis=-1, keepdims=True)
    xc = x - mu
    var = jnp.mean(xc * xc, axis=-1, keepdims=True)
    return xc * lax.rsqrt(var + LN_EPS) * g + b


def _norm_phases(ohg_ref, oml_ref, x_ref, wo_ref, ln1g_ref, ln1b_ref, h1_ref, alpha):
    env = {}

    def project():
        env["mix"] = _dot(ohg_ref[...], wo_ref[0:HG_WIDTH, :]) + _dot(oml_ref[...], wo_ref[HG_WIDTH:, :])

    def norm():
        h1_ref[...] = _layer_norm(alpha * x_ref[...] + env["mix"], ln1g_ref[...], ln1b_ref[...])

    return project, norm


def _ffn_phases(h1_ref, p_ref, wg_ref, wu_ref, wd_ref, ln2g_ref, ln2b_ref, wpp_ref, wpg_ref, bpg_ref,
                out_ref, act_scr, ffn_scr, alpha, d_ff):
    env = {}
    d = h1_ref.shape[-1]

    def hidden(lo):
        if lo == 0:
            env["h1b"] = h1_ref[...].astype(BF16)
        h1b = env["h1b"]
        gate = _dot(h1b, wg_ref[:, lo:lo + FF_TILE])
        up = _dot(h1b, wu_ref[:, lo:lo + FF_TILE])
        act_scr[:, lo:lo + FF_TILE] = (_silu(gate) * up).astype(BF16)

    def down(lo):
        ffn_scr[:, lo:lo + FF_TILE] = _dot(act_scr[...], wd_ref[:, lo:lo + FF_TILE])

    def norm():
        env["pemb"] = _dot(p_ref[...].astype(BF16), wpp_ref[...])
        env["h2"] = _layer_norm(alpha * h1_ref[...] + ffn_scr[...], ln2g_ref[...], ln2b_ref[...])

    def embed():
        h2 = env["h2"]
        pgate = _sigmoid(_dot(h2.astype(BF16), wpg_ref[...]) + bpg_ref[...])
        out_ref[...] = h2 + pgate * env["pemb"]

    hidden_tiles = [functools.partial(hidden, lo) for lo in range(0, d_ff, FF_TILE)]
    down_tiles = [functools.partial(down, lo) for lo in range(0, d, FF_TILE)]
    room = [HIDDEN_ROOM] * len(hidden_tiles) + [DOWN_ROOM] * len(down_tiles)
    return hidden_tiles + down_tiles, room, norm, embed


N_MIX_IN = 10
N_MIX_CONST = 9
N_TAIL_CONST = 11


def _mixer_groups(hg, ml):
    hg_setup, *hg_chunks, hg_finish = hg
    ml_scalars, ml_copies, *ml_heads = ml
    groups = [[hg_setup, ml_scalars], hg_chunks[:1], [ml_copies] + hg_chunks[1:2]]
    rest_hg, rest_ml = hg_chunks[2:], ml_heads
    for j in range(max(len(rest_hg), len(rest_ml))):
        groups.append(rest_ml[j:j + 1] + rest_hg[j:j + 1])
    groups[-1].append(hg_finish)
    return groups


def _issue_order(ffn, room, groups):
    total = float(sum(room))
    order, done, seen = [], 0, 0.0
    for f, r in zip(ffn, room):
        order.append(f)
        seen += r
        upto = round(seen / total * len(groups))
        for group in groups[done:upto]:
            order.extend(group)
        done = max(done, upto)
    return order


def _fused_kernel(*refs, tl, sub, tiles_per_seq, alpha, d_ff):
    it = iter(refs)
    take = lambda n: [next(it) for _ in range(n)]
    (logf_ref, hq_ref, hk_ref, hv_ref, hg_ref, mq_ref, mk_ref, mv_ref, mo_ref, grow_ref) = take(N_MIX_IN)
    (hgn_ref, tril_ref, cw_ref, cb_ref, mgn_ref,
     ident_ref, triu_ref, bias_ref, selsum_ref) = take(N_MIX_CONST)
    x_ref, p_ref = take(2)
    (wo_ref, ln1g_ref, ln1b_ref, wg_ref, wu_ref, wd_ref,
     ln2g_ref, ln2b_ref, wpp_ref, wpg_ref, bpg_ref) = take(N_TAIL_CONST)
    (out_ref,) = take(1)
    st_ref, b_scr, c_scr, cbuf, c_ref, m_ref, ohg_scr, oml_scr, h1_scr, act_scr, ffn_scr = take(11)

    g = pl.program_id(0)

    @pl.when(g == 0)
    def _():
        h1_scr[...] = jnp.zeros_like(h1_scr)

    @pl.when(g % tiles_per_seq == 0)
    def _():
        st_ref[...] = jnp.zeros_like(st_ref)
        cbuf[0:CONV_PAD, :] = jnp.zeros((CONV_PAD, 2 * ML_QK_WIDTH), F32)
        c_ref[...] = jnp.zeros_like(c_ref)
        m_ref[...] = jnp.zeros_like(m_ref)

    cur = g % 2
    prev = 1 - cur
    groups = []
    for off in range(0, tl, sub):
        rows = pl.ds(off, sub)
        at = lambda ref: ref.at[rows]
        hg_phases = _hgrn2_phases(at(logf_ref), at(hq_ref), at(hk_ref), at(hv_ref), at(hg_ref), hgn_ref,
                                  tril_ref, ohg_scr.at[rows], st_ref, b_scr, c_scr, sub // CHUNK)
        ml_phases = _mlstm_phases(at(mq_ref), at(mk_ref), at(mv_ref), at(mo_ref), grow_ref.at[:, rows],
                                  cw_ref, cb_ref, mgn_ref, ident_ref, triu_ref, bias_ref, selsum_ref,
                                  oml_scr.at[rows], cbuf, c_ref, m_ref, sub)
        groups += _mixer_groups(hg_phases, ml_phases)
    project, norm1 = _norm_phases(ohg_scr, oml_scr, x_ref, wo_ref, ln1g_ref, ln1b_ref, h1_scr.at[cur], alpha)
    ffn, room, norm2, embed = _ffn_phases(h1_scr.at[prev], p_ref, wg_ref, wu_ref, wd_ref, ln2g_ref, ln2b_ref,
                                          wpp_ref, wpg_ref, bpg_ref, out_ref, act_scr, ffn_scr, alpha, d_ff)
    for f in _issue_order(ffn, room, groups) + [project, norm2, embed, norm1]:
        f()


def _fused(mix_in, hgn, conv_w, conv_b, mgn, x2, p2, tail_consts, batch, seq, alpha):
    t, d = x2.shape
    tl, sub = TAIL_TILE, MIX_TILE
    assert seq % tl == 0 and tl % sub == 0 and sub % CHUNK == 0 and sub >= ML_AUG
    tiles_per_seq = seq // tl
    n_tiles = t // tl
    d_ff = tail_consts[3].shape[1]
    assert d_ff % FF_TILE == 0
    mix_consts = (hgn, _chunk_tril(sub), conv_w, conv_b, mgn) + _mlstm_constants(sub)
    assert len(mix_in) == N_MIX_IN and len(mix_consts) == N_MIX_CONST and len(tail_consts) == N_TAIL_CONST

    def tile(lag):
        return lambda g: jnp.clip(g - lag, 0, n_tiles - 1)

    def const(a):
        return pl.BlockSpec(a.shape, lambda g: (0,) * a.ndim, pipeline_mode=pl.Buffered(1))

    mix_specs = [pl.BlockSpec((tl, a.shape[1]), lambda g: (tile(0)(g), 0)) for a in mix_in[:-1]]
    mix_specs.append(pl.BlockSpec((GATE_ROWS, tl), lambda g: (0, tile(0)(g))))
    in_specs = (mix_specs + [const(a) for a in mix_consts]
                + [pl.BlockSpec((tl, d), lambda g: (tile(0)(g), 0)),
                   pl.BlockSpec((tl, p2.shape[1]), lambda g: (tile(1)(g), 0))]
                + [const(a) for a in tail_consts])
    return pl.pallas_call(
        functools.partial(_fused_kernel, tl=tl, sub=sub, tiles_per_seq=tiles_per_seq, alpha=alpha, d_ff=d_ff),
        grid=(n_tiles + 1,),
        in_specs=in_specs,
        out_specs=pl.BlockSpec((tl, d), lambda g: (tile(1)(g), 0)),
        out_shape=jax.ShapeDtypeStruct((t, d), F32),
        scratch_shapes=[pltpu.VMEM((HG_HEADS, HG_DV, HG_DK), F32),
                        pltpu.VMEM((sub, HG_WIDTH), F32),
                        pltpu.VMEM((sub, HG_WIDTH), F32),
                        pltpu.VMEM((sub + CONV_PAD, 2 * ML_QK_WIDTH), F32),
                        pltpu.VMEM((ML_HEADS, ML_DQK, ML_AUG), F32),
                        pltpu.VMEM((GATE_TILE, sub), F32),
                        pltpu.VMEM((tl, HG_WIDTH), BF16),
                        pltpu.VMEM((tl, ML_WIDTH), BF16),
                        pltpu.VMEM((2, tl, d), F32),
                        pltpu.VMEM((tl, d_ff), BF16),
                        pltpu.VMEM((tl, d), F32)],
        compiler_params=pltpu.CompilerParams(dimension_semantics=("arbitrary",),
                                             vmem_limit_bytes=VMEM_LIMIT),
        name="mix_tail",
    )(*mix_in, *mix_consts, x2, p2, *tail_consts)


def _pick(n, candidates):
    for c in candidates:
        if n % c == 0:
            return c
    raise ValueError(f"no tile for {n}")


def kernel(x, p, w_in, b_in, hg_lb_logits, ml_conv_w, ml_conv_b, hg_norm_g, ml_norm_g, w_out, ln1_g, ln1_b,
           w_ffn_gate, w_ffn_up, w_ffn_down, ln2_g, ln2_b, ple_w_proj, ple_w_gate, ple_b_gate):
    batch, seq, d = x.shape
    depth = w_in.shape[0]
    t = batch * seq
    alpha = float((2 * depth) ** 0.25)
    tm = _pick(t, (1024, 512, 256))
    assert w_in.shape[2] == OFF_GATES + 2 * ML_HEADS
    assert depth == 1, "lower-bound cumsum is specialised to a single layer"

    x2 = x.reshape(t, d)
    for i in range(depth):
        w_i = w_in[i]
        w_bf = w_i.astype(BF16)
        b_row = b_in[i].reshape(1, -1)
        wg_t = w_i[:, OFF_GATES:].T
        gate_pad = ((0, GATE_TILE - ML_HEADS), (0, 0))
        wgt = jnp.concatenate([jnp.pad(wg_t[:ML_HEADS], gate_pad), jnp.pad(wg_t[ML_HEADS:], gate_pad)]).astype(BF16)
        bg = b_in[i, OFF_GATES:].reshape(2 * ML_HEADS, 1)
        bgt = jnp.concatenate([jnp.pad(bg[:ML_HEADS], gate_pad), jnp.pad(bg[ML_HEADS:], gate_pad)])

        mix_in = _inproj(x2, w_bf, b_row, wgt, bgt, hg_lb_logits, tm)
        tail_consts = (w_out[i].astype(BF16), ln1_g[i].reshape(1, d), ln1_b[i].reshape(1, d),
                       w_ffn_gate[i].astype(BF16), w_ffn_up[i].astype(BF16), w_ffn_down[i].astype(BF16),
                       ln2_g[i].reshape(1, d), ln2_b[i].reshape(1, d),
                       ple_w_proj[i].astype(BF16), ple_w_gate[i].astype(BF16), ple_b_gate[i].reshape(1, d))
        x2 = _fused(mix_in, hg_norm_g[i].reshape(1, HG_WIDTH), ml_conv_w[i], ml_conv_b[i].reshape(1, -1),
                    ml_norm_g[i].reshape(1, ML_WIDTH), x2, p[i].reshape(t, -1), tail_consts, batch, seq, alpha)
    return x2.reshape(batch, seq, d)
```

```python
import functools

import numpy as np
import jax
import jax.numpy as jnp
from jax import lax
from jax.experimental import pallas as pl
from jax.experimental.pallas import tpu as pltpu

F32 = jnp.float32
BF16 = jnp.bfloat16

CHUNK = 64
SUB = 16
N_SUB = CHUNK // SUB
EX = 8
N_EX = CHUNK // EX
EX_PER_SUB = SUB // EX
LOG2E = 1.4426950408889634
HG_HEADS = 4
HG_DK = 128
HG_DV = 128
HG_WIDTH = HG_HEADS * HG_DV
ML_HEADS = 4
ML_DQK = 64
ML_DV = 128
ML_WIDTH = ML_HEADS * ML_DV
ML_QK_WIDTH = ML_HEADS * ML_DQK
CONV_K = 4
LN_EPS = 1e-5
RMS_EPS = 1e-6

OFF_HQ = 0
OFF_HF = OFF_HQ + HG_HEADS * HG_DK
OFF_HV = OFF_HF + HG_HEADS * HG_DK
OFF_HG = OFF_HV + HG_WIDTH
OFF_MQ = OFF_HG + HG_WIDTH
OFF_MK = OFF_MQ + ML_QK_WIDTH
OFF_MV = OFF_MK + ML_QK_WIDTH
OFF_MO = OFF_MV + ML_WIDTH
OFF_GATES = OFF_MO + ML_WIDTH
LANE = 128
GATE_TILE = 8
GATE_ROWS = 2 * GATE_TILE
PROJ_PIECE = 256

VMEM_LIMIT = 60 * 1024 * 1024
MIX_TILE = 256
TAIL_TILE = 512


def _sigmoid(x):
    return 0.5 * jnp.tanh(0.5 * x) + 0.5


def _silu(x):
    h = 0.5 * x
    return h + h * jnp.tanh(h)


def _log_sigmoid(x):
    return jnp.minimum(x, 0.0) - jnp.log(1.0 + jnp.exp(-jnp.abs(x)))


def _split3(x):
    hi = x.astype(BF16)
    r1 = x - hi.astype(F32)
    mid = r1.astype(BF16)
    lo = (r1 - mid.astype(F32)).astype(BF16)
    return hi, mid, lo


def _split3_f32(x):
    return [s.astype(F32) for s in _split3(x)]


def _dot(a, b):
    return jnp.dot(a, b, preferred_element_type=F32)


def _dot_nt(a, b):
    return lax.dot_general(a, b, (((1,), (1,)), ((), ())), preferred_element_type=F32)


def _dot_tn(a, b):
    return lax.dot_general(a, b, (((0,), (0,)), ((), ())), preferred_element_type=F32)


def _rows(blocks):
    return jnp.concatenate(blocks, axis=0)


def _cumsum_rows(tril_bf, x):
    hi, mid, lo = _split3(x)
    return _dot(tril_bf, hi) + _dot(tril_bf, mid) + _dot(tril_bf, lo)


def _head_rms(o, width):
    ms = jnp.sum(o * o, axis=-1, keepdims=True) * (1.0 / width)
    return o * lax.rsqrt(ms + RMS_EPS)


def _inproj_kernel(x_ref, w_ref, b_ref, wgt_ref, bgt_ref, lbl_ref, logf_ref, pk_ref, grow_ref):
    xb = x_ref[...].astype(BF16)

    logits = lbl_ref[...]
    mx = jnp.max(logits, axis=0, keepdims=True)
    ex = jnp.exp(logits - mx)
    den = jnp.sum(ex, axis=0, keepdims=True)
    lb = ex[0:1, :] / den
    one_m_lb = (den - ex[0:1, :]) / den

    def act(fn):
        def store(u, off, cols):
            pk_ref[:, pl.ds(off + cols.start, n)] = fn(u).astype(BF16)
        return store

    plain = act(lambda u: u)

    def forget(u, off, cols):
        sig = _sigmoid(u)
        logf_ref[:, cols] = jnp.log(lb[:, cols] + one_m_lb[:, cols] * sig)
        pk_ref[:, pl.ds(off + cols.start, n)] = (one_m_lb[:, cols] * (1.0 - sig)).astype(BF16)

    n = PROJ_PIECE
    pieces = {"hq": (OFF_HQ, HG_WIDTH, act(_silu)), "hf": (OFF_HF, HG_WIDTH, forget),
              "hv": (OFF_HV, HG_WIDTH, plain), "hg": (OFF_HG, HG_WIDTH, act(_silu)),
              "mq": (OFF_MQ, ML_QK_WIDTH, plain), "mk": (OFF_MK, ML_QK_WIDTH, plain),
              "mv": (OFF_MV, ML_WIDTH, plain), "mo": (OFF_MO, ML_WIDTH, act(_sigmoid))}
    order = ["hq", "hv", "hf", "mq", "hg", "hv", "mo", "mk", "hq", "mv", "hf", "mv", "hg", "mo"]
    taken = {name: 0 for name in pieces}
    for name in order:
        off, width, store = pieces[name]
        lo = taken[name]
        taken[name] = lo + n
        store(_dot(xb, w_ref[:, off + lo:off + lo + n]) + b_ref[:, off + lo:off + lo + n], off, slice(lo, lo + n))
    assert all(taken[name] == pieces[name][1] for name in pieces)

    gt = _dot_nt(wgt_ref[...], xb) + bgt_ref[...]
    sub = lax.broadcasted_iota(jnp.int32, gt.shape, 0)
    is_fgate = (sub >= GATE_TILE) & (sub < GATE_TILE + ML_HEADS)
    grow_ref[...] = jnp.where(is_fgate, _log_sigmoid(gt), gt)


def _inproj(x2, w_bf, b_row, wgt, bgt, lb_logits, tm):
    t, d = x2.shape
    grid = (t // tm,)
    row = lambda i: (i, 0)
    const = lambda i: (0, 0)
    out_shapes = (
        jax.ShapeDtypeStruct((t, HG_WIDTH), F32),
        jax.ShapeDtypeStruct((t, OFF_GATES), BF16),
        jax.ShapeDtypeStruct((GATE_ROWS, t), F32),
    )
    out_specs = (
        pl.BlockSpec((tm, HG_WIDTH), row),
        pl.BlockSpec((tm, OFF_GATES), row),
        pl.BlockSpec((GATE_ROWS, tm), lambda i: (0, i)),
    )
    in_specs = [
        pl.BlockSpec((tm, d), row),
        pl.BlockSpec((d, OFF_GATES), const, pipeline_mode=pl.Buffered(1)),
        pl.BlockSpec((1, OFF_GATES), const),
        pl.BlockSpec(wgt.shape, const),
        pl.BlockSpec(bgt.shape, const),
        pl.BlockSpec(lb_logits.shape, const),
    ]
    return pl.pallas_call(
        _inproj_kernel, grid=grid, in_specs=in_specs, out_specs=out_specs, out_shape=out_shapes,
        compiler_params=pltpu.CompilerParams(dimension_semantics=("arbitrary",),
                                             vmem_limit_bytes=VMEM_LIMIT),
        name="inproj",
    )(x2, w_bf, b_row, wgt, bgt, lb_logits)


def _chunk_tril(ts):
    idx = np.arange(ts)
    return jnp.asarray((idx[:, None] >= idx[None, :]) & (idx[:, None] // CHUNK == idx[None, :] // CHUNK), BF16)


def _hgrn2_phases(logf_ref, q_ref, k_ref, v_ref, og_ref, gn_ref, tril_ref, o_ref,
                  st_ref, b_scr, c_scr, n_chunks):
    row = lax.broadcasted_iota(jnp.int32, (CHUNK, CHUNK), 0)
    col = lax.broadcasted_iota(jnp.int32, (CHUNK, CHUNK), 1)
    exact_mask = (col <= row) & (row // EX == col // EX)
    pair_mask = row // SUB == col // SUB
    key_pos = col % EX
    env = {}

    def blocks(vals):
        return _rows([jnp.broadcast_to(jnp.asarray(x, F32), (EX, HG_WIDTH)) for x in vals])

    def setup():
        b_scr[...] = _cumsum_rows(tril_ref[...], logf_ref[...]) * LOG2E
        c_scr[...] = b_scr[...] - jnp.log2(k_ref[...].astype(F32))
        env["states"] = [st_ref[h] for h in range(HG_HEADS)]

    def scores(c):
        r0 = c * CHUNK
        rows = pl.ds(r0, CHUNK)
        b2 = b_scr[rows, :]
        c2 = c_scr[rows, :]
        qf = q_ref[rows, :].astype(F32)

        bd = [jnp.zeros((1, HG_WIDTH), F32)]
        bd += [b_scr[pl.ds(r0 + EX * m - 1, 1), :] for m in range(1, N_EX + 1)]
        per_sub = EX_PER_SUB
        sub_start = blocks([bd[(m // per_sub) * per_sub] for m in range(N_EX)])
        q_st = qf * jnp.exp2(b2 - sub_start)
        q_in = (q_st * jnp.exp2(sub_start)).astype(BF16)
        k_dec = jnp.exp2(bd[N_EX] - c2).astype(BF16)
        chunk_decay = jnp.exp2(bd[N_EX])
        q_half = qf * jnp.exp2(b2 - blocks([bd[m] if m % per_sub else jnp.inf for m in range(N_EX)]))
        k_half = jnp.exp2(blocks([-jnp.inf if m % per_sub else bd[m + 1] for m in range(N_EX)]) - c2)

        a_heads = []
        for h in range(HG_HEADS):
            hs = slice(h * HG_DK, (h + 1) * HG_DK)
            exact = jnp.zeros((CHUNK, CHUNK), F32)
            for j in range(EX):
                c_j = _rows([jnp.broadcast_to(c_scr[pl.ds(r0 + m * EX + j, 1), hs], (EX, HG_DK))
                            for m in range(N_EX)])
                prod = qf[:, hs] * jnp.exp2(b2[:, hs] - c_j)
                exact = jnp.where(key_pos == j, jnp.sum(prod, axis=-1, keepdims=True), exact)
            half = _dot_nt(q_half[:, hs].astype(BF16), k_half[:, hs].astype(BF16))
            off = [jnp.zeros((SUB, CHUNK), F32)]
            for i in range(1, N_SUB):
                k_i = jnp.exp2(bd[i * per_sub][:, hs] - c2[0:i * SUB, hs])
                k_i = _rows([k_i, jnp.zeros((CHUNK - i * SUB, HG_DK), F32)])
                off.append(_dot_nt(q_st[i * SUB:(i + 1) * SUB, hs].astype(BF16), k_i.astype(BF16)))
            a = jnp.where(exact_mask, exact, jnp.where(pair_mask, half, 0.0)) + _rows(off)
            a_heads.append(a.astype(BF16))
        env[c] = (a_heads, q_in, k_dec, chunk_decay)

    def outputs(c):
        a_heads, q_in, k_dec, chunk_decay = env.pop(c)
        states = env["states"]
        rows = pl.ds(c * CHUNK, CHUNK)
        v = v_ref[rows, :]
        outs = []
        for h in range(HG_HEADS):
            hs = slice(h * HG_DK, (h + 1) * HG_DK)
            st = states[h]
            o_h = _dot(a_heads[h], v[:, hs]) + _dot_nt(q_in[:, hs], st.astype(BF16))
            states[h] = st * chunk_decay[:, hs] + _dot_tn(v[:, hs], k_dec[:, hs])
            outs.append(_head_rms(o_h, HG_DV))
        o = jnp.concatenate(outs, axis=-1)
        o_ref[rows, :] = (o * gn_ref[...] * og_ref[rows, :].astype(F32)).astype(o_ref.dtype)

    def finish():
        for h in range(HG_HEADS):
            st_ref[h] = env["states"][h]

    phases = [setup, functools.partial(scores, 0)]
    for c in range(n_chunks):
        if c + 1 < n_chunks:
            phases.append(functools.partial(scores, c + 1))
        phases.append(functools.partial(outputs, c))
    phases.append(finish)
    return phases


CONV_PAD = 8
SEL_ROWS = 128
GRP_M, GRP_WI, GRP_EN = 0, 32, 64
ML_AUG = 2 * ML_DV


def _mlstm_constants(tl):
    ident = np.eye(tl, dtype=np.float32)
    triu = np.triu(np.ones((tl, tl), np.float32))
    bias = np.where(np.tril(np.ones((tl, tl), bool)), 0.0, -np.inf).astype(np.float32)
    sel_sum = np.zeros((SEL_ROWS, LANE), np.float32)
    for q, grp in enumerate((GRP_M, GRP_WI, GRP_EN)):
        for h in range(ML_HEADS):
            for k in range(3):
                sel_sum[grp + GATE_TILE * k + h, q * GATE_TILE + h] = 1.0
    return (jnp.asarray(ident, BF16), jnp.asarray(triu, BF16), jnp.asarray(bias), jnp.asarray(sel_sum, BF16))


def _mlstm_phases(mq_ref, mk_ref, mv_ref, og_ref, grow_ref, cw_ref, cb_ref, gn_ref,
                  ident_ref, triu_ref, bias_ref, selsum_ref, o_ref,
                  cbuf, c_ref, m_ref, tl):
    env = {}

    def scalars():
        cbuf[CONV_PAD:CONV_PAD + tl, 0:ML_QK_WIDTH] = mq_ref[...].astype(F32)
        cbuf[CONV_PAD:CONV_PAD + tl, ML_QK_WIDTH:] = mk_ref[...].astype(F32)
        acc = cb_ref[...] + cw_ref[0:1, :] * cbuf[pl.ds(CONV_PAD - (CONV_K - 1), tl), :]
        for tap in range(1, CONV_K):
            acc = acc + cw_ref[tap:tap + 1, :] * cbuf[pl.ds(CONV_PAD - (CONV_K - 1) + tap, tl), :]
        qk = _silu(acc)
        cbuf[0:CONV_PAD, :] = cbuf[tl:tl + CONV_PAD, :]
        env["q"] = (qk[:, 0:ML_QK_WIDTH] * (ML_DQK ** -0.5)).astype(BF16)
        k = qk[:, ML_QK_WIDTH:]
        env["k"] = k.astype(BF16)
        env["k_t"] = k.T

        gates = grow_ref[...]
        i_g = gates[0:GATE_TILE, :]
        hi, mid, lo = _split3(gates[GATE_TILE:, :])
        triu = triu_ref[...]
        part = _dot(_rows([hi, mid]), triu)
        g = part[0:GATE_TILE, :] + part[GATE_TILE:, :] + _dot(_rows([lo, lo]), triu)[0:GATE_TILE, :]
        u = i_g - g
        lane = lax.broadcasted_iota(jnp.int32, (GATE_TILE, tl), 1)
        cm = u
        shift = 1
        while shift < tl:
            cm = jnp.maximum(cm, jnp.where(lane >= shift, pltpu.roll(cm, shift, axis=1), -jnp.inf))
            shift *= 2
        m_prev = m_ref[...]
        m_run = jnp.maximum(m_prev, cm)
        w_inter = jnp.exp(m_prev - m_run)
        e_negm = jnp.exp(-(g + m_run))
        m_last = jnp.broadcast_to(m_run[:, tl - 1:tl], (GATE_TILE, tl))
        g_last = jnp.broadcast_to(g[:, tl - 1:tl], (GATE_TILE, tl))
        env["w_s"] = jnp.exp(u - m_last)
        env["w_old"] = jnp.exp(m_prev - m_last)
        m_ref[...] = g_last + m_last

        env["u"] = u
        zeros8 = jnp.zeros((GATE_TILE, tl), F32)
        env["pack"] = _rows(_split3_f32(m_run) + [zeros8] + _split3_f32(w_inter) + [zeros8]
                            + _split3_f32(e_negm) + [zeros8] * 5).astype(BF16)

    def copies():
        pack_t = _dot_nt(ident_ref[...], env["pack"]).astype(BF16)
        env["cols"] = _dot(pack_t, selsum_ref[...])

    def head(h):
        qs = slice(h * ML_DQK, (h + 1) * ML_DQK)
        vs = slice(h * ML_DV, (h + 1) * ML_DV)
        q_bf, k_bf, cols = env["q"], env["k"], env["cols"]
        m_col = cols[:, h:h + 1]
        wi_rep = jnp.broadcast_to(cols[:, GATE_TILE + h:GATE_TILE + h + 1], (tl, ML_DV))
        en_rep = jnp.broadcast_to(cols[:, 2 * GATE_TILE + h:2 * GATE_TILE + h + 1], (tl, ML_DV))
        w_intra = jnp.exp((env["u"][h:h + 1, :] - m_col) + bias_ref[...])
        p = (_dot_nt(q_bf[:, qs], k_bf[:, qs]) * w_intra).astype(BF16)
        v_aug = jnp.concatenate([mv_ref[:, vs], jnp.ones((tl, ML_DV), BF16)], axis=1)
        c_aug = c_ref[h]
        intra = _dot(p, v_aug)
        inter = _dot(q_bf[:, qs], c_aug.astype(BF16))
        num = intra[:, 0:ML_DV] + wi_rep * inter[:, 0:ML_DV]
        den = intra[:, ML_DV:] + wi_rep * inter[:, ML_DV:]
        hh = num / jnp.maximum(jnp.abs(den), en_rep)
        kw_t = (env["k_t"][qs, :] * env["w_s"][h:h + 1, :]).astype(BF16)
        c_ref[h] = env["w_old"][h:h + 1, 0:ML_AUG] * c_aug + _dot(kw_t, v_aug)
        o_ref[:, vs] = (_head_rms(hh, ML_DV) * gn_ref[:, vs] * og_ref[:, vs].astype(F32)).astype(o_ref.dtype)

    return [scalars, copies] + [functools.partial(head, h) for h in range(ML_HEADS)]


FF_TILE = 256
HIDDEN_ROOM, DOWN_ROOM = 1, 1


def _layer_norm(x, g, b):
    mu = jnp.mean(x, axis=-1, keepdims=True)
    xc = x - mu
    var = jnp.mean(xc * xc, axis=-1, keepdims=True)
    return xc * lax.rsqrt(var + LN_EPS) * g + b


def _norm_phases(ohg_ref, oml_ref, x_ref, wo_ref, ln1g_ref, ln1b_ref, h1_ref, alpha):
    env = {}

    def project():
        env["mix"] = _dot(ohg_ref[...], wo_ref[0:HG_WIDTH, :]) + _dot(oml_ref[...], wo_ref[HG_WIDTH:, :])

    def norm():
        h1_ref[...] = _layer_norm(alpha * x_ref[...] + env["mix"], ln1g_ref[...], ln1b_ref[...])

    return project, norm


def _ffn_phases(h1_ref, p_ref, wg_ref, wu_ref, wd_ref, ln2g_ref, ln2b_ref, wpp_ref, wpg_ref, bpg_ref,
                out_ref, act_scr, ffn_scr, alpha, d_ff):
    env = {}
    d = h1_ref.shape[-1]

    def gate(lo):
        if lo == 0:
            env["h1b"] = h1_ref[...].astype(BF16)
        env["gate"] = _silu(_dot(env["h1b"], wg_ref[:, lo:lo + FF_TILE]))

    def up(lo):
        up_proj = _dot(env["h1b"], wu_ref[:, lo:lo + FF_TILE])
        act_scr[:, lo:lo + FF_TILE] = (env.pop("gate") * up_proj).astype(BF16)

    def down(lo):
        ffn_scr[:, lo:lo + FF_TILE] = _dot(act_scr[...], wd_ref[:, lo:lo + FF_TILE])

    def norm():
        env["pemb"] = _dot(p_ref[...].astype(BF16), wpp_ref[...])
        env["h2"] = _layer_norm(alpha * h1_ref[...] + ffn_scr[...], ln2g_ref[...], ln2b_ref[...])

    def embed():
        h2 = env["h2"]
        pgate = _sigmoid(_dot(h2.astype(BF16), wpg_ref[...]) + bpg_ref[...])
        out_ref[...] = h2 + pgate * env["pemb"]

    hidden_tiles = [functools.partial(f, lo) for lo in range(0, d_ff, FF_TILE) for f in (gate, up)]
    down_tiles = [functools.partial(down, lo) for lo in range(0, d, FF_TILE)]
    room = [HIDDEN_ROOM] * len(hidden_tiles) + [DOWN_ROOM] * len(down_tiles)
    return hidden_tiles + down_tiles, room, norm, embed


N_MIX_IN = 3
N_MIX_CONST = 9
N_TAIL_CONST = 11


def _mixer_groups(hg, ml):
    hg_setup, *hg_chunks, hg_finish = hg
    ml_scalars, ml_copies, *ml_heads = ml
    groups = [[hg_setup, ml_scalars], hg_chunks[:1], [ml_copies], hg_chunks[1:2]]
    rest_hg, rest_ml = hg_chunks[2:], ml_heads
    per_head = -(-len(rest_hg) // len(rest_ml))
    for j, head in enumerate(rest_ml):
        groups.append([head])
        groups += [[f] for f in rest_hg[j * per_head:(j + 1) * per_head]]
    groups[-1].append(hg_finish)
    return groups


def _issue_order(ffn, room, groups):
    total = float(sum(room))
    order, done, seen = [], 0, 0.0
    for f, r in zip(ffn, room):
        order.append(f)
        seen += r
        upto = round(seen / total * len(groups))
        for group in groups[done:upto]:
            order.extend(group)
        done = max(done, upto)
    return order


def _fused_kernel(*refs, tl, sub, tiles_per_seq, alpha, d_ff):
    it = iter(refs)
    take = lambda n: [next(it) for _ in range(n)]
    logf_ref, pk_ref, grow_ref = take(N_MIX_IN)
    (hgn_ref, tril_ref, cw_ref, cb_ref, mgn_ref,
     ident_ref, triu_ref, bias_ref, selsum_ref) = take(N_MIX_CONST)
    x_ref, p_ref = take(2)
    (wo_ref, ln1g_ref, ln1b_ref, wg_ref, wu_ref, wd_ref,
     ln2g_ref, ln2b_ref, wpp_ref, wpg_ref, bpg_ref) = take(N_TAIL_CONST)
    (out_ref,) = take(1)
    st_ref, b_scr, c_scr, cbuf, c_ref, m_ref, ohg_scr, oml_scr, h1_scr, act_scr, ffn_scr = take(11)

    g = pl.program_id(0)

    @pl.when(g == 0)
    def _():
        h1_scr[...] = jnp.zeros_like(h1_scr)

    @pl.when(g % tiles_per_seq == 0)
    def _():
        st_ref[...] = jnp.zeros_like(st_ref)
        cbuf[0:CONV_PAD, :] = jnp.zeros((CONV_PAD, 2 * ML_QK_WIDTH), F32)
        c_ref[...] = jnp.zeros_like(c_ref)
        m_ref[...] = jnp.zeros_like(m_ref)

    cur = g % 2
    prev = 1 - cur
    groups = []
    for off in range(0, tl, sub):
        rows = pl.ds(off, sub)
        pk = lambda col0, width: pk_ref.at[rows, pl.ds(col0, width)]
        hg_phases = _hgrn2_phases(logf_ref.at[rows], pk(OFF_HQ, HG_WIDTH), pk(OFF_HF, HG_WIDTH),
                                  pk(OFF_HV, HG_WIDTH), pk(OFF_HG, HG_WIDTH), hgn_ref,
                                  tril_ref, ohg_scr.at[rows], st_ref, b_scr, c_scr, sub // CHUNK)
        ml_phases = _mlstm_phases(pk(OFF_MQ, ML_QK_WIDTH), pk(OFF_MK, ML_QK_WIDTH), pk(OFF_MV, ML_WIDTH),
                                  pk(OFF_MO, ML_WIDTH), grow_ref.at[:, rows],
                                  cw_ref, cb_ref, mgn_ref, ident_ref, triu_ref, bias_ref, selsum_ref,
                                  oml_scr.at[rows], cbuf, c_ref, m_ref, sub)
        groups += _mixer_groups(hg_phases, ml_phases)
    project, norm1 = _norm_phases(ohg_scr, oml_scr, x_ref, wo_ref, ln1g_ref, ln1b_ref, h1_scr.at[cur], alpha)
    ffn, room, norm2, embed = _ffn_phases(h1_scr.at[prev], p_ref, wg_ref, wu_ref, wd_ref, ln2g_ref, ln2b_ref,
                                          wpp_ref, wpg_ref, bpg_ref, out_ref, act_scr, ffn_scr, alpha, d_ff)
    for f in _issue_order(ffn, room, groups) + [project, norm2, embed, norm1]:
        f()


def _fused(mix_in, hgn, conv_w, conv_b, mgn, x2, p2, tail_consts, batch, seq, alpha):
    t, d = x2.shape
    tl, sub = TAIL_TILE, MIX_TILE
    assert seq % tl == 0 and tl % sub == 0 and sub % CHUNK == 0 and sub >= ML_AUG
    tiles_per_seq = seq // tl
    n_tiles = t // tl
    d_ff = tail_consts[3].shape[1]
    assert d_ff % FF_TILE == 0
    mix_consts = (hgn, _chunk_tril(sub), conv_w, conv_b, mgn) + _mlstm_constants(sub)
    assert len(mix_in) == N_MIX_IN and len(mix_consts) == N_MIX_CONST and len(tail_consts) == N_TAIL_CONST

    def tile(lag):
        return lambda g: jnp.clip(g - lag, 0, n_tiles - 1)

    def const(a):
        return pl.BlockSpec(a.shape, lambda g: (0,) * a.ndim, pipeline_mode=pl.Buffered(1))

    mix_specs = [pl.BlockSpec((tl, a.shape[1]), lambda g: (tile(0)(g), 0)) for a in mix_in[:-1]]
    mix_specs.append(pl.BlockSpec((GATE_ROWS, tl), lambda g: (0, tile(0)(g))))
    in_specs = (mix_specs + [const(a) for a in mix_consts]
                + [pl.BlockSpec((tl, d), lambda g: (tile(0)(g), 0)),
                   pl.BlockSpec((tl, p2.shape[1]), lambda g: (tile(1)(g), 0))]
                + [const(a) for a in tail_consts])
    return pl.pallas_call(
        functools.partial(_fused_kernel, tl=tl, sub=sub, tiles_per_seq=tiles_per_seq, alpha=alpha, d_ff=d_ff),
        grid=(n_tiles + 1,),
        in_specs=in_specs,
        out_specs=pl.BlockSpec((tl, d), lambda g: (tile(1)(g), 0)),
        out_shape=jax.ShapeDtypeStruct((t, d), F32),
        scratch_shapes=[pltpu.VMEM((HG_HEADS, HG_DV, HG_DK), F32),
                        pltpu.VMEM((sub, HG_WIDTH), F32),
                        pltpu.VMEM((sub, HG_WIDTH), F32),
                        pltpu.VMEM((sub + CONV_PAD, 2 * ML_QK_WIDTH), F32),
                        pltpu.VMEM((ML_HEADS, ML_DQK, ML_AUG), F32),
                        pltpu.VMEM((GATE_TILE, sub), F32),
                        pltpu.VMEM((tl, HG_WIDTH), BF16),
                        pltpu.VMEM((tl, ML_WIDTH), BF16),
                        pltpu.VMEM((2, tl, d), F32),
                        pltpu.VMEM((tl, d_ff), BF16),
                        pltpu.VMEM((tl, d), F32)],
        compiler_params=pltpu.CompilerParams(dimension_semantics=("arbitrary",),
                                             vmem_limit_bytes=VMEM_LIMIT),
        name="mix_tail",
    )(*mix_in, *mix_consts, x2, p2, *tail_consts)


def _pick(n, candidates):
    for c in candidates:
        if n % c == 0:
            return c
    raise ValueError(f"no tile for {n}")


def kernel(x, p, w_in, b_in, hg_lb_logits, ml_conv_w, ml_conv_b, hg_norm_g, ml_norm_g, w_out, ln1_g, ln1_b,
           w_ffn_gate, w_ffn_up, w_ffn_down, ln2_g, ln2_b, ple_w_proj, ple_w_gate, ple_b_gate):
    batch, seq, d = x.shape
    depth = w_in.shape[0]
    t = batch * seq
    alpha = float((2 * depth) ** 0.25)
    tm = _pick(t, (1024, 512, 256))
    assert w_in.shape[2] == OFF_GATES + 2 * ML_HEADS
    assert depth == 1, "lower-bound cumsum is specialised to a single layer"

    x2 = x.reshape(t, d)
    for i in range(depth):
        w_i = w_in[i]
        w_bf = w_i.astype(BF16)
        b_row = b_in[i].reshape(1, -1)
        wg_t = w_i[:, OFF_GATES:].T
        gate_pad = ((0, GATE_TILE - ML_HEADS), (0, 0))
        wgt = jnp.concatenate([jnp.pad(wg_t[:ML_HEADS], gate_pad), jnp.pad(wg_t[ML_HEADS:], gate_pad)]).astype(BF16)
        bg = b_in[i, OFF_GATES:].reshape(2 * ML_HEADS, 1)
        bgt = jnp.concatenate([jnp.pad(bg[:ML_HEADS], gate_pad), jnp.pad(bg[ML_HEADS:], gate_pad)])

        mix_in = _inproj(x2, w_bf, b_row, wgt, bgt, hg_lb_logits, tm)
        tail_consts = (w_out[i].astype(BF16), ln1_g[i].reshape(1, d), ln1_b[i].reshape(1, d),
                       w_ffn_gate[i].astype(BF16), w_ffn_up[i].astype(BF16), w_ffn_down[i].astype(BF16),
                       ln2_g[i].reshape(1, d), ln2_b[i].reshape(1, d),
                       ple_w_proj[i].astype(BF16), ple_w_gate[i].astype(BF16), ple_b_gate[i].reshape(1, d))
        x2 = _fused(mix_in, hg_norm_g[i].reshape(1, HG_WIDTH), ml_conv_w[i], ml_conv_b[i].reshape(1, -1),
                    ml_norm_g[i].reshape(1, ML_WIDTH), x2, p[i].reshape(t, -1), tail_consts, batch, seq, alpha)
    return x2.reshape(batch, seq, d)
```

```python
import functools

import numpy as np
import jax
import jax.numpy as jnp
from jax import lax
from jax.experimental import pallas as pl
from jax.experimental.pallas import tpu as pltpu

F32 = jnp.float32
BF16 = jnp.bfloat16

CHUNK = 64
SUB = 16
N_SUB = CHUNK // SUB
EX = 8
N_EX = CHUNK // EX
EX_PER_SUB = SUB // EX
LOG2E = 1.4426950408889634
HG_HEADS = 4
HG_DK = 128
HG_DV = 128
HG_WIDTH = HG_HEADS * HG_DV
ML_HEADS = 4
ML_DQK = 64
ML_DV = 128
ML_WIDTH = ML_HEADS * ML_DV
ML_QK_WIDTH = ML_HEADS * ML_DQK
CONV_K = 4
LN_EPS = 1e-5
RMS_EPS = 1e-6

OFF_HQ = 0
OFF_HF = OFF_HQ + HG_HEADS * HG_DK
OFF_HV = OFF_HF + HG_HEADS * HG_DK
OFF_HG = OFF_HV + HG_WIDTH
OFF_MQ = OFF_HG + HG_WIDTH
OFF_MK = OFF_MQ + ML_QK_WIDTH
OFF_MV = OFF_MK + ML_QK_WIDTH
OFF_MO = OFF_MV + ML_WIDTH
OFF_GATES = OFF_MO + ML_WIDTH
LANE = 128
GATE_TILE = 8
GATE_ROWS = 2 * GATE_TILE
PROJ_PIECE = 256

VMEM_LIMIT = 60 * 1024 * 1024
MIX_TILE = 256
TAIL_TILE = 512
INPROJ_TILE = 1024


def _sigmoid(x):
    return 0.5 * jnp.tanh(0.5 * x) + 0.5


def _silu(x):
    return x * _sigmoid(x)


def _log_sigmoid(x):
    return jnp.minimum(x, 0.0) - jnp.log(1.0 + jnp.exp(-jnp.abs(x)))


def _split3(x):
    hi = x.astype(BF16)
    r1 = x - hi.astype(F32)
    mid = r1.astype(BF16)
    lo = (r1 - mid.astype(F32)).astype(BF16)
    return hi, mid, lo


def _split3_f32(x):
    return [s.astype(F32) for s in _split3(x)]


def _dot(a, b):
    return jnp.dot(a, b, preferred_element_type=F32)


def _dot_nt(a, b):
    return lax.dot_general(a, b, (((1,), (1,)), ((), ())), preferred_element_type=F32)


def _dot_tn(a, b):
    return lax.dot_general(a, b, (((0,), (0,)), ((), ())), preferred_element_type=F32)


def _rows(blocks):
    return jnp.concatenate(blocks, axis=0)


def _cumsum_rows(tril_bf, x):
    hi, mid, lo = _split3(x)
    return _dot(tril_bf, hi) + _dot(tril_bf, mid) + _dot(tril_bf, lo)


def _head_rms(o, width):
    ms = jnp.sum(o * o, axis=-1, keepdims=True) * (1.0 / width)
    return o * lax.rsqrt(ms + RMS_EPS)


def _inproj_kernel(x_ref, w_ref, b_ref, wgt_ref, bgt_ref, lbl_ref,
                   logf_ref, hq_ref, hk_ref, hv_ref, hg_ref,
                   mq_ref, mk_ref, mv_ref, mo_ref, grow_ref):
    xb = x_ref[...].astype(BF16)

    logits = lbl_ref[...]
    mx = jnp.max(logits, axis=0, keepdims=True)
    ex = jnp.exp(logits - mx)
    den = jnp.sum(ex, axis=0, keepdims=True)
    lb = ex[0:1, :] / den
    one_m_lb = (den - ex[0:1, :]) / den

    def plain(out_ref):
        def store(u, cols):
            out_ref[:, cols] = u.astype(BF16)
        return store

    def act(out_ref, fn):
        def store(u, cols):
            out_ref[:, cols] = fn(u).astype(BF16)
        return store

    def forget(u, cols):
        sig = _sigmoid(u)
        logf_ref[:, cols] = jnp.log(lb[:, cols] + one_m_lb[:, cols] * sig)
        hk_ref[:, cols] = (one_m_lb[:, cols] * (1.0 - sig)).astype(BF16)

    n = PROJ_PIECE
    pieces = {"hq": (OFF_HQ, HG_WIDTH, act(hq_ref, _silu)), "hf": (OFF_HF, HG_WIDTH, forget),
              "hv": (OFF_HV, HG_WIDTH, plain(hv_ref)), "hg": (OFF_HG, HG_WIDTH, act(hg_ref, _silu)),
              "mq": (OFF_MQ, ML_QK_WIDTH, plain(mq_ref)), "mk": (OFF_MK, ML_QK_WIDTH, plain(mk_ref)),
              "mv": (OFF_MV, ML_WIDTH, plain(mv_ref)), "mo": (OFF_MO, ML_WIDTH, act(mo_ref, _sigmoid))}
    order = ["hq", "hv", "hf", "mq", "hg", "hv", "mo", "mk", "hq", "mv", "hf", "mv", "hg", "mo"]
    taken = {name: 0 for name in pieces}
    for name in order:
        off, width, store = pieces[name]
        lo = taken[name]
        taken[name] = lo + n
        store(_dot(xb, w_ref[:, off + lo:off + lo + n]) + b_ref[:, off + lo:off + lo + n], slice(lo, lo + n))
    assert all(taken[name] == pieces[name][1] for name in pieces)

    gt = _dot_nt(wgt_ref[...], xb) + bgt_ref[...]
    sub = lax.broadcasted_iota(jnp.int32, gt.shape, 0)
    is_fgate = (sub >= GATE_TILE) & (sub < GATE_TILE + ML_HEADS)
    grow_ref[...] = jnp.where(is_fgate, _log_sigmoid(gt), gt)


def _inproj(x2, w_bf, b_row, wgt, bgt, lb_logits, tm):
    t, d = x2.shape
    grid = (t // tm,)
    row = lambda i: (i, 0)
    const = lambda i: (0, 0)
    out_shapes = (
        jax.ShapeDtypeStruct((t, HG_WIDTH), F32),
        jax.ShapeDtypeStruct((t, HG_WIDTH), BF16),
        jax.ShapeDtypeStruct((t, HG_WIDTH), BF16),
        jax.ShapeDtypeStruct((t, HG_WIDTH), BF16),
        jax.ShapeDtypeStruct((t, HG_WIDTH), BF16),
        jax.ShapeDtypeStruct((t, ML_QK_WIDTH), BF16),
        jax.ShapeDtypeStruct((t, ML_QK_WIDTH), BF16),
        jax.ShapeDtypeStruct((t, ML_WIDTH), BF16),
        jax.ShapeDtypeStruct((t, ML_WIDTH), BF16),
        jax.ShapeDtypeStruct((GATE_ROWS, t), F32),
    )
    out_specs = (
        pl.BlockSpec((tm, HG_WIDTH), row), pl.BlockSpec((tm, HG_WIDTH), row),
        pl.BlockSpec((tm, HG_WIDTH), row), pl.BlockSpec((tm, HG_WIDTH), row),
        pl.BlockSpec((tm, HG_WIDTH), row), pl.BlockSpec((tm, ML_QK_WIDTH), row),
        pl.BlockSpec((tm, ML_QK_WIDTH), row), pl.BlockSpec((tm, ML_WIDTH), row),
        pl.BlockSpec((tm, ML_WIDTH), row),
        pl.BlockSpec((GATE_ROWS, tm), lambda i: (0, i)),
    )
    in_specs = [
        pl.BlockSpec((tm, d), row),
        pl.BlockSpec((d, OFF_GATES), const, pipeline_mode=pl.Buffered(1)),
        pl.BlockSpec((1, OFF_GATES), const),
        pl.BlockSpec(wgt.shape, const),
        pl.BlockSpec(bgt.shape, const),
        pl.BlockSpec(lb_logits.shape, const),
    ]
    return pl.pallas_call(
        _inproj_kernel, grid=grid, in_specs=in_specs, out_specs=out_specs, out_shape=out_shapes,
        compiler_params=pltpu.CompilerParams(dimension_semantics=("arbitrary",),
                                             vmem_limit_bytes=VMEM_LIMIT),
        name="inproj",
    )(x2, w_bf, b_row, wgt, bgt, lb_logits)


def _chunk_tril(ts):
    idx = np.arange(ts)
    return jnp.asarray((idx[:, None] >= idx[None, :]) & (idx[:, None] // CHUNK == idx[None, :] // CHUNK), BF16)


def _hgrn2_phases(logf_ref, q_ref, k_ref, v_ref, og_ref, gn_ref, tril_ref, o_ref,
                  st_ref, b_scr, c_scr, n_chunks):
    row = lax.broadcasted_iota(jnp.int32, (CHUNK, CHUNK), 0)
    col = lax.broadcasted_iota(jnp.int32, (CHUNK, CHUNK), 1)
    exact_mask = (col <= row) & (row // EX == col // EX)
    pair_mask = row // SUB == col // SUB
    key_pos = col % EX
    env = {}

    def blocks(vals):
        return _rows([jnp.broadcast_to(jnp.asarray(x, F32), (EX, HG_WIDTH)) for x in vals])

    def setup():
        b_scr[...] = _cumsum_rows(tril_ref[...], logf_ref[...]) * LOG2E
        c_scr[...] = b_scr[...] - jnp.log2(k_ref[...].astype(F32))
        env["states"] = [st_ref[h] for h in range(HG_HEADS)]

    def scores(c):
        r0 = c * CHUNK
        rows = pl.ds(r0, CHUNK)
        b2 = b_scr[rows, :]
        c2 = c_scr[rows, :]
        qf = q_ref[rows, :].astype(F32)

        bd = [jnp.zeros((1, HG_WIDTH), F32)]
        bd += [b_scr[pl.ds(r0 + EX * m - 1, 1), :] for m in range(1, N_EX + 1)]
        per_sub = EX_PER_SUB
        sub_start = blocks([bd[(m // per_sub) * per_sub] for m in range(N_EX)])
        q_st = qf * jnp.exp2(b2 - sub_start)
        q_in = (q_st * jnp.exp2(sub_start)).astype(BF16)
        k_dec = jnp.exp2(bd[N_EX] - c2).astype(BF16)
        chunk_decay = jnp.exp2(bd[N_EX])
        q_half = qf * jnp.exp2(b2 - blocks([bd[m] if m % per_sub else jnp.inf for m in range(N_EX)]))
        k_half = jnp.exp2(blocks([-jnp.inf if m % per_sub else bd[m + 1] for m in range(N_EX)]) - c2)

        a_heads = []
        for h in range(HG_HEADS):
            hs = slice(h * HG_DK, (h + 1) * HG_DK)
            exact = jnp.zeros((CHUNK, CHUNK), F32)
            for j in range(EX):
                c_j = _rows([jnp.broadcast_to(c_scr[pl.ds(r0 + m * EX + j, 1), hs], (EX, HG_DK))
                            for m in range(N_EX)])
                prod = qf[:, hs] * jnp.exp2(b2[:, hs] - c_j)
                exact = jnp.where(key_pos == j, jnp.sum(prod, axis=-1, keepdims=True), exact)
            half = _dot_nt(q_half[:, hs].astype(BF16), k_half[:, hs].astype(BF16))
            off = [jnp.zeros((SUB, CHUNK), F32)]
            for i in range(1, N_SUB):
                k_i = jnp.exp2(bd[i * per_sub][:, hs] - c2[0:i * SUB, hs])
                k_i = _rows([k_i, jnp.zeros((CHUNK - i * SUB, HG_DK), F32)])
                off.append(_dot_nt(q_st[i * SUB:(i + 1) * SUB, hs].astype(BF16), k_i.astype(BF16)))
            a = jnp.where(exact_mask, exact, jnp.where(pair_mask, half, 0.0)) + _rows(off)
            a_heads.append(a.astype(BF16))
        env[c] = (a_heads, q_in, k_dec, chunk_decay)

    def outputs(c):
        a_heads, q_in, k_dec, chunk_decay = env.pop(c)
        states = env["states"]
        rows = pl.ds(c * CHUNK, CHUNK)
        v = v_ref[rows, :]
        outs = []
        for h in range(HG_HEADS):
            hs = slice(h * HG_DK, (h + 1) * HG_DK)
            st = states[h]
            o_h = _dot(a_heads[h], v[:, hs]) + _dot_nt(q_in[:, hs], st.astype(BF16))
            states[h] = st * chunk_decay[:, hs] + _dot_tn(v[:, hs], k_dec[:, hs])
            outs.append(_head_rms(o_h, HG_DV))
        o = jnp.concatenate(outs, axis=-1)
        o_ref[rows, :] = (o * gn_ref[...] * og_ref[rows, :].astype(F32)).astype(o_ref.dtype)

    def finish():
        for h in range(HG_HEADS):
            st_ref[h] = env["states"][h]

    phases = [setup, functools.partial(scores, 0)]
    for c in range(n_chunks):
        step = [functools.partial(scores, c + 1)] if c + 1 < n_chunks else []
        step.append(functools.partial(outputs, c))
        phases.append(lambda step=step: [f() for f in step])
    phases.append(finish)
    return phases


CONV_PAD = 8
SEL_ROWS = 128
GRP_M, GRP_WI, GRP_EN = 0, 32, 64
ML_AUG = 2 * ML_DV


def _mlstm_constants(tl):
    ident = np.eye(tl, dtype=np.float32)
    triu = np.triu(np.ones((tl, tl), np.float32))
    bias = np.where(np.tril(np.ones((tl, tl), bool)), 0.0, -np.inf).astype(np.float32)
    sel_sum = np.zeros((SEL_ROWS, LANE), np.float32)
    for q, grp in enumerate((GRP_M, GRP_WI, GRP_EN)):
        for h in range(ML_HEADS):
            for k in range(3):
                sel_sum[grp + GATE_TILE * k + h, q * GATE_TILE + h] = 1.0
    return (jnp.asarray(ident, BF16), jnp.asarray(triu, BF16), jnp.asarray(bias), jnp.asarray(sel_sum, BF16))


def _mlstm_phases(mq_ref, mk_ref, mv_ref, og_ref, grow_ref, cw_ref, cb_ref, gn_ref,
                  ident_ref, triu_ref, bias_ref, selsum_ref, o_ref,
                  cbuf, c_ref, m_ref, tl):
    env = {}

    def scalars():
        cbuf[CONV_PAD:CONV_PAD + tl, 0:ML_QK_WIDTH] = mq_ref[...].astype(F32)
        cbuf[CONV_PAD:CONV_PAD + tl, ML_QK_WIDTH:] = mk_ref[...].astype(F32)
        acc = cb_ref[...] + cw_ref[0:1, :] * cbuf[pl.ds(CONV_PAD - (CONV_K - 1), tl), :]
        for tap in range(1, CONV_K):
            acc = acc + cw_ref[tap:tap + 1, :] * cbuf[pl.ds(CONV_PAD - (CONV_K - 1) + tap, tl), :]
        qk = _silu(acc)
        cbuf[0:CONV_PAD, :] = cbuf[tl:tl + CONV_PAD, :]
        env["q"] = (qk[:, 0:ML_QK_WIDTH] * (ML_DQK ** -0.5)).astype(BF16)
        k = qk[:, ML_QK_WIDTH:]
        env["k"] = k.astype(BF16)
        env["k_t"] = k.T

        gates = grow_ref[...]
        i_g = gates[0:GATE_TILE, :]
        hi, mid, lo = _split3(gates[GATE_TILE:, :])
        triu = triu_ref[...]
        part = _dot(_rows([hi, mid]), triu)
        g = part[0:GATE_TILE, :] + part[GATE_TILE:, :] + _dot(_rows([lo, lo]), triu)[0:GATE_TILE, :]
        u = i_g - g
        lane = lax.broadcasted_iota(jnp.int32, (GATE_TILE, tl), 1)
        cm = u
        shift = 1
        while shift < tl:
            cm = jnp.maximum(cm, jnp.where(lane >= shift, pltpu.roll(cm, shift, axis=1), -jnp.inf))
            shift *= 2
        m_prev = m_ref[...]
        m_run = jnp.maximum(m_prev, cm)
        w_inter = jnp.exp(m_prev - m_run)
        e_negm = jnp.exp(-(g + m_run))
        m_last = jnp.broadcast_to(m_run[:, tl - 1:tl], (GATE_TILE, tl))
        g_last = jnp.broadcast_to(g[:, tl - 1:tl], (GATE_TILE, tl))
        env["w_s"] = jnp.exp(u - m_last)
        env["w_old"] = jnp.exp(m_prev - m_last)
        m_ref[...] = g_last + m_last

        env["u"] = u
        zeros8 = jnp.zeros((GATE_TILE, tl), F32)
        env["pack"] = _rows(_split3_f32(m_run) + [zeros8] + _split3_f32(w_inter) + [zeros8]
                            + _split3_f32(e_negm) + [zeros8] * 5).astype(BF16)

    def copies():
        pack_t = _dot_nt(ident_ref[...], env["pack"]).astype(BF16)
        env["cols"] = _dot(pack_t, selsum_ref[...])

    def head(h):
        qs = slice(h * ML_DQK, (h + 1) * ML_DQK)
        vs = slice(h * ML_DV, (h + 1) * ML_DV)
        q_bf, k_bf, cols = env["q"], env["k"], env["cols"]
        m_col = cols[:, h:h + 1]
        wi_rep = jnp.broadcast_to(cols[:, GATE_TILE + h:GATE_TILE + h + 1], (tl, ML_DV))
        en_rep = jnp.broadcast_to(cols[:, 2 * GATE_TILE + h:2 * GATE_TILE + h + 1], (tl, ML_DV))
        w_intra = jnp.exp((env["u"][h:h + 1, :] - m_col) + bias_ref[...])
        p = (_dot_nt(q_bf[:, qs], k_bf[:, qs]) * w_intra).astype(BF16)
        v_aug = jnp.concatenate([mv_ref[:, vs], jnp.ones((tl, ML_DV), BF16)], axis=1)
        c_aug = c_ref[h]
        intra = _dot(p, v_aug)
        inter = _dot(q_bf[:, qs], c_aug.astype(BF16))
        num = intra[:, 0:ML_DV] + wi_rep * inter[:, 0:ML_DV]
        den = intra[:, ML_DV:] + wi_rep * inter[:, ML_DV:]
        hh = num / jnp.maximum(jnp.abs(den), en_rep)
        kw_t = (env["k_t"][qs, :] * env["w_s"][h:h + 1, :]).astype(BF16)
        c_ref[h] = env["w_old"][h:h + 1, 0:ML_AUG] * c_aug + _dot(kw_t, v_aug)
        o_ref[:, vs] = (_head_rms(hh, ML_DV) * gn_ref[:, vs] * og_ref[:, vs].astype(F32)).astype(o_ref.dtype)

    return [scalars, copies] + [functools.partial(head, h) for h in range(ML_HEADS)]


FF_TILE = 256


def _layer_norm(x, g, b):
    mu = jnp.mean(x, axis=-1, keepdims=True)
    xc = x - mu
    var = jnp.mean(xc * xc, axis=-1, keepdims=True)
    return xc * lax.rsqrt(var + LN_EPS) * g + b


def _norm_phases(ohg_ref, oml_ref, x_ref, wo_ref, ln1g_ref, ln1b_ref, h1_ref, alpha):
    env = {}

    def project():
        env["mix"] = _dot(ohg_ref[...], wo_ref[0:HG_WIDTH, :]) + _dot(oml_ref[...], wo_ref[HG_WIDTH:, :])

    def norm():
        h1_ref[...] = _layer_norm(alpha * x_ref[...] + env["mix"], ln1g_ref[...], ln1b_ref[...])

    return project, norm


def _ffn_phases(h1_ref, p_ref, wg_ref, wu_ref, wd_ref, ln2g_ref, ln2b_ref, wpp_ref, wpg_ref, bpg_ref,
                out_ref, act_scr, ffn_scr, alpha, d_ff):
    env = {}
    d = h1_ref.shape[-1]

    def hidden(lo):
        if lo == 0:
            env["h1b"] = h1_ref[...].astype(BF16)
        h1b = env["h1b"]
        gate = _dot(h1b, wg_ref[:, lo:lo + FF_TILE])
        up = _dot(h1b, wu_ref[:, lo:lo + FF_TILE])
        act_scr[:, lo:lo + FF_TILE] = (_silu(gate) * up).astype(BF16)

    def down(lo):
        ffn_scr[:, lo:lo + FF_TILE] = _dot(act_scr[...], wd_ref[:, lo:lo + FF_TILE])

    def norm():
        env["pemb"] = _dot(p_ref[...].astype(BF16), wpp_ref[...])
        env["h2"] = _layer_norm(alpha * h1_ref[...] + ffn_scr[...], ln2g_ref[...], ln2b_ref[...])

    def embed():
        h2 = env["h2"]
        pgate = _sigmoid(_dot(h2.astype(BF16), wpg_ref[...]) + bpg_ref[...])
        out_ref[...] = h2 + pgate * env["pemb"]

    tiles = ([functools.partial(hidden, lo) for lo in range(0, d_ff, FF_TILE)]
             + [functools.partial(down, lo) for lo in range(0, d, FF_TILE)])
    return tiles, norm, embed


N_MIX_IN = 10
N_MIX_CONST = 9
N_TAIL_CONST = 11


def _mixer_groups(hg, ml):
    hg_setup, *hg_chunks, hg_finish = hg
    ml_scalars, ml_copies, *ml_heads = ml
    groups = [[hg_setup, ml_scalars], hg_chunks[:1], [ml_copies] + hg_chunks[1:2]]
    rest_hg, rest_ml = hg_chunks[2:], ml_heads
    for j in range(max(len(rest_hg), len(rest_ml))):
        groups.append(rest_ml[j:j + 1] + rest_hg[j:j + 1])
    groups[-1].append(hg_finish)
    return groups


def _issue_order(ffn, groups):
    order, done = [], 0
    for j, f in enumerate(ffn):
        order.append(f)
        upto = round((j + 1) / len(ffn) * len(groups))
        for group in groups[done:upto]:
            order.extend(group)
        done = max(done, upto)
    return order


def _fused_kernel(*refs, tl, sub, tiles_per_seq, alpha, d_ff):
    it = iter(refs)
    take = lambda n: [next(it) for _ in range(n)]
    (logf_ref, hq_ref, hk_ref, hv_ref, hg_ref, mq_ref, mk_ref, mv_ref, mo_ref, grow_ref) = take(N_MIX_IN)
    (hgn_ref, tril_ref, cw_ref, cb_ref, mgn_ref,
     ident_ref, triu_ref, bias_ref, selsum_ref) = take(N_MIX_CONST)
    x_ref, p_ref = take(2)
    (wo_ref, ln1g_ref, ln1b_ref, wg_ref, wu_ref, wd_ref,
     ln2g_ref, ln2b_ref, wpp_ref, wpg_ref, bpg_ref) = take(N_TAIL_CONST)
    (out_ref,) = take(1)
    st_ref, b_scr, c_scr, cbuf, c_ref, m_ref, ohg_scr, oml_scr, h1_scr, act_scr, ffn_scr = take(11)

    g = pl.program_id(0)

    @pl.when(g == 0)
    def _():
        h1_scr[...] = jnp.zeros_like(h1_scr)

    @pl.when(g % tiles_per_seq == 0)
    def _():
        st_ref[...] = jnp.zeros_like(st_ref)
        cbuf[0:CONV_PAD, :] = jnp.zeros((CONV_PAD, 2 * ML_QK_WIDTH), F32)
        c_ref[...] = jnp.zeros_like(c_ref)
        m_ref[...] = jnp.zeros_like(m_ref)

    cur = g % 2
    prev = 1 - cur
    groups = []
    for off in range(0, tl, sub):
        rows = pl.ds(off, sub)
        at = lambda ref: ref.at[rows]
        hg_phases = _hgrn2_phases(at(logf_ref), at(hq_ref), at(hk_ref), at(hv_ref), at(hg_ref), hgn_ref,
                                  tril_ref, ohg_scr.at[rows], st_ref, b_scr, c_scr, sub // CHUNK)
        ml_phases = _mlstm_phases(at(mq_ref), at(mk_ref), at(mv_ref), at(mo_ref), grow_ref.at[:, rows],
                                  cw_ref, cb_ref, mgn_ref, ident_ref, triu_ref, bias_ref, selsum_ref,
                                  oml_scr.at[rows], cbuf, c_ref, m_ref, sub)
        groups += _mixer_groups(hg_phases, ml_phases)
    project, norm1 = _norm_phases(ohg_scr, oml_scr, x_ref, wo_ref, ln1g_ref, ln1b_ref, h1_scr.at[cur], alpha)
    ffn, norm2, embed = _ffn_phases(h1_scr.at[prev], p_ref, wg_ref, wu_ref, wd_ref, ln2g_ref, ln2b_ref,
                                    wpp_ref, wpg_ref, bpg_ref, out_ref, act_scr, ffn_scr, alpha, d_ff)
    for f in _issue_order(ffn, groups) + [project, norm2, embed, norm1]:
        f()


def _fused(mix_in, hgn, conv_w, conv_b, mgn, x2, p2, tail_consts, batch, seq, alpha):
    t, d = x2.shape
    tl, sub = TAIL_TILE, MIX_TILE
    assert seq % tl == 0 and tl % sub == 0 and sub % CHUNK == 0 and sub >= ML_AUG
    tiles_per_seq = seq // tl
    n_tiles = t // tl
    d_ff = tail_consts[3].shape[1]
    assert d_ff % FF_TILE == 0
    mix_consts = (hgn, _chunk_tril(sub), conv_w, conv_b, mgn) + _mlstm_constants(sub)
    assert len(mix_in) == N_MIX_IN and len(mix_consts) == N_MIX_CONST and len(tail_consts) == N_TAIL_CONST

    def tile(lag):
        return lambda g: jnp.clip(g - lag, 0, n_tiles - 1)

    def const(a):
        return pl.BlockSpec(a.shape, lambda g: (0,) * a.ndim, pipeline_mode=pl.Buffered(1))

    mix_specs = [pl.BlockSpec((tl, a.shape[1]), lambda g: (tile(0)(g), 0)) for a in mix_in[:-1]]
    mix_specs.append(pl.BlockSpec((GATE_ROWS, tl), lambda g: (0, tile(0)(g))))
    in_specs = (mix_specs + [const(a) for a in mix_consts]
                + [pl.BlockSpec((tl, d), lambda g: (tile(0)(g), 0)),
                   pl.BlockSpec((tl, p2.shape[1]), lambda g: (tile(1)(g), 0))]
                + [const(a) for a in tail_consts])
    return pl.pallas_call(
        functools.partial(_fused_kernel, tl=tl, sub=sub, tiles_per_seq=tiles_per_seq, alpha=alpha, d_ff=d_ff),
        grid=(n_tiles + 1,),
        in_specs=in_specs,
        out_specs=pl.BlockSpec((tl, d), lambda g: (tile(1)(g), 0)),
        out_shape=jax.ShapeDtypeStruct((t, d), F32),
        scratch_shapes=[pltpu.VMEM((HG_HEADS, HG_DV, HG_DK), F32),
                        pltpu.VMEM((sub, HG_WIDTH), F32),
                        pltpu.VMEM((sub, HG_WIDTH), F32),
                        pltpu.VMEM((sub + CONV_PAD, 2 * ML_QK_WIDTH), F32),
                        pltpu.VMEM((ML_HEADS, ML_DQK, ML_AUG), F32),
                        pltpu.VMEM((GATE_TILE, sub), F32),
                        pltpu.VMEM((tl, HG_WIDTH), BF16),
                        pltpu.VMEM((tl, ML_WIDTH), BF16),
                        pltpu.VMEM((2, tl, d), F32),
                        pltpu.VMEM((tl, d_ff), BF16),
                        pltpu.VMEM((tl, d), F32)],
        compiler_params=pltpu.CompilerParams(dimension_semantics=("arbitrary",),
                                             vmem_limit_bytes=VMEM_LIMIT),
        name="mix_tail",
    )(*mix_in, *mix_consts, x2, p2, *tail_consts)


def kernel(x, p, w_in, b_in, hg_lb_logits, ml_conv_w, ml_conv_b, hg_norm_g, ml_norm_g, w_out, ln1_g, ln1_b,
           w_ffn_gate, w_ffn_up, w_ffn_down, ln2_g, ln2_b, ple_w_proj, ple_w_gate, ple_b_gate):
    batch, seq, d = x.shape
    depth = w_in.shape[0]
    t = batch * seq
    alpha = float((2 * depth) ** 0.25)
    tm = INPROJ_TILE
    assert t % tm == 0
    assert w_in.shape[2] == OFF_GATES + 2 * ML_HEADS
    assert depth == 1, "lower-bound cumsum is specialised to a single layer"

    x2 = x.reshape(t, d)
    for i in range(depth):
        w_i = w_in[i]
        w_bf = w_i.astype(BF16)
        b_row = b_in[i].reshape(1, -1)
        wg_t = w_i[:, OFF_GATES:].T
        gate_pad = ((0, GATE_TILE - ML_HEADS), (0, 0))
        wgt = jnp.concatenate([jnp.pad(wg_t[:ML_HEADS], gate_pad), jnp.pad(wg_t[ML_HEADS:], gate_pad)]).astype(BF16)
        bg = b_in[i, OFF_GATES:].reshape(2 * ML_HEADS, 1)
        bgt = jnp.concatenate([jnp.pad(bg[:ML_HEADS], gate_pad), jnp.pad(bg[ML_HEADS:], gate_pad)])

        mix_in = _inproj(x2, w_bf, b_row, wgt, bgt, hg_lb_logits, tm)
        tail_consts = (w_out[i].astype(BF16), ln1_g[i].reshape(1, d), ln1_b[i].reshape(1, d),
                       w_ffn_gate[i].astype(BF16), w_ffn_up[i].astype(BF16), w_ffn_down[i].astype(BF16),
                       ln2_g[i].reshape(1, d), ln2_b[i].reshape(1, d),
                       ple_w_proj[i].astype(BF16), ple_w_gate[i].astype(BF16), ple_b_gate[i].reshape(1, d))
        x2 = _fused(mix_in, hg_norm_g[i].reshape(1, HG_WIDTH), ml_conv_w[i], ml_conv_b[i].reshape(1, -1),
                    ml_norm_g[i].reshape(1, ML_WIDTH), x2, p[i].reshape(t, -1), tail_consts, batch, seq, alpha)
    return x2.reshape(batch, seq, d)
```

```python
import functools

import numpy as np
import jax
import jax.numpy as jnp
from jax import lax
from jax.experimental import pallas as pl
from jax.experimental.pallas import tpu as pltpu

F32 = jnp.float32
BF16 = jnp.bfloat16

CHUNK = 64
SUB = 16
N_SUB = CHUNK // SUB
EX = 8
N_EX = CHUNK // EX
EX_PER_SUB = SUB // EX
LOG2E = 1.4426950408889634
HG_HEADS = 4
HG_DK = 128
HG_DV = 128
HG_WIDTH = HG_HEADS * HG_DV
ML_HEADS = 4
ML_DQK = 64
ML_DV = 128
ML_WIDTH = ML_HEADS * ML_DV
ML_QK_WIDTH = ML_HEADS * ML_DQK
CONV_K = 4
LN_EPS = 1e-5
RMS_EPS = 1e-6

OFF_HQ = 0
OFF_HF = OFF_HQ + HG_HEADS * HG_DK
OFF_HV = OFF_HF + HG_HEADS * HG_DK
OFF_HG = OFF_HV + HG_WIDTH
OFF_MQ = OFF_HG + HG_WIDTH
OFF_MK = OFF_MQ + ML_QK_WIDTH
OFF_MV = OFF_MK + ML_QK_WIDTH
OFF_MO = OFF_MV + ML_WIDTH
OFF_GATES = OFF_MO + ML_WIDTH
LANE = 128
GATE_TILE = 8
BF16_ROWS = 16
GATE_ROWS = 2 * GATE_TILE
PROJ_PIECE = 256
N_MIX_IN = 10

VMEM_LIMIT = 60 * 1024 * 1024
MIX_TILE = 256
TAIL_TILE = 512
INPROJ_TILE = 1024


def _sigmoid(x):
    return 0.5 * jnp.tanh(0.5 * x) + 0.5


def _silu(x):
    return x * _sigmoid(x)


def _log_sigmoid(x):
    return jnp.minimum(x, 0.0) - jnp.log(1.0 + jnp.exp(-jnp.abs(x)))


def _split3(x):
    hi = x.astype(BF16)
    r1 = x - hi.astype(F32)
    mid = r1.astype(BF16)
    lo = (r1 - mid.astype(F32)).astype(BF16)
    return hi, mid, lo


def _split3_f32(x):
    return [s.astype(F32) for s in _split3(x)]


def _dot(a, b):
    return jnp.dot(a, b, preferred_element_type=F32)


def _dot_nt(a, b):
    return lax.dot_general(a, b, (((1,), (1,)), ((), ())), preferred_element_type=F32)


def _dot_tn(a, b):
    return lax.dot_general(a, b, (((0,), (0,)), ((), ())), preferred_element_type=F32)


def _rows(blocks):
    return jnp.concatenate(blocks, axis=0)


def _cumsum_rows(tril_bf, x):
    hi, mid, lo = _split3(x)
    return _dot(tril_bf, hi) + _dot(tril_bf, mid) + _dot(tril_bf, lo)


def _head_rms(o, width):
    ms = jnp.sum(o * o, axis=-1, keepdims=True) * (1.0 / width)
    return o * lax.rsqrt(ms + RMS_EPS)


def _inproj_kernel(x_ref, w_ref, b_ref, wgt_ref, bgt_ref, lbl_ref, *refs, n_cast):
    cast_src, refs = refs[:n_cast], refs[n_cast:]
    (logf_ref, hq_ref, hk_ref, hv_ref, hg_ref, mq_ref, mk_ref, mv_ref, mo_ref, grow_ref) = refs[:N_MIX_IN]
    for src, dst in zip(cast_src, refs[N_MIX_IN:]):
        dst[...] = src[...].astype(BF16)

    xb = x_ref[...].astype(BF16)

    logits = lbl_ref[...]
    mx = jnp.max(logits, axis=0, keepdims=True)
    ex = jnp.exp(logits - mx)
    den = jnp.sum(ex, axis=0, keepdims=True)
    lb = ex[0:1, :] / den
    one_m_lb = (den - ex[0:1, :]) / den

    def plain(out_ref):
        def store(u, cols):
            out_ref[:, cols] = u.astype(BF16)
        return store

    def act(out_ref, fn):
        def store(u, cols):
            out_ref[:, cols] = fn(u).astype(BF16)
        return store

    def forget(u, cols):
        sig = _sigmoid(u)
        logf_ref[:, cols] = jnp.log(lb[:, cols] + one_m_lb[:, cols] * sig)
        hk_ref[:, cols] = (one_m_lb[:, cols] * (1.0 - sig)).astype(BF16)

    n = PROJ_PIECE
    pieces = {"hq": (OFF_HQ, HG_WIDTH, act(hq_ref, _silu)), "hf": (OFF_HF, HG_WIDTH, forget),
              "hv": (OFF_HV, HG_WIDTH, plain(hv_ref)), "hg": (OFF_HG, HG_WIDTH, act(hg_ref, _silu)),
              "mq": (OFF_MQ, ML_QK_WIDTH, plain(mq_ref)), "mk": (OFF_MK, ML_QK_WIDTH, plain(mk_ref)),
              "mv": (OFF_MV, ML_WIDTH, plain(mv_ref)), "mo": (OFF_MO, ML_WIDTH, act(mo_ref, _sigmoid))}
    order = ["hq", "hv", "hf", "mq", "hg", "hv", "mo", "mk", "hq", "mv", "hf", "mv", "hg", "mo"]
    taken = {name: 0 for name in pieces}
    for name in order:
        off, width, store = pieces[name]
        lo = taken[name]
        taken[name] = lo + n
        store(_dot(xb, w_ref[:, off + lo:off + lo + n]) + b_ref[:, off + lo:off + lo + n], slice(lo, lo + n))
    assert all(taken[name] == pieces[name][1] for name in pieces)

    gt = _dot_nt(wgt_ref[...], xb) + bgt_ref[...]
    sub = lax.broadcasted_iota(jnp.int32, gt.shape, 0)
    is_fgate = (sub >= GATE_TILE) & (sub < GATE_TILE + ML_HEADS)
    grow_ref[...] = jnp.where(is_fgate, _log_sigmoid(gt), gt)


def _cast_blocks(rows, steps):
    per = 1
    while steps % per or rows % (steps // per) or (rows // (steps // per)) % BF16_ROWS:
        per += 1
        assert per <= steps, (rows, steps)
    return rows // (steps // per), per


def _inproj(x2, w_bf, b_row, wgt, bgt, lb_logits, to_cast, tm):
    t, d = x2.shape
    grid = (t // tm,)
    row = lambda i: (i, 0)
    const = lambda i: (0, 0)
    cast_specs = []
    for a in to_cast:
        block_rows, per = _cast_blocks(a.shape[0], grid[0])
        cast_specs.append(pl.BlockSpec((block_rows, a.shape[1]), lambda i, per=per: (i // per, 0)))
    out_shapes = (
        jax.ShapeDtypeStruct((t, HG_WIDTH), F32),
        jax.ShapeDtypeStruct((t, HG_WIDTH), BF16),
        jax.ShapeDtypeStruct((t, HG_WIDTH), BF16),
        jax.ShapeDtypeStruct((t, HG_WIDTH), BF16),
        jax.ShapeDtypeStruct((t, HG_WIDTH), BF16),
        jax.ShapeDtypeStruct((t, ML_QK_WIDTH), BF16),
        jax.ShapeDtypeStruct((t, ML_QK_WIDTH), BF16),
        jax.ShapeDtypeStruct((t, ML_WIDTH), BF16),
        jax.ShapeDtypeStruct((t, ML_WIDTH), BF16),
        jax.ShapeDtypeStruct((GATE_ROWS, t), F32),
    ) + tuple(jax.ShapeDtypeStruct(a.shape, BF16) for a in to_cast)
    out_specs = (
        pl.BlockSpec((tm, HG_WIDTH), row), pl.BlockSpec((tm, HG_WIDTH), row),
        pl.BlockSpec((tm, HG_WIDTH), row), pl.BlockSpec((tm, HG_WIDTH), row),
        pl.BlockSpec((tm, HG_WIDTH), row), pl.BlockSpec((tm, ML_QK_WIDTH), row),
        pl.BlockSpec((tm, ML_QK_WIDTH), row), pl.BlockSpec((tm, ML_WIDTH), row),
        pl.BlockSpec((tm, ML_WIDTH), row),
        pl.BlockSpec((GATE_ROWS, tm), lambda i: (0, i)),
    ) + tuple(cast_specs)
    in_specs = [
        pl.BlockSpec((tm, d), row),
        pl.BlockSpec((d, OFF_GATES), const, pipeline_mode=pl.Buffered(1)),
        pl.BlockSpec((1, OFF_GATES), const),
        pl.BlockSpec(wgt.shape, const),
        pl.BlockSpec(bgt.shape, const),
        pl.BlockSpec(lb_logits.shape, const),
    ] + cast_specs
    outs = pl.pallas_call(
        functools.partial(_inproj_kernel, n_cast=len(to_cast)),
        grid=grid, in_specs=in_specs, out_specs=out_specs, out_shape=out_shapes,
        compiler_params=pltpu.CompilerParams(dimension_semantics=("arbitrary",),
                                             vmem_limit_bytes=VMEM_LIMIT),
        name="inproj",
    )(x2, w_bf, b_row, wgt, bgt, lb_logits, *to_cast)
    return outs[:N_MIX_IN], outs[N_MIX_IN:]


def _chunk_tril(ts):
    idx = np.arange(ts)
    return jnp.asarray((idx[:, None] >= idx[None, :]) & (idx[:, None] // CHUNK == idx[None, :] // CHUNK), BF16)


def _hgrn2_phases(logf_ref, q_ref, k_ref, v_ref, og_ref, gn_ref, tril_ref, o_ref,
                  st_ref, b_scr, c_scr, n_chunks):
    row = lax.broadcasted_iota(jnp.int32, (CHUNK, CHUNK), 0)
    col = lax.broadcasted_iota(jnp.int32, (CHUNK, CHUNK), 1)
    exact_mask = (col <= row) & (row // EX == col // EX)
    pair_mask = row // SUB == col // SUB
    key_pos = col % EX
    env = {}

    def blocks(vals):
        return _rows([jnp.broadcast_to(jnp.asarray(x, F32), (EX, HG_WIDTH)) for x in vals])

    def setup():
        b_scr[...] = _cumsum_rows(tril_ref[...], logf_ref[...]) * LOG2E
        c_scr[...] = b_scr[...] - jnp.log2(k_ref[...].astype(F32))
        env["states"] = [st_ref[h] for h in range(HG_HEADS)]

    def scores(c):
        r0 = c * CHUNK
        rows = pl.ds(r0, CHUNK)
        b2 = b_scr[rows, :]
        c2 = c_scr[rows, :]
        qf = q_ref[rows, :].astype(F32)

        bd = [jnp.zeros((1, HG_WIDTH), F32)]
        bd += [b_scr[pl.ds(r0 + EX * m - 1, 1), :] for m in range(1, N_EX + 1)]
        per_sub = EX_PER_SUB
        sub_start = blocks([bd[(m // per_sub) * per_sub] for m in range(N_EX)])
        q_st = qf * jnp.exp2(b2 - sub_start)
        q_in = (q_st * jnp.exp2(sub_start)).astype(BF16)
        k_dec = jnp.exp2(bd[N_EX] - c2).astype(BF16)
        chunk_decay = jnp.exp2(bd[N_EX])
        q_half = qf * jnp.exp2(b2 - blocks([bd[m] if m % per_sub else jnp.inf for m in range(N_EX)]))
        k_half = jnp.exp2(blocks([-jnp.inf if m % per_sub else bd[m + 1] for m in range(N_EX)]) - c2)

        a_heads = []
        for h in range(HG_HEADS):
            hs = slice(h * HG_DK, (h + 1) * HG_DK)
            exact = jnp.zeros((CHUNK, CHUNK), F32)
            for j in range(EX):
                c_j = _rows([jnp.broadcast_to(c_scr[pl.ds(r0 + m * EX + j, 1), hs], (EX, HG_DK))
                            for m in range(N_EX)])
                prod = qf[:, hs] * jnp.exp2(b2[:, hs] - c_j)
                exact = jnp.where(key_pos == j, jnp.sum(prod, axis=-1, keepdims=True), exact)
            half = _dot_nt(q_half[:, hs].astype(BF16), k_half[:, hs].astype(BF16))
            off = [jnp.zeros((SUB, CHUNK), F32)]
            for i in range(1, N_SUB):
                k_i = jnp.exp2(bd[i * per_sub][:, hs] - c2[0:i * SUB, hs])
                k_i = _rows([k_i, jnp.zeros((CHUNK - i * SUB, HG_DK), F32)])
                off.append(_dot_nt(q_st[i * SUB:(i + 1) * SUB, hs].astype(BF16), k_i.astype(BF16)))
            a = jnp.where(exact_mask, exact, jnp.where(pair_mask, half, 0.0)) + _rows(off)
            a_heads.append(a.astype(BF16))
        env[c] = (a_heads, q_in, k_dec, chunk_decay)

    def outputs(c):
        a_heads, q_in, k_dec, chunk_decay = env.pop(c)
        states = env["states"]
        rows = pl.ds(c * CHUNK, CHUNK)
        v = v_ref[rows, :]
        outs = []
        for h in range(HG_HEADS):
            hs = slice(h * HG_DK, (h + 1) * HG_DK)
            st = states[h]
            o_h = _dot(a_heads[h], v[:, hs]) + _dot_nt(q_in[:, hs], st.astype(BF16))
            states[h] = st * chunk_decay[:, hs] + _dot_tn(v[:, hs], k_dec[:, hs])
            outs.append(_head_rms(o_h, HG_DV))
        o = jnp.concatenate(outs, axis=-1)
        o_ref[rows, :] = (o * gn_ref[...] * og_ref[rows, :].astype(F32)).astype(o_ref.dtype)

    def finish():
        for h in range(HG_HEADS):
            st_ref[h] = env["states"][h]

    phases = [setup, functools.partial(scores, 0)]
    for c in range(n_chunks):
        step = [functools.partial(scores, c + 1)] if c + 1 < n_chunks else []
        step.append(functools.partial(outputs, c))
        phases.append(lambda step=step: [f() for f in step])
    phases.append(finish)
    return phases


CONV_PAD = 8
SEL_ROWS = 128
GRP_M, GRP_WI, GRP_EN = 0, 32, 64
ML_AUG = 2 * ML_DV


def _mlstm_constants(tl):
    ident = np.eye(tl, dtype=np.float32)
    triu = np.triu(np.ones((tl, tl), np.float32))
    bias = np.where(np.tril(np.ones((tl, tl), bool)), 0.0, -np.inf).astype(np.float32)
    sel_sum = np.zeros((SEL_ROWS, LANE), np.float32)
    for q, grp in enumerate((GRP_M, GRP_WI, GRP_EN)):
        for h in range(ML_HEADS):
            for k in range(3):
                sel_sum[grp + GATE_TILE * k + h, q * GATE_TILE + h] = 1.0
    return (jnp.asarray(ident, BF16), jnp.asarray(triu, BF16), jnp.asarray(bias), jnp.asarray(sel_sum, BF16))


def _mlstm_phases(mq_ref, mk_ref, mv_ref, og_ref, grow_ref, cw_ref, cb_ref, gn_ref,
                  ident_ref, triu_ref, bias_ref, selsum_ref, o_ref,
                  cbuf, c_ref, m_ref, tl):
    env = {}

    def scalars():
        cbuf[CONV_PAD:CONV_PAD + tl, 0:ML_QK_WIDTH] = mq_ref[...].astype(F32)
        cbuf[CONV_PAD:CONV_PAD + tl, ML_QK_WIDTH:] = mk_ref[...].astype(F32)
        acc = cb_ref[...] + cw_ref[0:1, :] * cbuf[pl.ds(CONV_PAD - (CONV_K - 1), tl), :]
        for tap in range(1, CONV_K):
            acc = acc + cw_ref[tap:tap + 1, :] * cbuf[pl.ds(CONV_PAD - (CONV_K - 1) + tap, tl), :]
        qk = _silu(acc)
        cbuf[0:CONV_PAD, :] = cbuf[tl:tl + CONV_PAD, :]
        env["q"] = (qk[:, 0:ML_QK_WIDTH] * (ML_DQK ** -0.5)).astype(BF16)
        k = qk[:, ML_QK_WIDTH:]
        env["k"] = k.astype(BF16)
        env["k_t"] = k.T

        gates = grow_ref[...]
        i_g = gates[0:GATE_TILE, :]
        hi, mid, lo = _split3(gates[GATE_TILE:, :])
        triu = triu_ref[...]
        part = _dot(_rows([hi, mid]), triu)
        g = part[0:GATE_TILE, :] + part[GATE_TILE:, :] + _dot(_rows([lo, lo]), triu)[0:GATE_TILE, :]
        u = i_g - g
        lane = lax.broadcasted_iota(jnp.int32, (GATE_TILE, tl), 1)
        cm = u
        shift = 1
        while shift < tl:
            cm = jnp.maximum(cm, jnp.where(lane >= shift, pltpu.roll(cm, shift, axis=1), -jnp.inf))
            shift *= 2
        m_prev = m_ref[...]
        m_run = jnp.maximum(m_prev, cm)
        w_inter = jnp.exp(m_prev - m_run)
        e_negm = jnp.exp(-(g + m_run))
        m_last = jnp.broadcast_to(m_run[:, tl - 1:tl], (GATE_TILE, tl))
        g_last = jnp.broadcast_to(g[:, tl - 1:tl], (GATE_TILE, tl))
        env["w_s"] = jnp.exp(u - m_last)
        env["w_old"] = jnp.exp(m_prev - m_last)
        m_ref[...] = g_last + m_last

        env["u"] = u
        zeros8 = jnp.zeros((GATE_TILE, tl), F32)
        env["pack"] = _rows(_split3_f32(m_run) + [zeros8] + _split3_f32(w_inter) + [zeros8]
                            + _split3_f32(e_negm) + [zeros8] * 5).astype(BF16)

    def copies():
        pack_t = _dot_nt(ident_ref[...], env["pack"]).astype(BF16)
        env["cols"] = _dot(pack_t, selsum_ref[...])

    def head(h):
        qs = slice(h * ML_DQK, (h + 1) * ML_DQK)
        vs = slice(h * ML_DV, (h + 1) * ML_DV)
        q_bf, k_bf, cols = env["q"], env["k"], env["cols"]
        m_col = cols[:, h:h + 1]
        wi_rep = jnp.broadcast_to(cols[:, GATE_TILE + h:GATE_TILE + h + 1], (tl, ML_DV))
        en_rep = jnp.broadcast_to(cols[:, 2 * GATE_TILE + h:2 * GATE_TILE + h + 1], (tl, ML_DV))
        w_intra = jnp.exp((env["u"][h:h + 1, :] - m_col) + bias_ref[...])
        p = (_dot_nt(q_bf[:, qs], k_bf[:, qs]) * w_intra).astype(BF16)
        v_aug = jnp.concatenate([mv_ref[:, vs], jnp.ones((tl, ML_DV), BF16)], axis=1)
        c_aug = c_ref[h]
        intra = _dot(p, v_aug)
        inter = _dot(q_bf[:, qs], c_aug.astype(BF16))
        num = intra[:, 0:ML_DV] + wi_rep * inter[:, 0:ML_DV]
        den = intra[:, ML_DV:] + wi_rep * inter[:, ML_DV:]
        hh = num / jnp.maximum(jnp.abs(den), en_rep)
        kw_t = (env["k_t"][qs, :] * env["w_s"][h:h + 1, :]).astype(BF16)
        c_ref[h] = env["w_old"][h:h + 1, 0:ML_AUG] * c_aug + _dot(kw_t, v_aug)
        o_ref[:, vs] = (_head_rms(hh, ML_DV) * gn_ref[:, vs] * og_ref[:, vs].astype(F32)).astype(o_ref.dtype)

    return [scalars, copies] + [functools.partial(head, h) for h in range(ML_HEADS)]


FF_TILE = 256


def _layer_norm(x, g, b):
    mu = jnp.mean(x, axis=-1, keepdims=True)
    xc = x - mu
    var = jnp.mean(xc * xc, axis=-1, keepdims=True)
    return xc * lax.rsqrt(var + LN_EPS) * g + b


def _norm_phases(ohg_ref, oml_ref, x_ref, wo_ref, ln1g_ref, ln1b_ref, h1_ref, alpha):
    env = {}

    def project():
        env["mix"] = _dot(ohg_ref[...], wo_ref[0:HG_WIDTH, :]) + _dot(oml_ref[...], wo_ref[HG_WIDTH:, :])

    def norm():
        h1_ref[...] = _layer_norm(alpha * x_ref[...] + env["mix"], ln1g_ref[...], ln1b_ref[...])

    return project, norm


def _ffn_phases(h1_ref, p_ref, wg_ref, wu_ref, wd_ref, ln2g_ref, ln2b_ref, wpp_ref, wpg_ref, bpg_ref,
                out_ref, act_scr, ffn_scr, alpha, d_ff):
    env = {}
    d = h1_ref.shape[-1]

    def hidden(lo):
        if lo == 0:
            env["h1b"] = h1_ref[...].astype(BF16)
        h1b = env["h1b"]
        gate = _dot(h1b, wg_ref[:, lo:lo + FF_TILE])
        up = _dot(h1b, wu_ref[:, lo:lo + FF_TILE])
        act_scr[:, lo:lo + FF_TILE] = (_silu(gate) * up).astype(BF16)

    def down(lo):
        ffn_scr[:, lo:lo + FF_TILE] = _dot(act_scr[...], wd_ref[:, lo:lo + FF_TILE])

    def norm():
        env["pemb"] = _dot(p_ref[...].astype(BF16), wpp_ref[...])
        env["h2"] = _layer_norm(alpha * h1_ref[...] + ffn_scr[...], ln2g_ref[...], ln2b_ref[...])

    def embed():
        h2 = env["h2"]
        pgate = _sigmoid(_dot(h2.astype(BF16), wpg_ref[...]) + bpg_ref[...])
        out_ref[...] = h2 + pgate * env["pemb"]

    tiles = ([functools.partial(hidden, lo) for lo in range(0, d_ff, FF_TILE)]
             + [functools.partial(down, lo) for lo in range(0, d, FF_TILE)])
    return tiles, norm, embed


N_MIX_CONST = 9
N_TAIL_CONST = 11


def _mixer_groups(hg, ml):
    hg_setup, *hg_chunks, hg_finish = hg
    ml_scalars, ml_copies, *ml_heads = ml
    groups = [[hg_setup, ml_scalars], hg_chunks[:1], [ml_copies] + hg_chunks[1:2]]
    rest_hg, rest_ml = hg_chunks[2:], ml_heads
    for j in range(max(len(rest_hg), len(rest_ml))):
        groups.append(rest_ml[j:j + 1] + rest_hg[j:j + 1])
    groups[-1].append(hg_finish)
    return groups


def _issue_order(ffn, groups):
    order, done = [], 0
    for j, f in enumerate(ffn):
        order.append(f)
        upto = round((j + 1) / len(ffn) * len(groups))
        for group in groups[done:upto]:
            order.extend(group)
        done = max(done, upto)
    return order


def _fused_kernel(*refs, tl, sub, tiles_per_seq, alpha, d_ff):
    it = iter(refs)
    take = lambda n: [next(it) for _ in range(n)]
    (logf_ref, hq_ref, hk_ref, hv_ref, hg_ref, mq_ref, mk_ref, mv_ref, mo_ref, grow_ref) = take(N_MIX_IN)
    (hgn_ref, tril_ref, cw_ref, cb_ref, mgn_ref,
     ident_ref, triu_ref, bias_ref, selsum_ref) = take(N_MIX_CONST)
    x_ref, p_ref = take(2)
    (wo_ref, ln1g_ref, ln1b_ref, wg_ref, wu_ref, wd_ref,
     ln2g_ref, ln2b_ref, wpp_ref, wpg_ref, bpg_ref) = take(N_TAIL_CONST)
    (out_ref,) = take(1)
    st_ref, b_scr, c_scr, cbuf, c_ref, m_ref, ohg_scr, oml_scr, h1_scr, act_scr, ffn_scr = take(11)

    g = pl.program_id(0)

    @pl.when(g == 0)
    def _():
        h1_scr[...] = jnp.zeros_like(h1_scr)

    @pl.when(g % tiles_per_seq == 0)
    def _():
        st_ref[...] = jnp.zeros_like(st_ref)
        cbuf[0:CONV_PAD, :] = jnp.zeros((CONV_PAD, 2 * ML_QK_WIDTH), F32)
        c_ref[...] = jnp.zeros_like(c_ref)
        m_ref[...] = jnp.zeros_like(m_ref)

    cur = g % 2
    prev = 1 - cur
    groups = []
    for off in range(0, tl, sub):
        rows = pl.ds(off, sub)
        at = lambda ref: ref.at[rows]
        hg_phases = _hgrn2_phases(at(logf_ref), at(hq_ref), at(hk_ref), at(hv_ref), at(hg_ref), hgn_ref,
                                  tril_ref, ohg_scr.at[rows], st_ref, b_scr, c_scr, sub // CHUNK)
        ml_phases = _mlstm_phases(at(mq_ref), at(mk_ref), at(mv_ref), at(mo_ref), grow_ref.at[:, rows],
                                  cw_ref, cb_ref, mgn_ref, ident_ref, triu_ref, bias_ref, selsum_ref,
                                  oml_scr.at[rows], cbuf, c_ref, m_ref, sub)
        groups += _mixer_groups(hg_phases, ml_phases)
    project, norm1 = _norm_phases(ohg_scr, oml_scr, x_ref, wo_ref, ln1g_ref, ln1b_ref, h1_scr.at[cur], alpha)
    ffn, norm2, embed = _ffn_phases(h1_scr.at[prev], p_ref, wg_ref, wu_ref, wd_ref, ln2g_ref, ln2b_ref,
                                    wpp_ref, wpg_ref, bpg_ref, out_ref, act_scr, ffn_scr, alpha, d_ff)
    for f in _issue_order(ffn, groups) + [project, norm2, embed, norm1]:
        f()


def _fused(mix_in, hgn, conv_w, conv_b, mgn, x2, p2, tail_consts, batch, seq, alpha):
    t, d = x2.shape
    tl, sub = TAIL_TILE, MIX_TILE
    assert seq % tl == 0 and tl % sub == 0 and sub % CHUNK == 0 and sub >= ML_AUG
    tiles_per_seq = seq // tl
    n_tiles = t // tl
    d_ff = tail_consts[3].shape[1]
    assert d_ff % FF_TILE == 0
    mix_consts = (hgn, _chunk_tril(sub), conv_w, conv_b, mgn) + _mlstm_constants(sub)
    assert len(mix_in) == N_MIX_IN and len(mix_consts) == N_MIX_CONST and len(tail_consts) == N_TAIL_CONST

    def tile(lag):
        return lambda g: jnp.clip(g - lag, 0, n_tiles - 1)

    def const(a):
        return pl.BlockSpec(a.shape, lambda g: (0,) * a.ndim, pipeline_mode=pl.Buffered(1))

    mix_specs = [pl.BlockSpec((tl, a.shape[1]), lambda g: (tile(0)(g), 0)) for a in mix_in[:-1]]
    mix_specs.append(pl.BlockSpec((GATE_ROWS, tl), lambda g: (0, tile(0)(g))))
    in_specs = (mix_specs + [const(a) for a in mix_consts]
                + [pl.BlockSpec((tl, d), lambda g: (tile(0)(g), 0)),
                   pl.BlockSpec((tl, p2.shape[1]), lambda g: (tile(1)(g), 0))]
                + [const(a) for a in tail_consts])
    return pl.pallas_call(
        functools.partial(_fused_kernel, tl=tl, sub=sub, tiles_per_seq=tiles_per_seq, alpha=alpha, d_ff=d_ff),
        grid=(n_tiles + 1,),
        in_specs=in_specs,
        out_specs=pl.BlockSpec((tl, d), lambda g: (tile(1)(g), 0)),
        out_shape=jax.ShapeDtypeStruct((t, d), F32),
        scratch_shapes=[pltpu.VMEM((HG_HEADS, HG_DV, HG_DK), F32),
                        pltpu.VMEM((sub, HG_WIDTH), F32),
                        pltpu.VMEM((sub, HG_WIDTH), F32),
                        pltpu.VMEM((sub + CONV_PAD, 2 * ML_QK_WIDTH), F32),
                        pltpu.VMEM((ML_HEADS, ML_DQK, ML_AUG), F32),
                        pltpu.VMEM((GATE_TILE, sub), F32),
                        pltpu.VMEM((tl, HG_WIDTH), BF16),
                        pltpu.VMEM((tl, ML_WIDTH), BF16),
                        pltpu.VMEM((2, tl, d), F32),
                        pltpu.VMEM((tl, d_ff), BF16),
                        pltpu.VMEM((tl, d), F32)],
        compiler_params=pltpu.CompilerParams(dimension_semantics=("arbitrary",),
                                             vmem_limit_bytes=VMEM_LIMIT),
        name="mix_tail",
    )(*mix_in, *mix_consts, x2, p2, *tail_consts)


def kernel(x, p, w_in, b_in, hg_lb_logits, ml_conv_w, ml_conv_b, hg_norm_g, ml_norm_g, w_out, ln1_g, ln1_b,
           w_ffn_gate, w_ffn_up, w_ffn_down, ln2_g, ln2_b, ple_w_proj, ple_w_gate, ple_b_gate):
    batch, seq, d = x.shape
    depth = w_in.shape[0]
    t = batch * seq
    alpha = float((2 * depth) ** 0.25)
    tm = INPROJ_TILE
    assert t % tm == 0
    assert w_in.shape[2] == OFF_GATES + 2 * ML_HEADS
    assert depth == 1, "lower-bound cumsum is specialised to a single layer"

    x2 = x.reshape(t, d)
    for i in range(depth):
        w_i = w_in[i]
        w_bf = w_i.astype(BF16)
        b_row = b_in[i].reshape(1, -1)
        wg_t = w_i[:, OFF_GATES:].T
        gate_pad = ((0, GATE_TILE - ML_HEADS), (0, 0))
        wgt = jnp.concatenate([jnp.pad(wg_t[:ML_HEADS], gate_pad), jnp.pad(wg_t[ML_HEADS:], gate_pad)]).astype(BF16)
        bg = b_in[i, OFF_GATES:].reshape(2 * ML_HEADS, 1)
        bgt = jnp.concatenate([jnp.pad(bg[:ML_HEADS], gate_pad), jnp.pad(bg[ML_HEADS:], gate_pad)])

        later = (w_out[i], w_ffn_gate[i], w_ffn_up[i], w_ffn_down[i], ple_w_proj[i], ple_w_gate[i])
        mix_in, (wo, wg, wu, wd, wpp, wpg) = _inproj(x2, w_bf, b_row, wgt, bgt, hg_lb_logits, later, tm)
        tail_consts = (wo, ln1_g[i].reshape(1, d), ln1_b[i].reshape(1, d), wg, wu, wd,
                       ln2_g[i].reshape(1, d), ln2_b[i].reshape(1, d), wpp, wpg, ple_b_gate[i].reshape(1, d))
        x2 = _fused(mix_in, hg_norm_g[i].reshape(1, HG_WIDTH), ml_conv_w[i], ml_conv_b[i].reshape(1, -1),
                    ml_norm_g[i].reshape(1, ML_WIDTH), x2, p[i].reshape(t, -1), tail_consts, batch, seq, alpha)
    return x2.reshape(batch, seq, d)
```

```python
import functools

import numpy as np
import jax
import jax.numpy as jnp
from jax import lax
from jax.experimental import pallas as pl
from jax.experimental.pallas import tpu as pltpu

F32 = jnp.float32
BF16 = jnp.bfloat16

CHUNK = 64
SUB = 16
N_SUB = CHUNK // SUB
EX = 8
N_EX = CHUNK // EX
EX_PER_SUB = SUB // EX
LOG2E = 1.4426950408889634
HG_HEADS = 4
HG_DK = 128
HG_DV = 128
HG_WIDTH = HG_HEADS * HG_DV
ML_HEADS = 4
ML_DQK = 64
ML_DV = 128
ML_WIDTH = ML_HEADS * ML_DV
ML_QK_WIDTH = ML_HEADS * ML_DQK
CONV_K = 4
LN_EPS = 1e-5
RMS_EPS = 1e-6

OFF_HQ = 0
OFF_HF = OFF_HQ + HG_HEADS * HG_DK
OFF_HV = OFF_HF + HG_HEADS * HG_DK
OFF_HG = OFF_HV + HG_WIDTH
OFF_MQ = OFF_HG + HG_WIDTH
OFF_MK = OFF_MQ + ML_QK_WIDTH
OFF_MV = OFF_MK + ML_QK_WIDTH
OFF_MO = OFF_MV + ML_WIDTH
OFF_GATES = OFF_MO + ML_WIDTH
LANE = 128
GATE_TILE = 8
BF16_ROWS = 16
GATE_ROWS = 2 * GATE_TILE
PROJ_PIECE = 256
N_MIX_IN = 10

VMEM_LIMIT = 60 * 1024 * 1024
MIX_TILE = 256
TAIL_TILE = 512
INPROJ_TILE = 1024


def _sigmoid(x):
    return 0.5 * jnp.tanh(0.5 * x) + 0.5


def _silu(x):
    return x * _sigmoid(x)


def _log_sigmoid(x):
    return jnp.minimum(x, 0.0) - jnp.log(1.0 + jnp.exp(-jnp.abs(x)))


def _split3(x):
    hi = x.astype(BF16)
    r1 = x - hi.astype(F32)
    mid = r1.astype(BF16)
    lo = (r1 - mid.astype(F32)).astype(BF16)
    return hi, mid, lo


def _split3_f32(x):
    return [s.astype(F32) for s in _split3(x)]


def _dot(a, b):
    return jnp.dot(a, b, preferred_element_type=F32)


def _dot_nt(a, b):
    return lax.dot_general(a, b, (((1,), (1,)), ((), ())), preferred_element_type=F32)


def _dot_tn(a, b):
    return lax.dot_general(a, b, (((0,), (0,)), ((), ())), preferred_element_type=F32)


def _rows(blocks):
    return jnp.concatenate(blocks, axis=0)


def _cumsum_rows(tril_bf, x):
    hi = x.astype(BF16)
    lo = (x - hi.astype(F32)).astype(BF16)
    return _dot(tril_bf, hi) + _dot(tril_bf, lo)


def _head_rms(o, width):
    ms = jnp.sum(o * o, axis=-1, keepdims=True) * (1.0 / width)
    return o * lax.rsqrt(ms + RMS_EPS)


def _inproj_kernel(x_ref, w_ref, b_ref, wgt_ref, bgt_ref, lbl_ref, *refs, n_cast):
    cast_src, refs = refs[:n_cast], refs[n_cast:]
    (logf_ref, hq_ref, hk_ref, hv_ref, hg_ref, mq_ref, mk_ref, mv_ref, mo_ref, grow_ref) = refs[:N_MIX_IN]
    for src, dst in zip(cast_src, refs[N_MIX_IN:]):
        dst[...] = src[...].astype(BF16)

    xb = x_ref[...].astype(BF16)

    logits = lbl_ref[...]
    mx = jnp.max(logits, axis=0, keepdims=True)
    ex = jnp.exp(logits - mx)
    den = jnp.sum(ex, axis=0, keepdims=True)
    lb = ex[0:1, :] / den
    one_m_lb = (den - ex[0:1, :]) / den

    def plain(out_ref):
        def store(u, cols):
            out_ref[:, cols] = u.astype(BF16)
        return store

    def act(out_ref, fn):
        def store(u, cols):
            out_ref[:, cols] = fn(u).astype(BF16)
        return store

    def forget(u, cols):
        sig = _sigmoid(u)
        logf_ref[:, cols] = jnp.log(lb[:, cols] + one_m_lb[:, cols] * sig)
        hk_ref[:, cols] = (one_m_lb[:, cols] * (1.0 - sig)).astype(BF16)

    n = PROJ_PIECE
    pieces = {"hq": (OFF_HQ, HG_WIDTH, act(hq_ref, _silu)), "hf": (OFF_HF, HG_WIDTH, forget),
              "hv": (OFF_HV, HG_WIDTH, plain(hv_ref)), "hg": (OFF_HG, HG_WIDTH, act(hg_ref, _silu)),
              "mq": (OFF_MQ, ML_QK_WIDTH, plain(mq_ref)), "mk": (OFF_MK, ML_QK_WIDTH, plain(mk_ref)),
              "mv": (OFF_MV, ML_WIDTH, plain(mv_ref)), "mo": (OFF_MO, ML_WIDTH, act(mo_ref, _sigmoid))}
    order = ["hq", "hv", "hf", "mq", "hg", "hv", "mo", "mk", "hq", "mv", "hf", "mv", "hg", "mo"]
    taken = {name: 0 for name in pieces}
    for name in order:
        off, width, store = pieces[name]
        lo = taken[name]
        taken[name] = lo + n
        store(_dot(xb, w_ref[:, off + lo:off + lo + n]) + b_ref[:, off + lo:off + lo + n], slice(lo, lo + n))
    assert all(taken[name] == pieces[name][1] for name in pieces)

    gt = _dot_nt(wgt_ref[...], xb) + bgt_ref[...]
    sub = lax.broadcasted_iota(jnp.int32, gt.shape, 0)
    is_fgate = (sub >= GATE_TILE) & (sub < GATE_TILE + ML_HEADS)
    grow_ref[...] = jnp.where(is_fgate, _log_sigmoid(gt), gt)


def _cast_blocks(rows, steps):
    per = 1
    while steps % per or rows % (steps // per) or (rows // (steps // per)) % BF16_ROWS:
        per += 1
        assert per <= steps, (rows, steps)
    return rows // (steps // per), per


def _inproj(x2, w_bf, b_row, wgt, bgt, lb_logits, to_cast, tm):
    t, d = x2.shape
    grid = (t // tm,)
    row = lambda i: (i, 0)
    const = lambda i: (0, 0)
    cast_specs = []
    for a in to_cast:
        block_rows, per = _cast_blocks(a.shape[0], grid[0])
        cast_specs.append(pl.BlockSpec((block_rows, a.shape[1]), lambda i, per=per: (i // per, 0)))
    out_shapes = (
        jax.ShapeDtypeStruct((t, HG_WIDTH), F32),
        jax.ShapeDtypeStruct((t, HG_WIDTH), BF16),
        jax.ShapeDtypeStruct((t, HG_WIDTH), BF16),
        jax.ShapeDtypeStruct((t, HG_WIDTH), BF16),
        jax.ShapeDtypeStruct((t, HG_WIDTH), BF16),
        jax.ShapeDtypeStruct((t, ML_QK_WIDTH), BF16),
        jax.ShapeDtypeStruct((t, ML_QK_WIDTH), BF16),
        jax.ShapeDtypeStruct((t, ML_WIDTH), BF16),
        jax.ShapeDtypeStruct((t, ML_WIDTH), BF16),
        jax.ShapeDtypeStruct((GATE_ROWS, t), F32),
    ) + tuple(jax.ShapeDtypeStruct(a.shape, BF16) for a in to_cast)
    out_specs = (
        pl.BlockSpec((tm, HG_WIDTH), row), pl.BlockSpec((tm, HG_WIDTH), row),
        pl.BlockSpec((tm, HG_WIDTH), row), pl.BlockSpec((tm, HG_WIDTH), row),
        pl.BlockSpec((tm, HG_WIDTH), row), pl.BlockSpec((tm, ML_QK_WIDTH), row),
        pl.BlockSpec((tm, ML_QK_WIDTH), row), pl.BlockSpec((tm, ML_WIDTH), row),
        pl.BlockSpec((tm, ML_WIDTH), row),
        pl.BlockSpec((GATE_ROWS, tm), lambda i: (0, i)),
    ) + tuple(cast_specs)
    in_specs = [
        pl.BlockSpec((tm, d), row),
        pl.BlockSpec((d, OFF_GATES), const, pipeline_mode=pl.Buffered(1)),
        pl.BlockSpec((1, OFF_GATES), const),
        pl.BlockSpec(wgt.shape, const),
        pl.BlockSpec(bgt.shape, const),
        pl.BlockSpec(lb_logits.shape, const),
    ] + cast_specs
    outs = pl.pallas_call(
        functools.partial(_inproj_kernel, n_cast=len(to_cast)),
        grid=grid, in_specs=in_specs, out_specs=out_specs, out_shape=out_shapes,
        compiler_params=pltpu.CompilerParams(dimension_semantics=("arbitrary",),
                                             vmem_limit_bytes=VMEM_LIMIT),
        name="inproj",
    )(x2, w_bf, b_row, wgt, bgt, lb_logits, *to_cast)
    return outs[:N_MIX_IN], outs[N_MIX_IN:]


def _chunk_tril(ts):
    idx = np.arange(ts)
    return jnp.asarray((idx[:, None] >= idx[None, :]) & (idx[:, None] // CHUNK == idx[None, :] // CHUNK), BF16)


def _hgrn2_phases(logf_ref, q_ref, k_ref, v_ref, og_ref, gn_ref, tril_ref, o_ref,
                  st_ref, b_scr, c_scr, n_chunks):
    row = lax.broadcasted_iota(jnp.int32, (CHUNK, CHUNK), 0)
    col = lax.broadcasted_iota(jnp.int32, (CHUNK, CHUNK), 1)
    exact_mask = (col <= row) & (row // EX == col // EX)
    pair_mask = row // SUB == col // SUB
    key_pos = col % EX
    env = {}

    def blocks(vals):
        return _rows([jnp.broadcast_to(jnp.asarray(x, F32), (EX, HG_WIDTH)) for x in vals])

    def setup():
        b_scr[...] = _cumsum_rows(tril_ref[...], logf_ref[...]) * LOG2E
        c_scr[...] = b_scr[...] - jnp.log2(k_ref[...].astype(F32))
        env["states"] = [st_ref[h] for h in range(HG_HEADS)]

    def scores(c):
        r0 = c * CHUNK
        rows = pl.ds(r0, CHUNK)
        b2 = b_scr[rows, :]
        c2 = c_scr[rows, :]
        qf = q_ref[rows, :].astype(F32)

        bd = [jnp.zeros((1, HG_WIDTH), F32)]
        bd += [b_scr[pl.ds(r0 + EX * m - 1, 1), :] for m in range(1, N_EX + 1)]
        per_sub = EX_PER_SUB
        sub_start = blocks([bd[(m // per_sub) * per_sub] for m in range(N_EX)])
        q_st = qf * jnp.exp2(b2 - sub_start)
        q_in = (q_st * jnp.exp2(sub_start)).astype(BF16)
        k_dec = jnp.exp2(bd[N_EX] - c2).astype(BF16)
        chunk_decay = jnp.exp2(bd[N_EX])
        q_half = qf * jnp.exp2(b2 - blocks([bd[m] if m % per_sub else jnp.inf for m in range(N_EX)]))
        k_half = jnp.exp2(blocks([-jnp.inf if m % per_sub else bd[m + 1] for m in range(N_EX)]) - c2)

        a_heads = []
        for h in range(HG_HEADS):
            hs = slice(h * HG_DK, (h + 1) * HG_DK)
            exact = jnp.zeros((CHUNK, CHUNK), F32)
            for j in range(EX):
                c_j = _rows([jnp.broadcast_to(c_scr[pl.ds(r0 + m * EX + j, 1), hs], (EX, HG_DK))
                            for m in range(N_EX)])
                prod = qf[:, hs] * jnp.exp2(b2[:, hs] - c_j)
                exact = jnp.where(key_pos == j, jnp.sum(prod, axis=-1, keepdims=True), exact)
            half = _dot_nt(q_half[:, hs].astype(BF16), k_half[:, hs].astype(BF16))
            off = [jnp.zeros((SUB, CHUNK), F32)]
            for i in range(1, N_SUB):
                k_i = jnp.exp2(bd[i * per_sub][:, hs] - c2[0:i * SUB, hs])
                k_i = _rows([k_i, jnp.zeros((CHUNK - i * SUB, HG_DK), F32)])
                off.append(_dot_nt(q_st[i * SUB:(i + 1) * SUB, hs].astype(BF16), k_i.astype(BF16)))
            a = jnp.where(exact_mask, exact, jnp.where(pair_mask, half, 0.0)) + _rows(off)
            a_heads.append(a.astype(BF16))
        env[c] = (a_heads, q_in, k_dec, chunk_decay)

    def outputs(c):
        a_heads, q_in, k_dec, chunk_decay = env.pop(c)
        states = env["states"]
        rows = pl.ds(c * CHUNK, CHUNK)
        v = v_ref[rows, :]
        outs = []
        for h in range(HG_HEADS):
            hs = slice(h * HG_DK, (h + 1) * HG_DK)
            st = states[h]
            o_h = _dot(a_heads[h], v[:, hs]) + _dot_nt(q_in[:, hs], st.astype(BF16))
            states[h] = st * chunk_decay[:, hs] + _dot_tn(v[:, hs], k_dec[:, hs])
            outs.append(_head_rms(o_h, HG_DV))
        o = jnp.concatenate(outs, axis=-1)
        o_ref[rows, :] = (o * gn_ref[...] * og_ref[rows, :].astype(F32)).astype(o_ref.dtype)

    def finish():
        for h in range(HG_HEADS):
            st_ref[h] = env["states"][h]

    phases = [setup, functools.partial(scores, 0)]
    for c in range(n_chunks):
        step = [functools.partial(scores, c + 1)] if c + 1 < n_chunks else []
        step.append(functools.partial(outputs, c))
        phases.append(lambda step=step: [f() for f in step])
    phases.append(finish)
    return phases


CONV_PAD = 8
SEL_ROWS = 128
GRP_M, GRP_WI, GRP_EN = 0, 32, 64
ML_AUG = 2 * ML_DV


def _mlstm_constants(tl):
    ident = np.eye(tl, dtype=np.float32)
    triu = np.triu(np.ones((tl, tl), np.float32))
    bias = np.where(np.tril(np.ones((tl, tl), bool)), 0.0, -np.inf).astype(np.float32)
    sel_sum = np.zeros((SEL_ROWS, LANE), np.float32)
    for q, grp in enumerate((GRP_M, GRP_WI, GRP_EN)):
        for h in range(ML_HEADS):
            for k in range(3):
                sel_sum[grp + GATE_TILE * k + h, q * GATE_TILE + h] = 1.0
    return (jnp.asarray(ident, BF16), jnp.asarray(triu, BF16), jnp.asarray(bias), jnp.asarray(sel_sum, BF16))


def _mlstm_phases(mq_ref, mk_ref, mv_ref, og_ref, grow_ref, cw_ref, cb_ref, gn_ref,
                  ident_ref, triu_ref, bias_ref, selsum_ref, o_ref,
                  cbuf, c_ref, m_ref, tl):
    env = {}

    def scalars():
        cbuf[CONV_PAD:CONV_PAD + tl, 0:ML_QK_WIDTH] = mq_ref[...].astype(F32)
        cbuf[CONV_PAD:CONV_PAD + tl, ML_QK_WIDTH:] = mk_ref[...].astype(F32)
        acc = cb_ref[...] + cw_ref[0:1, :] * cbuf[pl.ds(CONV_PAD - (CONV_K - 1), tl), :]
        for tap in range(1, CONV_K):
            acc = acc + cw_ref[tap:tap + 1, :] * cbuf[pl.ds(CONV_PAD - (CONV_K - 1) + tap, tl), :]
        qk = _silu(acc)
        cbuf[0:CONV_PAD, :] = cbuf[tl:tl + CONV_PAD, :]
        env["q"] = (qk[:, 0:ML_QK_WIDTH] * (ML_DQK ** -0.5)).astype(BF16)
        k = qk[:, ML_QK_WIDTH:]
        env["k"] = k.astype(BF16)
        env["k_t"] = k.T

        gates = grow_ref[...]
        i_g = gates[0:GATE_TILE, :]
        hi, mid, lo = _split3(gates[GATE_TILE:, :])
        triu = triu_ref[...]
        part = _dot(_rows([hi, mid]), triu)
        g = part[0:GATE_TILE, :] + part[GATE_TILE:, :] + _dot(_rows([lo, lo]), triu)[0:GATE_TILE, :]
        u = i_g - g
        lane = lax.broadcasted_iota(jnp.int32, (GATE_TILE, tl), 1)
        cm = u
        shift = 1
        while shift < tl:
            cm = jnp.maximum(cm, jnp.where(lane >= shift, pltpu.roll(cm, shift, axis=1), -jnp.inf))
            shift *= 2
        m_prev = m_ref[...]
        m_run = jnp.maximum(m_prev, cm)
        w_inter = jnp.exp(m_prev - m_run)
        e_negm = jnp.exp(-(g + m_run))
        m_last = jnp.broadcast_to(m_run[:, tl - 1:tl], (GATE_TILE, tl))
        g_last = jnp.broadcast_to(g[:, tl - 1:tl], (GATE_TILE, tl))
        env["w_s"] = jnp.exp(u - m_last)
        env["w_old"] = jnp.exp(m_prev - m_last)
        m_ref[...] = g_last + m_last

        env["u"] = u
        zeros8 = jnp.zeros((GATE_TILE, tl), F32)
        env["pack"] = _rows(_split3_f32(m_run) + [zeros8] + _split3_f32(w_inter) + [zeros8]
                            + _split3_f32(e_negm) + [zeros8] * 5).astype(BF16)

    def copies():
        pack_t = _dot_nt(ident_ref[...], env["pack"]).astype(BF16)
        env["cols"] = _dot(pack_t, selsum_ref[...])

    def head(h):
        qs = slice(h * ML_DQK, (h + 1) * ML_DQK)
        vs = slice(h * ML_DV, (h + 1) * ML_DV)
        q_bf, k_bf, cols = env["q"], env["k"], env["cols"]
        m_col = cols[:, h:h + 1]
        wi_rep = jnp.broadcast_to(cols[:, GATE_TILE + h:GATE_TILE + h + 1], (tl, ML_DV))
        en_rep = jnp.broadcast_to(cols[:, 2 * GATE_TILE + h:2 * GATE_TILE + h + 1], (tl, ML_DV))
        w_intra = jnp.exp((env["u"][h:h + 1, :] - m_col) + bias_ref[...])
        p = (_dot_nt(q_bf[:, qs], k_bf[:, qs]) * w_intra).astype(BF16)
        v_aug = jnp.concatenate([mv_ref[:, vs], jnp.ones((tl, ML_DV), BF16)], axis=1)
        c_aug = c_ref[h]
        intra = _dot(p, v_aug)
        inter = _dot(q_bf[:, qs], c_aug.astype(BF16))
        num = intra[:, 0:ML_DV] + wi_rep * inter[:, 0:ML_DV]
        den = intra[:, ML_DV:] + wi_rep * inter[:, ML_DV:]
        hh = num / jnp.maximum(jnp.abs(den), en_rep)
        kw_t = (env["k_t"][qs, :] * env["w_s"][h:h + 1, :]).astype(BF16)
        c_ref[h] = env["w_old"][h:h + 1, 0:ML_AUG] * c_aug + _dot(kw_t, v_aug)
        o_ref[:, vs] = (_head_rms(hh, ML_DV) * gn_ref[:, vs] * og_ref[:, vs].astype(F32)).astype(o_ref.dtype)

    return [scalars, copies] + [functools.partial(head, h) for h in range(ML_HEADS)]


FF_TILE = 256


def _layer_norm(x, g, b):
    mu = jnp.mean(x, axis=-1, keepdims=True)
    xc = x - mu
    var = jnp.mean(xc * xc, axis=-1, keepdims=True)
    return xc * lax.rsqrt(var + LN_EPS) * g + b


def _norm_phases(ohg_ref, oml_ref, x_ref, wo_ref, ln1g_ref, ln1b_ref, h1_ref, alpha):
    env = {}

    def project():
        env["mix"] = _dot(ohg_ref[...], wo_ref[0:HG_WIDTH, :]) + _dot(oml_ref[...], wo_ref[HG_WIDTH:, :])

    def norm():
        h1_ref[...] = _layer_norm(alpha * x_ref[...] + env["mix"], ln1g_ref[...], ln1b_ref[...])

    return project, norm


def _ffn_phases(h1_ref, p_ref, wg_ref, wu_ref, wd_ref, ln2g_ref, ln2b_ref, wpp_ref, wpg_ref, bpg_ref,
                out_ref, act_scr, ffn_scr, alpha, d_ff):
    env = {}
    d = h1_ref.shape[-1]

    def hidden(lo):
        if lo == 0:
            env["h1b"] = h1_ref[...].astype(BF16)
        h1b = env["h1b"]
        gate = _dot(h1b, wg_ref[:, lo:lo + FF_TILE])
        up = _dot(h1b, wu_ref[:, lo:lo + FF_TILE])
        act_scr[:, lo:lo + FF_TILE] = (_silu(gate) * up).astype(BF16)

    def down(lo):
        ffn_scr[:, lo:lo + FF_TILE] = _dot(act_scr[...], wd_ref[:, lo:lo + FF_TILE])

    def norm():
        env["pemb"] = _dot(p_ref[...].astype(BF16), wpp_ref[...])
        env["h2"] = _layer_norm(alpha * h1_ref[...] + ffn_scr[...], ln2g_ref[...], ln2b_ref[...])

    def embed():
        h2 = env["h2"]
        pgate = _sigmoid(_dot(h2.astype(BF16), wpg_ref[...]) + bpg_ref[...])
        out_ref[...] = h2 + pgate * env["pemb"]

    tiles = ([functools.partial(hidden, lo) for lo in range(0, d_ff, FF_TILE)]
             + [functools.partial(down, lo) for lo in range(0, d, FF_TILE)])
    return tiles, norm, embed


N_MIX_CONST = 9
N_TAIL_CONST = 11


def _mixer_groups(hg, ml):
    hg_setup, *hg_chunks, hg_finish = hg
    ml_scalars, ml_copies, *ml_heads = ml
    groups = [[hg_setup, ml_scalars], hg_chunks[:1], [ml_copies] + hg_chunks[1:2]]
    rest_hg, rest_ml = hg_chunks[2:], ml_heads
    for j in range(max(len(rest_hg), len(rest_ml))):
        groups.append(rest_ml[j:j + 1] + rest_hg[j:j + 1])
    groups[-1].append(hg_finish)
    return groups


def _issue_order(ffn, groups):
    order, done = [], 0
    for j, f in enumerate(ffn):
        order.append(f)
        upto = round((j + 1) / len(ffn) * len(groups))
        for group in groups[done:upto]:
            order.extend(group)
        done = max(done, upto)
    return order


def _fused_kernel(*refs, tl, sub, tiles_per_seq, alpha, d_ff):
    it = iter(refs)
    take = lambda n: [next(it) for _ in range(n)]
    (logf_ref, hq_ref, hk_ref, hv_ref, hg_ref, mq_ref, mk_ref, mv_ref, mo_ref, grow_ref) = take(N_MIX_IN)
    (hgn_ref, tril_ref, cw_ref, cb_ref, mgn_ref,
     ident_ref, triu_ref, bias_ref, selsum_ref) = take(N_MIX_CONST)
    x_ref, p_ref = take(2)
    (wo_ref, ln1g_ref, ln1b_ref, wg_ref, wu_ref, wd_ref,
     ln2g_ref, ln2b_ref, wpp_ref, wpg_ref, bpg_ref) = take(N_TAIL_CONST)
    (out_ref,) = take(1)
    st_ref, b_scr, c_scr, cbuf, c_ref, m_ref, ohg_scr, oml_scr, h1_scr, act_scr, ffn_scr = take(11)

    g = pl.program_id(0)

    @pl.when(g == 0)
    def _():
        h1_scr[...] = jnp.zeros_like(h1_scr)

    @pl.when(g % tiles_per_seq == 0)
    def _():
        st_ref[...] = jnp.zeros_like(st_ref)
        cbuf[0:CONV_PAD, :] = jnp.zeros((CONV_PAD, 2 * ML_QK_WIDTH), F32)
        c_ref[...] = jnp.zeros_like(c_ref)
        m_ref[...] = jnp.zeros_like(m_ref)

    cur = g % 2
    prev = 1 - cur
    groups = []
    for off in range(0, tl, sub):
        rows = pl.ds(off, sub)
        at = lambda ref: ref.at[rows]
        hg_phases = _hgrn2_phases(at(logf_ref), at(hq_ref), at(hk_ref), at(hv_ref), at(hg_ref), hgn_ref,
                                  tril_ref, ohg_scr.at[rows], st_ref, b_scr, c_scr, sub // CHUNK)
        ml_phases = _mlstm_phases(at(mq_ref), at(mk_ref), at(mv_ref), at(mo_ref), grow_ref.at[:, rows],
                                  cw_ref, cb_ref, mgn_ref, ident_ref, triu_ref, bias_ref, selsum_ref,
                                  oml_scr.at[rows], cbuf, c_ref, m_ref, sub)
        groups += _mixer_groups(hg_phases, ml_phases)
    project, norm1 = _norm_phases(ohg_scr, oml_scr, x_ref, wo_ref, ln1g_ref, ln1b_ref, h1_scr.at[cur], alpha)
    ffn, norm2, embed = _ffn_phases(h1_scr.at[prev], p_ref, wg_ref, wu_ref, wd_ref, ln2g_ref, ln2b_ref,
                                    wpp_ref, wpg_ref, bpg_ref, out_ref, act_scr, ffn_scr, alpha, d_ff)
    for f in _issue_order(ffn, groups) + [project, norm2, embed, norm1]:
        f()


def _fused(mix_in, hgn, conv_w, conv_b, mgn, x2, p2, tail_consts, batch, seq, alpha):
    t, d = x2.shape
    tl, sub = TAIL_TILE, MIX_TILE
    assert seq % tl == 0 and tl % sub == 0 and sub % CHUNK == 0 and sub >= ML_AUG
    tiles_per_seq = seq // tl
    n_tiles = t // tl
    d_ff = tail_consts[3].shape[1]
    assert d_ff % FF_TILE == 0
    mix_consts = (hgn, _chunk_tril(sub), conv_w, conv_b, mgn) + _mlstm_constants(sub)
    assert len(mix_in) == N_MIX_IN and len(mix_consts) == N_MIX_CONST and len(tail_consts) == N_TAIL_CONST

    def tile(lag):
        return lambda g: jnp.clip(g - lag, 0, n_tiles - 1)

    def const(a):
        return pl.BlockSpec(a.shape, lambda g: (0,) * a.ndim, pipeline_mode=pl.Buffered(1))

    mix_specs = [pl.BlockSpec((tl, a.shape[1]), lambda g: (tile(0)(g), 0)) for a in mix_in[:-1]]
    mix_specs.append(pl.BlockSpec((GATE_ROWS, tl), lambda g: (0, tile(0)(g))))
    in_specs = (mix_specs + [const(a) for a in mix_consts]
                + [pl.BlockSpec((tl, d), lambda g: (tile(0)(g), 0)),
                   pl.BlockSpec((tl, p2.shape[1]), lambda g: (tile(1)(g), 0))]
                + [const(a) for a in tail_consts])
    return pl.pallas_call(
        functools.partial(_fused_kernel, tl=tl, sub=sub, tiles_per_seq=tiles_per_seq, alpha=alpha, d_ff=d_ff),
        grid=(n_tiles + 1,),
        in_specs=in_specs,
        out_specs=pl.BlockSpec((tl, d), lambda g: (tile(1)(g), 0)),
        out_shape=jax.ShapeDtypeStruct((t, d), F32),
        scratch_shapes=[pltpu.VMEM((HG_HEADS, HG_DV, HG_DK), F32),
                        pltpu.VMEM((sub, HG_WIDTH), F32),
                        pltpu.VMEM((sub, HG_WIDTH), F32),
                        pltpu.VMEM((sub + CONV_PAD, 2 * ML_QK_WIDTH), F32),
                        pltpu.VMEM((ML_HEADS, ML_DQK, ML_AUG), F32),
                        pltpu.VMEM((GATE_TILE, sub), F32),
                        pltpu.VMEM((tl, HG_WIDTH), BF16),
                        pltpu.VMEM((tl, ML_WIDTH), BF16),
                        pltpu.VMEM((2, tl, d), F32),
                        pltpu.VMEM((tl, d_ff), BF16),
                        pltpu.VMEM((tl, d), F32)],
        compiler_params=pltpu.CompilerParams(dimension_semantics=("arbitrary",),
                                             vmem_limit_bytes=VMEM_LIMIT),
        name="mix_tail",
    )(*mix_in, *mix_consts, x2, p2, *tail_consts)


def kernel(x, p, w_in, b_in, hg_lb_logits, ml_conv_w, ml_conv_b, hg_norm_g, ml_norm_g, w_out, ln1_g, ln1_b,
           w_ffn_gate, w_ffn_up, w_ffn_down, ln2_g, ln2_b, ple_w_proj, ple_w_gate, ple_b_gate):
    batch, seq, d = x.shape
    depth = w_in.shape[0]
    t = batch * seq
    alpha = float((2 * depth) ** 0.25)
    tm = INPROJ_TILE
    assert t % tm == 0
    assert w_in.shape[2] == OFF_GATES + 2 * ML_HEADS
    assert depth == 1, "lower-bound cumsum is specialised to a single layer"

    x2 = x.reshape(t, d)
    for i in range(depth):
        w_i = w_in[i]
        w_bf = w_i.astype(BF16)
        b_row = b_in[i].reshape(1, -1)
        wg_t = w_i[:, OFF_GATES:].T
        gate_pad = ((0, GATE_TILE - ML_HEADS), (0, 0))
        wgt = jnp.concatenate([jnp.pad(wg_t[:ML_HEADS], gate_pad), jnp.pad(wg_t[ML_HEADS:], gate_pad)]).astype(BF16)
        bg = b_in[i, OFF_GATES:].reshape(2 * ML_HEADS, 1)
        bgt = jnp.concatenate([jnp.pad(bg[:ML_HEADS], gate_pad), jnp.pad(bg[ML_HEADS:], gate_pad)])

        later = (w_out[i], w_ffn_gate[i], w_ffn_up[i], w_ffn_down[i], ple_w_proj[i], ple_w_gate[i])
        mix_in, (wo, wg, wu, wd, wpp, wpg) = _inproj(x2, w_bf, b_row, wgt, bgt, hg_lb_logits, later, tm)
        tail_consts = (wo, ln1_g[i].reshape(1, d), ln1_b[i].reshape(1, d), wg, wu, wd,
                       ln2_g[i].reshape(1, d), ln2_b[i].reshape(1, d), wpp, wpg, ple_b_gate[i].reshape(1, d))
        x2 = _fused(mix_in, hg_norm_g[i].reshape(1, HG_WIDTH), ml_conv_w[i], ml_conv_b[i].reshape(1, -1),
                    ml_norm_g[i].reshape(1, ML_WIDTH), x2, p[i].reshape(t, -1), tail_consts, batch, seq, alpha)
    return x2.reshape(batch, seq, d)
```

```python
import functools

import numpy as np
import jax
import jax.numpy as jnp
from jax import lax
from jax.experimental import pallas as pl
from jax.experimental.pallas import tpu as pltpu

F32 = jnp.float32
BF16 = jnp.bfloat16

CHUNK = 64
SUB = 16
N_SUB = CHUNK // SUB
EX = 8
N_EX = CHUNK // EX
EX_PER_SUB = SUB // EX
LOG2E = 1.4426950408889634
HG_HEADS = 4
HG_DK = 128
HG_DV = 128
HG_WIDTH = HG_HEADS * HG_DV
ML_HEADS = 4
ML_DQK = 64
ML_DV = 128
ML_WIDTH = ML_HEADS * ML_DV
ML_QK_WIDTH = ML_HEADS * ML_DQK
CONV_K = 4
LN_EPS = 1e-5
RMS_EPS = 1e-6

OFF_HQ = 0
OFF_HF = OFF_HQ + HG_HEADS * HG_DK
OFF_HV = OFF_HF + HG_HEADS * HG_DK
OFF_HG = OFF_HV + HG_WIDTH
OFF_MQ = OFF_HG + HG_WIDTH
OFF_MK = OFF_MQ + ML_QK_WIDTH
OFF_MV = OFF_MK + ML_QK_WIDTH
OFF_MO = OFF_MV + ML_WIDTH
OFF_GATES = OFF_MO + ML_WIDTH
LANE = 128
GATE_TILE = 8
BF16_ROWS = 16
GATE_ROWS = 2 * GATE_TILE
PROJ_PIECE = 256
N_MIX_IN = 10

VMEM_LIMIT = 60 * 1024 * 1024
MIX_TILE = 256
TAIL_TILE = 512
INPROJ_TILE = 1024


def _sigmoid(x):
    return 0.5 * jnp.tanh(0.5 * x) + 0.5


def _silu(x):
    h = 0.5 * x
    return h + h * jnp.tanh(h)


def _log_sigmoid(x):
    return jnp.minimum(x, 0.0) - jnp.log(1.0 + jnp.exp(-jnp.abs(x)))


def _split3(x):
    hi = x.astype(BF16)
    r1 = x - hi.astype(F32)
    mid = r1.astype(BF16)
    lo = (r1 - mid.astype(F32)).astype(BF16)
    return hi, mid, lo


def _split3_f32(x):
    return [s.astype(F32) for s in _split3(x)]


def _dot(a, b):
    return jnp.dot(a, b, preferred_element_type=F32)


def _dot_nt(a, b):
    return lax.dot_general(a, b, (((1,), (1,)), ((), ())), preferred_element_type=F32)


def _dot_tn(a, b):
    return lax.dot_general(a, b, (((0,), (0,)), ((), ())), preferred_element_type=F32)


def _rows(blocks):
    return jnp.concatenate(blocks, axis=0)


def _cumsum_rows(tril_bf, x):
    hi = x.astype(BF16)
    lo = (x - hi.astype(F32)).astype(BF16)
    return _dot(tril_bf, hi) + _dot(tril_bf, lo)


def _head_rms(o, width):
    ms = jnp.sum(o * o, axis=-1, keepdims=True) * (1.0 / width)
    return o * lax.rsqrt(ms + RMS_EPS)


def _inproj_kernel(x_ref, w_ref, b_ref, wgt_ref, bgt_ref, lbl_ref, *refs, n_cast):
    cast_src, refs = refs[:n_cast], refs[n_cast:]
    (logf_ref, hq_ref, hk_ref, hv_ref, hg_ref, mq_ref, mk_ref, mv_ref, mo_ref, grow_ref) = refs[:N_MIX_IN]
    for src, dst in zip(cast_src, refs[N_MIX_IN:]):
        dst[...] = src[...].astype(BF16)

    xb = x_ref[...].astype(BF16)

    logits = lbl_ref[...]
    mx = jnp.max(logits, axis=0, keepdims=True)
    ex = jnp.exp(logits - mx)
    den = jnp.sum(ex, axis=0, keepdims=True)
    lb = ex[0:1, :] / den
    one_m_lb = (den - ex[0:1, :]) / den

    def plain(out_ref):
        def store(u, cols):
            out_ref[:, cols] = u.astype(BF16)
        return store

    def act(out_ref, fn):
        def store(u, cols):
            out_ref[:, cols] = fn(u).astype(BF16)
        return store

    def forget(u, cols):
        sig = _sigmoid(u)
        logf_ref[:, cols] = jnp.log(lb[:, cols] + one_m_lb[:, cols] * sig)
        hk_ref[:, cols] = (one_m_lb[:, cols] * (1.0 - sig)).astype(BF16)

    n = PROJ_PIECE
    pieces = {"hq": (OFF_HQ, HG_WIDTH, act(hq_ref, _silu)), "hf": (OFF_HF, HG_WIDTH, forget),
              "hv": (OFF_HV, HG_WIDTH, plain(hv_ref)), "hg": (OFF_HG, HG_WIDTH, act(hg_ref, _silu)),
              "mq": (OFF_MQ, ML_QK_WIDTH, plain(mq_ref)), "mk": (OFF_MK, ML_QK_WIDTH, plain(mk_ref)),
              "mv": (OFF_MV, ML_WIDTH, plain(mv_ref)), "mo": (OFF_MO, ML_WIDTH, act(mo_ref, _sigmoid))}
    order = ["hq", "hv", "hf", "mq", "hg", "hv", "mo", "mk", "hq", "mv", "hf", "mv", "hg", "mo"]
    taken = {name: 0 for name in pieces}
    for name in order:
        off, width, store = pieces[name]
        lo = taken[name]
        taken[name] = lo + n
        store(_dot(xb, w_ref[:, off + lo:off + lo + n]) + b_ref[:, off + lo:off + lo + n], slice(lo, lo + n))
    assert all(taken[name] == pieces[name][1] for name in pieces)

    gt = _dot_nt(wgt_ref[...], xb) + bgt_ref[...]
    sub = lax.broadcasted_iota(jnp.int32, gt.shape, 0)
    is_fgate = (sub >= GATE_TILE) & (sub < GATE_TILE + ML_HEADS)
    grow_ref[...] = jnp.where(is_fgate, _log_sigmoid(gt), gt)


def _cast_blocks(rows, steps):
    per = 1
    while steps % per or rows % (steps // per) or (rows // (steps // per)) % BF16_ROWS:
        per += 1
        assert per <= steps, (rows, steps)
    return rows // (steps // per), per


def _inproj(x2, w_bf, b_row, wgt, bgt, lb_logits, to_cast, tm):
    t, d = x2.shape
    grid = (t // tm,)
    row = lambda i: (i, 0)
    const = lambda i: (0, 0)
    cast_specs = []
    for a in to_cast:
        block_rows, per = _cast_blocks(a.shape[0], grid[0])
        cast_specs.append(pl.BlockSpec((block_rows, a.shape[1]), lambda i, per=per: (i // per, 0)))
    out_shapes = (
        jax.ShapeDtypeStruct((t, HG_WIDTH), F32),
        jax.ShapeDtypeStruct((t, HG_WIDTH), BF16),
        jax.ShapeDtypeStruct((t, HG_WIDTH), BF16),
        jax.ShapeDtypeStruct((t, HG_WIDTH), BF16),
        jax.ShapeDtypeStruct((t, HG_WIDTH), BF16),
        jax.ShapeDtypeStruct((t, ML_QK_WIDTH), BF16),
        jax.ShapeDtypeStruct((t, ML_QK_WIDTH), BF16),
        jax.ShapeDtypeStruct((t, ML_WIDTH), BF16),
        jax.ShapeDtypeStruct((t, ML_WIDTH), BF16),
        jax.ShapeDtypeStruct((GATE_ROWS, t), F32),
    ) + tuple(jax.ShapeDtypeStruct(a.shape, BF16) for a in to_cast)
    out_specs = (
        pl.BlockSpec((tm, HG_WIDTH), row), pl.BlockSpec((tm, HG_WIDTH), row),
        pl.BlockSpec((tm, HG_WIDTH), row), pl.BlockSpec((tm, HG_WIDTH), row),
        pl.BlockSpec((tm, HG_WIDTH), row), pl.BlockSpec((tm, ML_QK_WIDTH), row),
        pl.BlockSpec((tm, ML_QK_WIDTH), row), pl.BlockSpec((tm, ML_WIDTH), row),
        pl.BlockSpec((tm, ML_WIDTH), row),
        pl.BlockSpec((GATE_ROWS, tm), lambda i: (0, i)),
    ) + tuple(cast_specs)
    in_specs = [
        pl.BlockSpec((tm, d), row),
        pl.BlockSpec((d, OFF_GATES), const, pipeline_mode=pl.Buffered(1)),
        pl.BlockSpec((1, OFF_GATES), const),
        pl.BlockSpec(wgt.shape, const),
        pl.BlockSpec(bgt.shape, const),
        pl.BlockSpec(lb_logits.shape, const),
    ] + cast_specs
    outs = pl.pallas_call(
        functools.partial(_inproj_kernel, n_cast=len(to_cast)),
        grid=grid, in_specs=in_specs, out_specs=out_specs, out_shape=out_shapes,
        compiler_params=pltpu.CompilerParams(dimension_semantics=("arbitrary",),
                                             vmem_limit_bytes=VMEM_LIMIT),
        name="inproj",
    )(x2, w_bf, b_row, wgt, bgt, lb_logits, *to_cast)
    return outs[:N_MIX_IN], outs[N_MIX_IN:]


def _chunk_tril(ts):
    idx = np.arange(ts)
    return jnp.asarray((idx[:, None] >= idx[None, :]) & (idx[:, None] // CHUNK == idx[None, :] // CHUNK), BF16)


def _hgrn2_phases(logf_ref, q_ref, k_ref, v_ref, og_ref, gn_ref, tril_ref, o_ref,
                  st_ref, b_scr, c_scr, n_chunks):
    row = lax.broadcasted_iota(jnp.int32, (CHUNK, CHUNK), 0)
    col = lax.broadcasted_iota(jnp.int32, (CHUNK, CHUNK), 1)
    exact_mask = (col <= row) & (row // EX == col // EX)
    pair_mask = row // SUB == col // SUB
    key_pos = col % EX
    env = {}

    def blocks(vals):
        return _rows([jnp.broadcast_to(jnp.asarray(x, F32), (EX, HG_WIDTH)) for x in vals])

    def setup():
        b_scr[...] = _cumsum_rows(tril_ref[...], logf_ref[...]) * LOG2E
        c_scr[...] = b_scr[...] - jnp.log2(k_ref[...].astype(F32))
        env["states"] = [st_ref[h] for h in range(HG_HEADS)]

    def scores(c):
        r0 = c * CHUNK
        rows = pl.ds(r0, CHUNK)
        b2 = b_scr[rows, :]
        c2 = c_scr[rows, :]
        qf = q_ref[rows, :].astype(F32)

        bd = [jnp.zeros((1, HG_WIDTH), F32)]
        bd += [b_scr[pl.ds(r0 + EX * m - 1, 1), :] for m in range(1, N_EX + 1)]
        per_sub = EX_PER_SUB
        sub_start = blocks([bd[(m // per_sub) * per_sub] for m in range(N_EX)])
        q_st = qf * jnp.exp2(b2 - sub_start)
        q_in = (q_st * jnp.exp2(sub_start)).astype(BF16)
        k_dec = jnp.exp2(bd[N_EX] - c2).astype(BF16)
        chunk_decay = jnp.exp2(bd[N_EX])
        q_half = qf * jnp.exp2(b2 - blocks([bd[m] if m % per_sub else jnp.inf for m in range(N_EX)]))
        k_half = jnp.exp2(blocks([-jnp.inf if m % per_sub else bd[m + 1] for m in range(N_EX)]) - c2)

        a_heads = []
        for h in range(HG_HEADS):
            hs = slice(h * HG_DK, (h + 1) * HG_DK)
            exact = jnp.zeros((CHUNK, CHUNK), F32)
            for j in range(EX):
                c_j = _rows([jnp.broadcast_to(c_scr[pl.ds(r0 + m * EX + j, 1), hs], (EX, HG_DK))
                            for m in range(N_EX)])
                prod = qf[:, hs] * jnp.exp2(b2[:, hs] - c_j)
                exact = jnp.where(key_pos == j, jnp.sum(prod, axis=-1, keepdims=True), exact)
            half = _dot_nt(q_half[:, hs].astype(BF16), k_half[:, hs].astype(BF16))
            off = [jnp.zeros((SUB, CHUNK), F32)]
            for i in range(1, N_SUB):
                k_i = jnp.exp2(bd[i * per_sub][:, hs] - c2[0:i * SUB, hs])
                k_i = _rows([k_i, jnp.zeros((CHUNK - i * SUB, HG_DK), F32)])
                off.append(_dot_nt(q_st[i * SUB:(i + 1) * SUB, hs].astype(BF16), k_i.astype(BF16)))
            a = jnp.where(exact_mask, exact, jnp.where(pair_mask, half, 0.0)) + _rows(off)
            a_heads.append(a.astype(BF16))
        env[c] = (a_heads, q_in, k_dec, chunk_decay)

    def outputs(c):
        a_heads, q_in, k_dec, chunk_decay = env.pop(c)
        states = env["states"]
        rows = pl.ds(c * CHUNK, CHUNK)
        v = v_ref[rows, :]
        outs = []
        for h in range(HG_HEADS):
            hs = slice(h * HG_DK, (h + 1) * HG_DK)
            st = states[h]
            o_h = _dot(a_heads[h], v[:, hs]) + _dot_nt(q_in[:, hs], st.astype(BF16))
            states[h] = st * chunk_decay[:, hs] + _dot_tn(v[:, hs], k_dec[:, hs])
            outs.append(_head_rms(o_h, HG_DV))
        o = jnp.concatenate(outs, axis=-1)
        o_ref[rows, :] = (o * gn_ref[...] * og_ref[rows, :].astype(F32)).astype(o_ref.dtype)

    def finish():
        for h in range(HG_HEADS):
            st_ref[h] = env["states"][h]

    phases = [setup, functools.partial(scores, 0)]
    for c in range(n_chunks):
        step = [functools.partial(scores, c + 1)] if c + 1 < n_chunks else []
        step.append(functools.partial(outputs, c))
        phases.append(lambda step=step: [f() for f in step])
    phases.append(finish)
    return phases


CONV_PAD = 8
SEL_ROWS = 128
GRP_M, GRP_WI, GRP_EN = 0, 32, 64
ML_AUG = 2 * ML_DV


def _mlstm_constants(tl):
    ident = np.eye(tl, dtype=np.float32)
    triu = np.triu(np.ones((tl, tl), np.float32))
    bias = np.where(np.tril(np.ones((tl, tl), bool)), 0.0, -np.inf).astype(np.float32)
    sel_sum = np.zeros((SEL_ROWS, LANE), np.float32)
    for q, grp in enumerate((GRP_M, GRP_WI, GRP_EN)):
        for h in range(ML_HEADS):
            for k in range(3):
                sel_sum[grp + GATE_TILE * k + h, q * GATE_TILE + h] = 1.0
    return (jnp.asarray(ident, BF16), jnp.asarray(triu, BF16), jnp.asarray(bias), jnp.asarray(sel_sum, BF16))


def _mlstm_phases(mq_ref, mk_ref, mv_ref, og_ref, grow_ref, cw_ref, cb_ref, gn_ref,
                  ident_ref, triu_ref, bias_ref, selsum_ref, o_ref,
                  cbuf, c_ref, m_ref, tl):
    env = {}

    def scalars():
        cbuf[CONV_PAD:CONV_PAD + tl, 0:ML_QK_WIDTH] = mq_ref[...].astype(F32)
        cbuf[CONV_PAD:CONV_PAD + tl, ML_QK_WIDTH:] = mk_ref[...].astype(F32)
        acc = cb_ref[...] + cw_ref[0:1, :] * cbuf[pl.ds(CONV_PAD - (CONV_K - 1), tl), :]
        for tap in range(1, CONV_K):
            acc = acc + cw_ref[tap:tap + 1, :] * cbuf[pl.ds(CONV_PAD - (CONV_K - 1) + tap, tl), :]
        qk = _silu(acc)
        cbuf[0:CONV_PAD, :] = cbuf[tl:tl + CONV_PAD, :]
        env["q"] = (qk[:, 0:ML_QK_WIDTH] * (ML_DQK ** -0.5)).astype(BF16)
        k = qk[:, ML_QK_WIDTH:]
        env["k"] = k.astype(BF16)
        env["k_t"] = k.T

        gates = grow_ref[...]
        i_g = gates[0:GATE_TILE, :]
        hi, mid, lo = _split3(gates[GATE_TILE:, :])
        triu = triu_ref[...]
        part = _dot(_rows([hi, mid]), triu)
        g = part[0:GATE_TILE, :] + part[GATE_TILE:, :] + _dot(_rows([lo, lo]), triu)[0:GATE_TILE, :]
        u = i_g - g
        lane = lax.broadcasted_iota(jnp.int32, (GATE_TILE, tl), 1)
        cm = u
        shift = 1
        while shift < tl:
            cm = jnp.maximum(cm, jnp.where(lane >= shift, pltpu.roll(cm, shift, axis=1), -jnp.inf))
            shift *= 2
        m_prev = m_ref[...]
        m_run = jnp.maximum(m_prev, cm)
        w_inter = jnp.exp(m_prev - m_run)
        e_negm = jnp.exp(-(g + m_run))
        m_last = jnp.broadcast_to(m_run[:, tl - 1:tl], (GATE_TILE, tl))
        g_last = jnp.broadcast_to(g[:, tl - 1:tl], (GATE_TILE, tl))
        env["w_s"] = jnp.exp(u - m_last)
        env["w_old"] = jnp.exp(m_prev - m_last)
        m_ref[...] = g_last + m_last

        env["u"] = u
        zeros8 = jnp.zeros((GATE_TILE, tl), F32)
        env["pack"] = _rows(_split3_f32(m_run) + [zeros8] + _split3_f32(w_inter) + [zeros8]
                            + _split3_f32(e_negm) + [zeros8] * 5).astype(BF16)

    def copies():
        pack_t = _dot_nt(ident_ref[...], env["pack"]).astype(BF16)
        env["cols"] = _dot(pack_t, selsum_ref[...])

    def head(h):
        qs = slice(h * ML_DQK, (h + 1) * ML_DQK)
        vs = slice(h * ML_DV, (h + 1) * ML_DV)
        q_bf, k_bf, cols = env["q"], env["k"], env["cols"]
        m_col = cols[:, h:h + 1]
        wi_rep = jnp.broadcast_to(cols[:, GATE_TILE + h:GATE_TILE + h + 1], (tl, ML_DV))
        en_rep = jnp.broadcast_to(cols[:, 2 * GATE_TILE + h:2 * GATE_TILE + h + 1], (tl, ML_DV))
        w_intra = jnp.exp((env["u"][h:h + 1, :] - m_col) + bias_ref[...])
        p = (_dot_nt(q_bf[:, qs], k_bf[:, qs]) * w_intra).astype(BF16)
        v_aug = jnp.concatenate([mv_ref[:, vs], jnp.ones((tl, ML_DV), BF16)], axis=1)
        c_aug = c_ref[h]
        intra = _dot(p, v_aug)
        inter = _dot(q_bf[:, qs], c_aug.astype(BF16))
        num = intra[:, 0:ML_DV] + wi_rep * inter[:, 0:ML_DV]
        den = intra[:, ML_DV:] + wi_rep * inter[:, ML_DV:]
        hh = num / jnp.maximum(jnp.abs(den), en_rep)
        kw_t = (env["k_t"][qs, :] * env["w_s"][h:h + 1, :]).astype(BF16)
        c_ref[h] = env["w_old"][h:h + 1, 0:ML_AUG] * c_aug + _dot(kw_t, v_aug)
        o_ref[:, vs] = (_head_rms(hh, ML_DV) * gn_ref[:, vs] * og_ref[:, vs].astype(F32)).astype(o_ref.dtype)

    return [scalars, copies] + [functools.partial(head, h) for h in range(ML_HEADS)]


FF_TILE = 256


def _layer_norm(x, g, b):
    mu = jnp.mean(x, axis=-1, keepdims=True)
    xc = x - mu
    var = jnp.mean(xc * xc, axis=-1, keepdims=True)
    return xc * lax.rsqrt(var + LN_EPS) * g + b


def _norm_phases(o_ref, x_ref, wo_ref, ln1g_ref, ln1b_ref, h1_ref, alpha):
    env = {}

    def project():
        env["mix"] = _dot(o_ref[...], wo_ref[...])

    def norm():
        h1_ref[...] = _layer_norm(alpha * x_ref[...] + env["mix"], ln1g_ref[...], ln1b_ref[...])

    return project, norm


def _ffn_phases(h1_ref, p_ref, wg_ref, wu_ref, wd_ref, ln2g_ref, ln2b_ref, wpp_ref, wpg_ref, bpg_ref,
                out_ref, act_scr, ffn_scr, alpha, d_ff):
    env = {}
    d = h1_ref.shape[-1]

    def hidden(lo):
        if lo == 0:
            env["h1b"] = h1_ref[...].astype(BF16)
        h1b = env["h1b"]
        gate = _dot(h1b, wg_ref[:, lo:lo + FF_TILE])
        up = _dot(h1b, wu_ref[:, lo:lo + FF_TILE])
        act_scr[:, lo:lo + FF_TILE] = (_silu(gate) * up).astype(BF16)

    def down(lo):
        ffn_scr[:, lo:lo + FF_TILE] = _dot(act_scr[...], wd_ref[:, lo:lo + FF_TILE])

    def norm():
        env["pemb"] = _dot(p_ref[...].astype(BF16), wpp_ref[...])
        env["h2"] = _layer_norm(alpha * h1_ref[...] + ffn_scr[...], ln2g_ref[...], ln2b_ref[...])

    def embed():
        h2 = env["h2"]
        pgate = _sigmoid(_dot(h2.astype(BF16), wpg_ref[...]) + bpg_ref[...])
        out_ref[...] = h2 + pgate * env["pemb"]

    tiles = ([functools.partial(hidden, lo) for lo in range(0, d_ff, FF_TILE)]
             + [functools.partial(down, lo) for lo in range(0, d, FF_TILE)])
    return tiles, norm, embed


N_MIX_CONST = 9
N_TAIL_CONST = 11


def _mixer_groups(hg, ml):
    hg_setup, *hg_chunks, hg_finish = hg
    ml_scalars, ml_copies, *ml_heads = ml
    groups = [[hg_setup, ml_scalars], hg_chunks[:1], [ml_copies] + hg_chunks[1:2]]
    rest_hg, rest_ml = hg_chunks[2:], ml_heads
    for j in range(max(len(rest_hg), len(rest_ml))):
        groups.append(rest_ml[j:j + 1] + rest_hg[j:j + 1])
    groups[-1].append(hg_finish)
    return groups


def _issue_order(ffn, groups):
    order, done = [], 0
    for j, f in enumerate(ffn):
        order.append(f)
        upto = round((j + 1) / len(ffn) * len(groups))
        for group in groups[done:upto]:
            order.extend(group)
        done = max(done, upto)
    return order


def _fused_kernel(*refs, tl, sub, tiles_per_seq, alpha, d_ff):
    it = iter(refs)
    take = lambda n: [next(it) for _ in range(n)]
    (logf_ref, hq_ref, hk_ref, hv_ref, hg_ref, mq_ref, mk_ref, mv_ref, mo_ref, grow_ref) = take(N_MIX_IN)
    (hgn_ref, tril_ref, cw_ref, cb_ref, mgn_ref,
     ident_ref, triu_ref, bias_ref, selsum_ref) = take(N_MIX_CONST)
    x_ref, p_ref = take(2)
    (wo_ref, ln1g_ref, ln1b_ref, wg_ref, wu_ref, wd_ref,
     ln2g_ref, ln2b_ref, wpp_ref, wpg_ref, bpg_ref) = take(N_TAIL_CONST)
    (out_ref,) = take(1)
    st_ref, b_scr, c_scr, cbuf, c_ref, m_ref, o_scr, h1_scr, act_scr, ffn_scr = take(10)

    g = pl.program_id(0)

    @pl.when(g == 0)
    def _():
        h1_scr[...] = jnp.zeros_like(h1_scr)

    @pl.when(g % tiles_per_seq == 0)
    def _():
        st_ref[...] = jnp.zeros_like(st_ref)
        cbuf[0:CONV_PAD, :] = jnp.zeros((CONV_PAD, 2 * ML_QK_WIDTH), F32)
        c_ref[...] = jnp.zeros_like(c_ref)
        m_ref[...] = jnp.zeros_like(m_ref)

    cur = g % 2
    prev = 1 - cur
    groups = []
    for off in range(0, tl, sub):
        rows = pl.ds(off, sub)
        at = lambda ref: ref.at[rows]
        hg_phases = _hgrn2_phases(at(logf_ref), at(hq_ref), at(hk_ref), at(hv_ref), at(hg_ref), hgn_ref,
                                  tril_ref, o_scr.at[rows, pl.ds(0, HG_WIDTH)], st_ref, b_scr, c_scr, sub // CHUNK)
        ml_phases = _mlstm_phases(at(mq_ref), at(mk_ref), at(mv_ref), at(mo_ref), grow_ref.at[:, rows],
                                  cw_ref, cb_ref, mgn_ref, ident_ref, triu_ref, bias_ref, selsum_ref,
                                  o_scr.at[rows, pl.ds(HG_WIDTH, ML_WIDTH)], cbuf, c_ref, m_ref, sub)
        groups += _mixer_groups(hg_phases, ml_phases)
    project, norm1 = _norm_phases(o_scr, x_ref, wo_ref, ln1g_ref, ln1b_ref, h1_scr.at[cur], alpha)
    ffn, norm2, embed = _ffn_phases(h1_scr.at[prev], p_ref, wg_ref, wu_ref, wd_ref, ln2g_ref, ln2b_ref,
                                    wpp_ref, wpg_ref, bpg_ref, out_ref, act_scr, ffn_scr, alpha, d_ff)
    for f in _issue_order(ffn, groups) + [project, norm2, embed, norm1]:
        f()


def _fused(mix_in, hgn, conv_w, conv_b, mgn, x2, p2, tail_consts, batch, seq, alpha):
    t, d = x2.shape
    tl, sub = TAIL_TILE, MIX_TILE
    assert seq % tl == 0 and tl % sub == 0 and sub % CHUNK == 0 and sub >= ML_AUG
    tiles_per_seq = seq // tl
    n_tiles = t // tl
    d_ff = tail_consts[3].shape[1]
    assert d_ff % FF_TILE == 0
    mix_consts = (hgn, _chunk_tril(sub), conv_w, conv_b, mgn) + _mlstm_constants(sub)
    assert len(mix_in) == N_MIX_IN and len(mix_consts) == N_MIX_CONST and len(tail_consts) == N_TAIL_CONST

    def tile(lag):
        return lambda g: jnp.clip(g - lag, 0, n_tiles - 1)

    def const(a):
        return pl.BlockSpec(a.shape, lambda g: (0,) * a.ndim, pipeline_mode=pl.Buffered(1))

    mix_specs = [pl.BlockSpec((tl, a.shape[1]), lambda g: (tile(0)(g), 0)) for a in mix_in[:-1]]
    mix_specs.append(pl.BlockSpec((GATE_ROWS, tl), lambda g: (0, tile(0)(g))))
    in_specs = (mix_specs + [const(a) for a in mix_consts]
                + [pl.BlockSpec((tl, d), lambda g: (tile(0)(g), 0)),
                   pl.BlockSpec((tl, p2.shape[1]), lambda g: (tile(1)(g), 0))]
                + [const(a) for a in tail_consts])
    return pl.pallas_call(
        functools.partial(_fused_kernel, tl=tl, sub=sub, tiles_per_seq=tiles_per_seq, alpha=alpha, d_ff=d_ff),
        grid=(n_tiles + 1,),
        in_specs=in_specs,
        out_specs=pl.BlockSpec((tl, d), lambda g: (tile(1)(g), 0)),
        out_shape=jax.ShapeDtypeStruct((t, d), F32),
        scratch_shapes=[pltpu.VMEM((HG_HEADS, HG_DV, HG_DK), F32),
                        pltpu.VMEM((sub, HG_WIDTH), F32),
                        pltpu.VMEM((sub, HG_WIDTH), F32),
                        pltpu.VMEM((sub + CONV_PAD, 2 * ML_QK_WIDTH), F32),
                        pltpu.VMEM((ML_HEADS, ML_DQK, ML_AUG), F32),
                        pltpu.VMEM((GATE_TILE, sub), F32),
                        pltpu.VMEM((tl, HG_WIDTH + ML_WIDTH), BF16),
                        pltpu.VMEM((2, tl, d), F32),
                        pltpu.VMEM((tl, d_ff), BF16),
                        pltpu.VMEM((tl, d), F32)],
        compiler_params=pltpu.CompilerParams(dimension_semantics=("arbitrary",),
                                             vmem_limit_bytes=VMEM_LIMIT),
        name="mix_tail",
    )(*mix_in, *mix_consts, x2, p2, *tail_consts)


def kernel(x, p, w_in, b_in, hg_lb_logits, ml_conv_w, ml_conv_b, hg_norm_g, ml_norm_g, w_out, ln1_g, ln1_b,
           w_ffn_gate, w_ffn_up, w_ffn_down, ln2_g, ln2_b, ple_w_proj, ple_w_gate, ple_b_gate):
    batch, seq, d = x.shape
    depth = w_in.shape[0]
    t = batch * seq
    alpha = float((2 * depth) ** 0.25)
    tm = INPROJ_TILE
    assert t % tm == 0
    assert w_in.shape[2] == OFF_GATES + 2 * ML_HEADS
    assert depth == 1, "lower-bound cumsum is specialised to a single layer"

    x2 = x.reshape(t, d)
    for i in range(depth):
        w_i = w_in[i]
        w_bf = w_i.astype(BF16)
        b_row = b_in[i].reshape(1, -1)
        wg_t = w_i[:, OFF_GATES:].T
        gate_pad = ((0, GATE_TILE - ML_HEADS), (0, 0))
        wgt = jnp.concatenate([jnp.pad(wg_t[:ML_HEADS], gate_pad), jnp.pad(wg_t[ML_HEADS:], gate_pad)]).astype(BF16)
        bg = b_in[i, OFF_GATES:].reshape(2 * ML_HEADS, 1)
        bgt = jnp.concatenate([jnp.pad(bg[:ML_HEADS], gate_pad), jnp.pad(bg[ML_HEADS:], gate_pad)])

        later = (w_out[i], w_ffn_gate[i], w_ffn_up[i], w_ffn_down[i], ple_w_proj[i], ple_w_gate[i])
        mix_in, (wo, wg, wu, wd, wpp, wpg) = _inproj(x2, w_bf, b_row, wgt, bgt, hg_lb_logits, later, tm)
        tail_consts = (wo, ln1_g[i].reshape(1, d), ln1_b[i].reshape(1, d), wg, wu, wd,
                       ln2_g[i].reshape(1, d), ln2_b[i].reshape(1, d), wpp, wpg, ple_b_gate[i].reshape(1, d))
        x2 = _fused(mix_in, hg_norm_g[i].reshape(1, HG_WIDTH), ml_conv_w[i], ml_conv_b[i].reshape(1, -1),
                    ml_norm_g[i].reshape(1, ML_WIDTH), x2, p[i].reshape(t, -1), tail_consts, batch, seq, alpha)
    return x2.reshape(batch, seq, d)
```

```python
import functools

import numpy as np
import jax
import jax.numpy as jnp
from jax import lax
from jax.experimental import pallas as pl
from jax.experimental.pallas import tpu as pltpu

F32 = jnp.float32
BF16 = jnp.bfloat16

CHUNK = 64
SUB = 16
N_SUB = CHUNK // SUB
EX = 8
N_EX = CHUNK // EX
EX_PER_SUB = SUB // EX
LOG2E = 1.4426950408889634
HG_HEADS = 4
HG_DK = 128
HG_DV = 128
HG_WIDTH = HG_HEADS * HG_DV
ML_HEADS = 4
ML_DQK = 64
ML_DV = 128
ML_WIDTH = ML_HEADS * ML_DV
ML_QK_WIDTH = ML_HEADS * ML_DQK
CONV_K = 4
LN_EPS = 1e-5
RMS_EPS = 1e-6

OFF_HQ = 0
OFF_HF = OFF_HQ + HG_HEADS * HG_DK
OFF_HV = OFF_HF + HG_HEADS * HG_DK
OFF_HG = OFF_HV + HG_WIDTH
OFF_MQ = OFF_HG + HG_WIDTH
OFF_MK = OFF_MQ + ML_QK_WIDTH
OFF_MV = OFF_MK + ML_QK_WIDTH
OFF_MO = OFF_MV + ML_WIDTH
OFF_GATES = OFF_MO + ML_WIDTH
LANE = 128
GATE_TILE = 8
BF16_ROWS = 16
GATE_ROWS = 2 * GATE_TILE
PROJ_PIECE = 256
N_MIX_IN = 10

VMEM_LIMIT = 60 * 1024 * 1024
MIX_TILE = 256
TAIL_TILE = 256
INPROJ_TILE = 1024


def _sigmoid(x):
    return 0.5 * jnp.tanh(0.5 * x) + 0.5


def _silu(x):
    return x * _sigmoid(x)


def _log_sigmoid(x):
    return jnp.minimum(x, 0.0) - jnp.log(1.0 + jnp.exp(-jnp.abs(x)))


def _split3(x):
    hi = x.astype(BF16)
    r1 = x - hi.astype(F32)
    mid = r1.astype(BF16)
    lo = (r1 - mid.astype(F32)).astype(BF16)
    return hi, mid, lo


def _split3_f32(x):
    return [s.astype(F32) for s in _split3(x)]


def _dot(a, b):
    return jnp.dot(a, b, preferred_element_type=F32)


def _dot_nt(a, b):
    return lax.dot_general(a, b, (((1,), (1,)), ((), ())), preferred_element_type=F32)


def _dot_tn(a, b):
    return lax.dot_general(a, b, (((0,), (0,)), ((), ())), preferred_element_type=F32)


def _rows(blocks):
    return jnp.concatenate(blocks, axis=0)


def _cumsum_rows(tril_bf, x):
    hi = x.astype(BF16)
    lo = (x - hi.astype(F32)).astype(BF16)
    return _dot(tril_bf, hi) + _dot(tril_bf, lo)


def _head_rms(o, width):
    ms = jnp.sum(o * o, axis=-1, keepdims=True) * (1.0 / width)
    return o * lax.rsqrt(ms + RMS_EPS)


def _inproj_kernel(x_ref, w_ref, b_ref, wgt_ref, bgt_ref, lbl_ref, *refs, n_cast):
    cast_src, refs = refs[:n_cast], refs[n_cast:]
    (logf_ref, hq_ref, hk_ref, hv_ref, hg_ref, mq_ref, mk_ref, mv_ref, mo_ref, grow_ref) = refs[:N_MIX_IN]
    for src, dst in zip(cast_src, refs[N_MIX_IN:]):
        dst[...] = src[...].astype(BF16)

    xb = x_ref[...].astype(BF16)

    logits = lbl_ref[...]
    mx = jnp.max(logits, axis=0, keepdims=True)
    ex = jnp.exp(logits - mx)
    den = jnp.sum(ex, axis=0, keepdims=True)
    lb = ex[0:1, :] / den
    one_m_lb = (den - ex[0:1, :]) / den

    def plain(out_ref):
        def store(u, cols):
            out_ref[:, cols] = u.astype(BF16)
        return store

    def act(out_ref, fn):
        def store(u, cols):
            out_ref[:, cols] = fn(u).astype(BF16)
        return store

    def forget(u, cols):
        sig = _sigmoid(u)
        logf_ref[:, cols] = jnp.log(lb[:, cols] + one_m_lb[:, cols] * sig)
        hk_ref[:, cols] = (one_m_lb[:, cols] * (1.0 - sig)).astype(BF16)

    n = PROJ_PIECE
    pieces = {"hq": (OFF_HQ, HG_WIDTH, act(hq_ref, _silu)), "hf": (OFF_HF, HG_WIDTH, forget),
              "hv": (OFF_HV, HG_WIDTH, plain(hv_ref)), "hg": (OFF_HG, HG_WIDTH, act(hg_ref, _silu)),
              "mq": (OFF_MQ, ML_QK_WIDTH, plain(mq_ref)), "mk": (OFF_MK, ML_QK_WIDTH, plain(mk_ref)),
              "mv": (OFF_MV, ML_WIDTH, plain(mv_ref)), "mo": (OFF_MO, ML_WIDTH, act(mo_ref, _sigmoid))}
    order = ["hq", "hv", "hf", "mq", "hg", "hv", "mo", "mk", "hq", "mv", "hf", "mv", "hg", "mo"]
    taken = {name: 0 for name in pieces}
    for name in order:
        off, width, store = pieces[name]
        lo = taken[name]
        taken[name] = lo + n
        store(_dot(xb, w_ref[:, off + lo:off + lo + n]) + b_ref[:, off + lo:off + lo + n], slice(lo, lo + n))
    assert all(taken[name] == pieces[name][1] for name in pieces)

    gt = _dot_nt(wgt_ref[...], xb) + bgt_ref[...]
    sub = lax.broadcasted_iota(jnp.int32, gt.shape, 0)
    is_fgate = (sub >= GATE_TILE) & (sub < GATE_TILE + ML_HEADS)
    grow_ref[...] = jnp.where(is_fgate, _log_sigmoid(gt), gt)


def _cast_blocks(rows, steps):
    per = 1
    while steps % per or rows % (steps // per) or (rows // (steps // per)) % BF16_ROWS:
        per += 1
        assert per <= steps, (rows, steps)
    return rows // (steps // per), per


def _inproj(x2, w_bf, b_row, wgt, bgt, lb_logits, to_cast, tm):
    t, d = x2.shape
    grid = (t // tm,)
    row = lambda i: (i, 0)
    const = lambda i: (0, 0)
    cast_specs = []
    for a in to_cast:
        block_rows, per = _cast_blocks(a.shape[0], grid[0])
        cast_specs.append(pl.BlockSpec((block_rows, a.shape[1]), lambda i, per=per: (i // per, 0)))
    out_shapes = (
        jax.ShapeDtypeStruct((t, HG_WIDTH), F32),
        jax.ShapeDtypeStruct((t, HG_WIDTH), BF16),
        jax.ShapeDtypeStruct((t, HG_WIDTH), BF16),
        jax.ShapeDtypeStruct((t, HG_WIDTH), BF16),
        jax.ShapeDtypeStruct((t, HG_WIDTH), BF16),
        jax.ShapeDtypeStruct((t, ML_QK_WIDTH), BF16),
        jax.ShapeDtypeStruct((t, ML_QK_WIDTH), BF16),
        jax.ShapeDtypeStruct((t, ML_WIDTH), BF16),
        jax.ShapeDtypeStruct((t, ML_WIDTH), BF16),
        jax.ShapeDtypeStruct((GATE_ROWS, t), F32),
    ) + tuple(jax.ShapeDtypeStruct(a.shape, BF16) for a in to_cast)
    out_specs = (
        pl.BlockSpec((tm, HG_WIDTH), row), pl.BlockSpec((tm, HG_WIDTH), row),
        pl.BlockSpec((tm, HG_WIDTH), row), pl.BlockSpec((tm, HG_WIDTH), row),
        pl.BlockSpec((tm, HG_WIDTH), row), pl.BlockSpec((tm, ML_QK_WIDTH), row),
        pl.BlockSpec((tm, ML_QK_WIDTH), row), pl.BlockSpec((tm, ML_WIDTH), row),
        pl.BlockSpec((tm, ML_WIDTH), row),
        pl.BlockSpec((GATE_ROWS, tm), lambda i: (0, i)),
    ) + tuple(cast_specs)
    in_specs = [
        pl.BlockSpec((tm, d), row),
        pl.BlockSpec((d, OFF_GATES), const, pipeline_mode=pl.Buffered(1)),
        pl.BlockSpec((1, OFF_GATES), const),
        pl.BlockSpec(wgt.shape, const),
        pl.BlockSpec(bgt.shape, const),
        pl.BlockSpec(lb_logits.shape, const),
    ] + cast_specs
    outs = pl.pallas_call(
        functools.partial(_inproj_kernel, n_cast=len(to_cast)),
        grid=grid, in_specs=in_specs, out_specs=out_specs, out_shape=out_shapes,
        compiler_params=pltpu.CompilerParams(dimension_semantics=("arbitrary",),
                                             vmem_limit_bytes=VMEM_LIMIT),
        name="inproj",
    )(x2, w_bf, b_row, wgt, bgt, lb_logits, *to_cast)
    return outs[:N_MIX_IN], outs[N_MIX_IN:]


def _chunk_tril(ts):
    idx = np.arange(ts)
    return jnp.asarray((idx[:, None] >= idx[None, :]) & (idx[:, None] // CHUNK == idx[None, :] // CHUNK), BF16)


def _hgrn2_phases(logf_ref, q_ref, k_ref, v_ref, og_ref, gn_ref, tril_ref, o_ref,
                  st_ref, b_scr, c_scr, n_chunks):
    row = lax.broadcasted_iota(jnp.int32, (CHUNK, CHUNK), 0)
    col = lax.broadcasted_iota(jnp.int32, (CHUNK, CHUNK), 1)
    exact_mask = (col <= row) & (row // EX == col // EX)
    pair_mask = row // SUB == col // SUB
    key_pos = col % EX
    env = {}

    def blocks(vals):
        return _rows([jnp.broadcast_to(jnp.asarray(x, F32), (EX, HG_WIDTH)) for x in vals])

    def setup():
        b_scr[...] = _cumsum_rows(tril_ref[...], logf_ref[...]) * LOG2E
        c_scr[...] = b_scr[...] - jnp.log2(k_ref[...].astype(F32))
        env["states"] = [st_ref[h] for h in range(HG_HEADS)]

    def scores(c):
        r0 = c * CHUNK
        rows = pl.ds(r0, CHUNK)
        b2 = b_scr[rows, :]
        c2 = c_scr[rows, :]
        qf = q_ref[rows, :].astype(F32)

        bd = [jnp.zeros((1, HG_WIDTH), F32)]
        bd += [b_scr[pl.ds(r0 + EX * m - 1, 1), :] for m in range(1, N_EX + 1)]
        per_sub = EX_PER_SUB
        sub_start = blocks([bd[(m // per_sub) * per_sub] for m in range(N_EX)])
        q_st = qf * jnp.exp2(b2 - sub_start)
        q_in = (q_st * jnp.exp2(sub_start)).astype(BF16)
        k_dec = jnp.exp2(bd[N_EX] - c2).astype(BF16)
        chunk_decay = jnp.exp2(bd[N_EX])
        q_half = qf * jnp.exp2(b2 - blocks([bd[m] if m % per_sub else jnp.inf for m in range(N_EX)]))
        k_half = jnp.exp2(blocks([-jnp.inf if m % per_sub else bd[m + 1] for m in range(N_EX)]) - c2)

        a_heads = []
        for h in range(HG_HEADS):
            hs = slice(h * HG_DK, (h + 1) * HG_DK)
            exact = jnp.zeros((CHUNK, CHUNK), F32)
            for j in range(EX):
                c_j = _rows([jnp.broadcast_to(c_scr[pl.ds(r0 + m * EX + j, 1), hs], (EX, HG_DK))
                            for m in range(N_EX)])
                prod = qf[:, hs] * jnp.exp2(b2[:, hs] - c_j)
                exact = jnp.where(key_pos == j, jnp.sum(prod, axis=-1, keepdims=True), exact)
            half = _dot_nt(q_half[:, hs].astype(BF16), k_half[:, hs].astype(BF16))
            off = [jnp.zeros((SUB, CHUNK), F32)]
            for i in range(1, N_SUB):
                k_i = jnp.exp2(bd[i * per_sub][:, hs] - c2[0:i * SUB, hs])
                k_i = _rows([k_i, jnp.zeros((CHUNK - i * SUB, HG_DK), F32)])
                off.append(_dot_nt(q_st[i * SUB:(i + 1) * SUB, hs].astype(BF16), k_i.astype(BF16)))
            a = jnp.where(exact_mask, exact, jnp.where(pair_mask, half, 0.0)) + _rows(off)
            a_heads.append(a.astype(BF16))
        env[c] = (a_heads, q_in, k_dec, chunk_decay)

    def outputs(c):
        a_heads, q_in, k_dec, chunk_decay = env.pop(c)
        states = env["states"]
        rows = pl.ds(c * CHUNK, CHUNK)
        v = v_ref[rows, :]
        outs = []
        for h in range(HG_HEADS):
            hs = slice(h * HG_DK, (h + 1) * HG_DK)
            st = states[h]
            o_h = _dot(a_heads[h], v[:, hs]) + _dot_nt(q_in[:, hs], st.astype(BF16))
            states[h] = st * chunk_decay[:, hs] + _dot_tn(v[:, hs], k_dec[:, hs])
            outs.append(_head_rms(o_h, HG_DV))
        o = jnp.concatenate(outs, axis=-1)
        o_ref[rows, :] = (o * gn_ref[...] * og_ref[rows, :].astype(F32)).astype(o_ref.dtype)

    def finish():
        for h in range(HG_HEADS):
            st_ref[h] = env["states"][h]

    phases = [setup, functools.partial(scores, 0)]
    for c in range(n_chunks):
        step = [functools.partial(scores, c + 1)] if c + 1 < n_chunks else []
        step.append(functools.partial(outputs, c))
        phases.append(lambda step=step: [f() for f in step])
    phases.append(finish)
    return phases


CONV_PAD = 8
SEL_ROWS = 128
GRP_M, GRP_WI, GRP_EN = 0, 32, 64
ML_AUG = 2 * ML_DV


def _mlstm_constants(tl):
    ident = np.eye(tl, dtype=np.float32)
    triu = np.triu(np.ones((tl, tl), np.float32))
    bias = np.where(np.tril(np.ones((tl, tl), bool)), 0.0, -np.inf).astype(np.float32)
    sel_sum = np.zeros((SEL_ROWS, LANE), np.float32)
    for q, grp in enumerate((GRP_M, GRP_WI, GRP_EN)):
        for h in range(ML_HEADS):
            for k in range(3):
                sel_sum[grp + GATE_TILE * k + h, q * GATE_TILE + h] = 1.0
    return (jnp.asarray(ident, BF16), jnp.asarray(triu, BF16), jnp.asarray(bias), jnp.asarray(sel_sum, BF16))


def _mlstm_phases(mq_ref, mk_ref, mv_ref, og_ref, grow_ref, cw_ref, cb_ref, gn_ref,
                  ident_ref, triu_ref, bias_ref, selsum_ref, o_ref,
                  cbuf, c_ref, m_ref, tl):
    env = {}

    def scalars():
        cbuf[CONV_PAD:CONV_PAD + tl, 0:ML_QK_WIDTH] = mq_ref[...].astype(F32)
        cbuf[CONV_PAD:CONV_PAD + tl, ML_QK_WIDTH:] = mk_ref[...].astype(F32)
        acc = cb_ref[...] + cw_ref[0:1, :] * cbuf[pl.ds(CONV_PAD - (CONV_K - 1), tl), :]
        for tap in range(1, CONV_K):
            acc = acc + cw_ref[tap:tap + 1, :] * cbuf[pl.ds(CONV_PAD - (CONV_K - 1) + tap, tl), :]
        qk = _silu(acc)
        cbuf[0:CONV_PAD, :] = cbuf[tl:tl + CONV_PAD, :]
        env["q"] = (qk[:, 0:ML_QK_WIDTH] * (ML_DQK ** -0.5)).astype(BF16)
        k = qk[:, ML_QK_WIDTH:]
        env["k"] = k.astype(BF16)
        env["k_t"] = k.T

        gates = grow_ref[...]
        i_g = gates[0:GATE_TILE, :]
        hi, mid, lo = _split3(gates[GATE_TILE:, :])
        triu = triu_ref[...]
        part = _dot(_rows([hi, mid]), triu)
        g = part[0:GATE_TILE, :] + part[GATE_TILE:, :] + _dot(_rows([lo, lo]), triu)[0:GATE_TILE, :]
        u = i_g - g
        lane = lax.broadcasted_iota(jnp.int32, (GATE_TILE, tl), 1)
        cm = u
        shift = 1
        while shift < tl:
            cm = jnp.maximum(cm, jnp.where(lane >= shift, pltpu.roll(cm, shift, axis=1), -jnp.inf))
            shift *= 2
        m_prev = m_ref[...]
        m_run = jnp.maximum(m_prev, cm)
        w_inter = jnp.exp(m_prev - m_run)
        e_negm = jnp.exp(-(g + m_run))
        m_last = jnp.broadcast_to(m_run[:, tl - 1:tl], (GATE_TILE, tl))
        g_last = jnp.broadcast_to(g[:, tl - 1:tl], (GATE_TILE, tl))
        env["w_s"] = jnp.exp(u - m_last)
        env["w_old"] = jnp.exp(m_prev - m_last)
        m_ref[...] = g_last + m_last

        env["u"] = u
        zeros8 = jnp.zeros((GATE_TILE, tl), F32)
        env["pack"] = _rows(_split3_f32(m_run) + [zeros8] + _split3_f32(w_inter) + [zeros8]
                            + _split3_f32(e_negm) + [zeros8] * 5).astype(BF16)

    def copies():
        pack_t = _dot_nt(ident_ref[...], env["pack"]).astype(BF16)
        env["cols"] = _dot(pack_t, selsum_ref[...])

    def head(h):
        qs = slice(h * ML_DQK, (h + 1) * ML_DQK)
        vs = slice(h * ML_DV, (h + 1) * ML_DV)
        q_bf, k_bf, cols = env["q"], env["k"], env["cols"]
        m_col = cols[:, h:h + 1]
        wi_rep = jnp.broadcast_to(cols[:, GATE_TILE + h:GATE_TILE + h + 1], (tl, ML_DV))
        en_rep = jnp.broadcast_to(cols[:, 2 * GATE_TILE + h:2 * GATE_TILE + h + 1], (tl, ML_DV))
        w_intra = jnp.exp((env["u"][h:h + 1, :] - m_col) + bias_ref[...])
        p = (_dot_nt(q_bf[:, qs], k_bf[:, qs]) * w_intra).astype(BF16)
        v_aug = jnp.concatenate([mv_ref[:, vs], jnp.ones((tl, ML_DV), BF16)], axis=1)
        c_aug = c_ref[h]
        intra = _dot(p, v_aug)
        inter = _dot(q_bf[:, qs], c_aug.astype(BF16))
        num = intra[:, 0:ML_DV] + wi_rep * inter[:, 0:ML_DV]
        den = intra[:, ML_DV:] + wi_rep * inter[:, ML_DV:]
        hh = num / jnp.maximum(jnp.abs(den), en_rep)
        kw_t = (env["k_t"][qs, :] * env["w_s"][h:h + 1, :]).astype(BF16)
        c_ref[h] = env["w_old"][h:h + 1, 0:ML_AUG] * c_aug + _dot(kw_t, v_aug)
        o_ref[:, vs] = (_head_rms(hh, ML_DV) * gn_ref[:, vs] * og_ref[:, vs].astype(F32)).astype(o_ref.dtype)

    return [scalars, copies] + [functools.partial(head, h) for h in range(ML_HEADS)]


FF_TILE = 256


def _layer_norm(x, g, b):
    mu = jnp.mean(x, axis=-1, keepdims=True)
    xc = x - mu
    var = jnp.mean(xc * xc, axis=-1, keepdims=True)
    return xc * lax.rsqrt(var + LN_EPS) * g + b


def _norm_phases(ohg_ref, oml_ref, x_ref, wo_ref, ln1g_ref, ln1b_ref, h1_ref, alpha):
    env = {}

    def project():
        env["mix"] = _dot(ohg_ref[...], wo_ref[0:HG_WIDTH, :]) + _dot(oml_ref[...], wo_ref[HG_WIDTH:, :])

    def norm():
        h1_ref[...] = _layer_norm(alpha * x_ref[...] + env["mix"], ln1g_ref[...], ln1b_ref[...])

    return project, norm


def _ffn_phases(h1_ref, p_ref, wg_ref, wu_ref, wd_ref, ln2g_ref, ln2b_ref, wpp_ref, wpg_ref, bpg_ref,
                out_ref, act_scr, ffn_scr, alpha, d_ff):
    env = {}
    d = h1_ref.shape[-1]

    def hidden(lo):
        if lo == 0:
            env["h1b"] = h1_ref[...].astype(BF16)
        h1b = env["h1b"]
        gate = _dot(h1b, wg_ref[:, lo:lo + FF_TILE])
        up = _dot(h1b, wu_ref[:, lo:lo + FF_TILE])
        act_scr[:, lo:lo + FF_TILE] = (_silu(gate) * up).astype(BF16)

    def down(lo):
        ffn_scr[:, lo:lo + FF_TILE] = _dot(act_scr[...], wd_ref[:, lo:lo + FF_TILE])

    def norm():
        env["pemb"] = _dot(p_ref[...].astype(BF16), wpp_ref[...])
        env["h2"] = _layer_norm(alpha * h1_ref[...] + ffn_scr[...], ln2g_ref[...], ln2b_ref[...])

    def embed():
        h2 = env["h2"]
        pgate = _sigmoid(_dot(h2.astype(BF16), wpg_ref[...]) + bpg_ref[...])
        out_ref[...] = h2 + pgate * env["pemb"]

    tiles = ([functools.partial(hidden, lo) for lo in range(0, d_ff, FF_TILE)]
             + [functools.partial(down, lo) for lo in range(0, d, FF_TILE)])
    return tiles, norm, embed


N_MIX_CONST = 9
N_TAIL_CONST = 11


def _mixer_groups(hg, ml):
    hg_setup, *hg_chunks, hg_finish = hg
    ml_scalars, ml_copies, *ml_heads = ml
    groups = [[hg_setup, ml_scalars], hg_chunks[:1], [ml_copies] + hg_chunks[1:2]]
    rest_hg, rest_ml = hg_chunks[2:], ml_heads
    for j in range(max(len(rest_hg), len(rest_ml))):
        groups.append(rest_ml[j:j + 1] + rest_hg[j:j + 1])
    groups[-1].append(hg_finish)
    return groups


def _issue_order(ffn, groups):
    order, done = [], 0
    for j, f in enumerate(ffn):
        order.append(f)
        upto = round((j + 1) / len(ffn) * len(groups))
        for group in groups[done:upto]:
            order.extend(group)
        done = max(done, upto)
    return order


def _fused_kernel(*refs, tl, sub, tiles_per_seq, alpha, d_ff):
    it = iter(refs)
    take = lambda n: [next(it) for _ in range(n)]
    (logf_ref, hq_ref, hk_ref, hv_ref, hg_ref, mq_ref, mk_ref, mv_ref, mo_ref, grow_ref) = take(N_MIX_IN)
    (hgn_ref, tril_ref, cw_ref, cb_ref, mgn_ref,
     ident_ref, triu_ref, bias_ref, selsum_ref) = take(N_MIX_CONST)
    x_ref, p_ref = take(2)
    (wo_ref, ln1g_ref, ln1b_ref, wg_ref, wu_ref, wd_ref,
     ln2g_ref, ln2b_ref, wpp_ref, wpg_ref, bpg_ref) = take(N_TAIL_CONST)
    (out_ref,) = take(1)
    st_ref, b_scr, c_scr, cbuf, c_ref, m_ref, ohg_scr, oml_scr, h1_scr, act_scr, ffn_scr = take(11)

    g = pl.program_id(0)

    @pl.when(g == 0)
    def _():
        h1_scr[...] = jnp.zeros_like(h1_scr)

    @pl.when(g % tiles_per_seq == 0)
    def _():
        st_ref[...] = jnp.zeros_like(st_ref)
        cbuf[0:CONV_PAD, :] = jnp.zeros((CONV_PAD, 2 * ML_QK_WIDTH), F32)
        c_ref[...] = jnp.zeros_like(c_ref)
        m_ref[...] = jnp.zeros_like(m_ref)

    cur = g % 2
    prev = 1 - cur
    groups = []
    for off in range(0, tl, sub):
        rows = pl.ds(off, sub)
        at = lambda ref: ref.at[rows]
        hg_phases = _hgrn2_phases(at(logf_ref), at(hq_ref), at(hk_ref), at(hv_ref), at(hg_ref), hgn_ref,
                                  tril_ref, ohg_scr.at[rows], st_ref, b_scr, c_scr, sub // CHUNK)
        ml_phases = _mlstm_phases(at(mq_ref), at(mk_ref), at(mv_ref), at(mo_ref), grow_ref.at[:, rows],
                                  cw_ref, cb_ref, mgn_ref, ident_ref, triu_ref, bias_ref, selsum_ref,
                                  oml_scr.at[rows], cbuf, c_ref, m_ref, sub)
        groups += _mixer_groups(hg_phases, ml_phases)
    project, norm1 = _norm_phases(ohg_scr, oml_scr, x_ref, wo_ref, ln1g_ref, ln1b_ref, h1_scr.at[cur], alpha)
    ffn, norm2, embed = _ffn_phases(h1_scr.at[prev], p_ref, wg_ref, wu_ref, wd_ref, ln2g_ref, ln2b_ref,
                                    wpp_ref, wpg_ref, bpg_ref, out_ref, act_scr, ffn_scr, alpha, d_ff)
    for f in _issue_order(ffn, groups) + [project, norm2, embed, norm1]:
        f()


def _fused(mix_in, hgn, conv_w, conv_b, mgn, x2, p2, tail_consts, batch, seq, alpha):
    t, d = x2.shape
    tl, sub = TAIL_TILE, MIX_TILE
    assert seq % tl == 0 and tl % sub == 0 and sub % CHUNK == 0 and sub >= ML_AUG
    tiles_per_seq = seq // tl
    n_tiles = t // tl
    d_ff = tail_consts[3].shape[1]
    assert d_ff % FF_TILE == 0
    mix_consts = (hgn, _chunk_tril(sub), conv_w, conv_b, mgn) + _mlstm_constants(sub)
    assert len(mix_in) == N_MIX_IN and len(mix_consts) == N_MIX_CONST and len(tail_consts) == N_TAIL_CONST

    def tile(lag):
        return lambda g: jnp.clip(g - lag, 0, n_tiles - 1)

    def const(a):
        return pl.BlockSpec(a.shape, lambda g: (0,) * a.ndim, pipeline_mode=pl.Buffered(1))

    mix_specs = [pl.BlockSpec((tl, a.shape[1]), lambda g: (tile(0)(g), 0)) for a in mix_in[:-1]]
    mix_specs.append(pl.BlockSpec((GATE_ROWS, tl), lambda g: (0, tile(0)(g))))
    in_specs = (mix_specs + [const(a) for a in mix_consts]
                + [pl.BlockSpec((tl, d), lambda g: (tile(0)(g), 0)),
                   pl.BlockSpec((tl, p2.shape[1]), lambda g: (tile(1)(g), 0))]
                + [const(a) for a in tail_consts])
    return pl.pallas_call(
        functools.partial(_fused_kernel, tl=tl, sub=sub, tiles_per_seq=tiles_per_seq, alpha=alpha, d_ff=d_ff),
        grid=(n_tiles + 1,),
        in_specs=in_specs,
        out_specs=pl.BlockSpec((tl, d), lambda g: (tile(1)(g), 0)),
        out_shape=jax.ShapeDtypeStruct((t, d), F32),
        scratch_shapes=[pltpu.VMEM((HG_HEADS, HG_DV, HG_DK), F32),
                        pltpu.VMEM((sub, HG_WIDTH), F32),
                        pltpu.VMEM((sub, HG_WIDTH), F32),
                        pltpu.VMEM((sub + CONV_PAD, 2 * ML_QK_WIDTH), F32),
                        pltpu.VMEM((ML_HEADS, ML_DQK, ML_AUG), F32),
                        pltpu.VMEM((GATE_TILE, sub), F32),
                        pltpu.VMEM((tl, HG_WIDTH), BF16),
                        pltpu.VMEM((tl, ML_WIDTH), BF16),
                        pltpu.VMEM((2, tl, d), F32),
                        pltpu.VMEM((tl, d_ff), BF16),
                        pltpu.VMEM((tl, d), F32)],
        compiler_params=pltpu.CompilerParams(dimension_semantics=("arbitrary",),
                                             vmem_limit_bytes=VMEM_LIMIT),
        name="mix_tail",
    )(*mix_in, *mix_consts, x2, p2, *tail_consts)


def kernel(x, p, w_in, b_in, hg_lb_logits, ml_conv_w, ml_conv_b, hg_norm_g, ml_norm_g, w_out, ln1_g, ln1_b,
           w_ffn_gate, w_ffn_up, w_ffn_down, ln2_g, ln2_b, ple_w_proj, ple_w_gate, ple_b_gate):
    batch, seq, d = x.shape
    depth = w_in.shape[0]
    t = batch * seq
    alpha = float((2 * depth) ** 0.25)
    tm = INPROJ_TILE
    assert t % tm == 0
    assert w_in.shape[2] == OFF_GATES + 2 * ML_HEADS
    assert depth == 1, "lower-bound cumsum is specialised to a single layer"

    x2 = x.reshape(t, d)
    for i in range(depth):
        w_i = w_in[i]
        w_bf = w_i.astype(BF16)
        b_row = b_in[i].reshape(1, -1)
        wg_t = w_i[:, OFF_GATES:].T
        gate_pad = ((0, GATE_TILE - ML_HEADS), (0, 0))
        wgt = jnp.concatenate([jnp.pad(wg_t[:ML_HEADS], gate_pad), jnp.pad(wg_t[ML_HEADS:], gate_pad)]).astype(BF16)
        bg = b_in[i, OFF_GATES:].reshape(2 * ML_HEADS, 1)
        bgt = jnp.concatenate([jnp.pad(bg[:ML_HEADS], gate_pad), jnp.pad(bg[ML_HEADS:], gate_pad)])

        later = (w_out[i], w_ffn_gate[i], w_ffn_up[i], w_ffn_down[i], ple_w_proj[i], ple_w_gate[i])
        mix_in, (wo, wg, wu, wd, wpp, wpg) = _inproj(x2, w_bf, b_row, wgt, bgt, hg_lb_logits, later, tm)
        tail_consts = (wo, ln1_g[i].reshape(1, d), ln1_b[i].reshape(1, d), wg, wu, wd,
                       ln2_g[i].reshape(1, d), ln2_b[i].reshape(1, d), wpp, wpg, ple_b_gate[i].reshape(1, d))
        x2 = _fused(mix_in, hg_norm_g[i].reshape(1, HG_WIDTH), ml_conv_w[i], ml_conv_b[i].reshape(1, -1),
                    ml_norm_g[i].reshape(1, ML_WIDTH), x2, p[i].reshape(t, -1), tail_consts, batch, seq, alpha)
    return x2.reshape(batch, seq, d)
```

```python
import functools

import numpy as np
import jax
import jax.numpy as jnp
from jax import lax
from jax.experimental import pallas as pl
from jax.experimental.pallas import tpu as pltpu

F32 = jnp.float32
BF16 = jnp.bfloat16

CHUNK = 64
SUB = 16
N_SUB = CHUNK // SUB
EX = 8
N_EX = CHUNK // EX
EX_PER_SUB = SUB // EX
LOG2E = 1.4426950408889634
HG_HEADS = 4
HG_DK = 128
HG_DV = 128
HG_WIDTH = HG_HEADS * HG_DV
ML_HEADS = 4
ML_DQK = 64
ML_DV = 128
ML_WIDTH = ML_HEADS * ML_DV
ML_QK_WIDTH = ML_HEADS * ML_DQK
CONV_K = 4
LN_EPS = 1e-5
RMS_EPS = 1e-6

OFF_HQ = 0
OFF_HF = OFF_HQ + HG_HEADS * HG_DK
OFF_HV = OFF_HF + HG_HEADS * HG_DK
OFF_HG = OFF_HV + HG_WIDTH
OFF_MQ = OFF_HG + HG_WIDTH
OFF_MK = OFF_MQ + ML_QK_WIDTH
OFF_MV = OFF_MK + ML_QK_WIDTH
OFF_MO = OFF_MV + ML_WIDTH
OFF_GATES = OFF_MO + ML_WIDTH
LANE = 128
GATE_TILE = 8
BF16_ROWS = 16
GATE_ROWS = 2 * GATE_TILE
PROJ_PIECE = 256
N_MIX_IN = 10

VMEM_LIMIT = 60 * 1024 * 1024
MIX_TILE = 256
TAIL_TILE = 512
INPROJ_TILE = 1024


def _sigmoid(x):
    return 0.5 * jnp.tanh(0.5 * x) + 0.5


def _silu(x):
    return x * _sigmoid(x)


def _log_sigmoid(x):
    return jnp.minimum(x, 0.0) - jnp.log(1.0 + jnp.exp(-jnp.abs(x)))


def _split3(x):
    hi = x.astype(BF16)
    r1 = x - hi.astype(F32)
    mid = r1.astype(BF16)
    lo = (r1 - mid.astype(F32)).astype(BF16)
    return hi, mid, lo


def _split3_f32(x):
    return [s.astype(F32) for s in _split3(x)]


def _dot(a, b):
    return jnp.dot(a, b, preferred_element_type=F32)


def _dot_nt(a, b):
    return lax.dot_general(a, b, (((1,), (1,)), ((), ())), preferred_element_type=F32)


def _dot_tn(a, b):
    return lax.dot_general(a, b, (((0,), (0,)), ((), ())), preferred_element_type=F32)


def _rows(blocks):
    return jnp.concatenate(blocks, axis=0)


def _cumsum_rows(tril_bf, x):
    hi = x.astype(BF16)
    lo = (x - hi.astype(F32)).astype(BF16)
    return _dot(tril_bf, hi) + _dot(tril_bf, lo)


def _head_rms(o, width):
    ms = jnp.sum(o * o, axis=-1, keepdims=True) * (1.0 / width)
    return o * lax.rsqrt(ms + RMS_EPS)


def _inproj_kernel(x_ref, w_ref, b_ref, wgt_ref, bgt_ref, lbl_ref, *refs, n_cast):
    cast_src, refs = refs[:n_cast], refs[n_cast:]
    (logf_ref, hq_ref, hk_ref, hv_ref, hg_ref, mq_ref, mk_ref, mv_ref, mo_ref, grow_ref) = refs[:N_MIX_IN]
    for src, dst in zip(cast_src, refs[N_MIX_IN:]):
        dst[...] = src[...].astype(BF16)

    xb = x_ref[...].astype(BF16)

    logits = lbl_ref[...]
    mx = jnp.max(logits, axis=0, keepdims=True)
    ex = jnp.exp(logits - mx)
    den = jnp.sum(ex, axis=0, keepdims=True)
    lb = ex[0:1, :] / den
    one_m_lb = (den - ex[0:1, :]) / den

    def plain(out_ref):
        def store(u, cols):
            out_ref[:, cols] = u.astype(BF16)
        return store

    def act(out_ref, fn):
        def store(u, cols):
            out_ref[:, cols] = fn(u).astype(BF16)
        return store

    def forget(u, cols):
        sig = _sigmoid(u)
        logf_ref[:, cols] = jnp.log(lb[:, cols] + one_m_lb[:, cols] * sig)
        hk_ref[:, cols] = (one_m_lb[:, cols] * (1.0 - sig)).astype(BF16)

    n = PROJ_PIECE
    pieces = {"hq": (OFF_HQ, HG_WIDTH, act(hq_ref, _silu)), "hf": (OFF_HF, HG_WIDTH, forget),
              "hv": (OFF_HV, HG_WIDTH, plain(hv_ref)), "hg": (OFF_HG, HG_WIDTH, act(hg_ref, _silu)),
              "mq": (OFF_MQ, ML_QK_WIDTH, plain(mq_ref)), "mk": (OFF_MK, ML_QK_WIDTH, plain(mk_ref)),
              "mv": (OFF_MV, ML_WIDTH, plain(mv_ref)), "mo": (OFF_MO, ML_WIDTH, act(mo_ref, _sigmoid))}
    order = ["hq", "hv", "hf", "mq", "hg", "hv", "mo", "mk", "hq", "mv", "hf", "mv", "hg", "mo"]
    taken = {name: 0 for name in pieces}
    for name in order:
        off, width, store = pieces[name]
        lo = taken[name]
        taken[name] = lo + n
        store(_dot(xb, w_ref[:, off + lo:off + lo + n]) + b_ref[:, off + lo:off + lo + n], slice(lo, lo + n))
    assert all(taken[name] == pieces[name][1] for name in pieces)

    gt = _dot_nt(wgt_ref[...], xb) + bgt_ref[...]
    sub = lax.broadcasted_iota(jnp.int32, gt.shape, 0)
    is_fgate = (sub >= GATE_TILE) & (sub < GATE_TILE + ML_HEADS)
    grow_ref[...] = jnp.where(is_fgate, _log_sigmoid(gt), gt)


def _cast_blocks(rows, steps):
    per = 1
    while steps % per or rows % (steps // per) or (rows // (steps // per)) % BF16_ROWS:
        per += 1
        assert per <= steps, (rows, steps)
    return rows // (steps // per), per


def _inproj(x2, w_bf, b_row, wgt, bgt, lb_logits, to_cast, tm):
    t, d = x2.shape
    grid = (t // tm,)
    row = lambda i: (i, 0)
    const = lambda i: (0, 0)
    cast_specs = []
    for a in to_cast:
        block_rows, per = _cast_blocks(a.shape[0], grid[0])
        cast_specs.append(pl.BlockSpec((block_rows, a.shape[1]), lambda i, per=per: (i // per, 0)))
    out_shapes = (
        jax.ShapeDtypeStruct((t, HG_WIDTH), F32),
        jax.ShapeDtypeStruct((t, HG_WIDTH), BF16),
        jax.ShapeDtypeStruct((t, HG_WIDTH), BF16),
        jax.ShapeDtypeStruct((t, HG_WIDTH), BF16),
        jax.ShapeDtypeStruct((t, HG_WIDTH), BF16),
        jax.ShapeDtypeStruct((t, ML_QK_WIDTH), BF16),
        jax.ShapeDtypeStruct((t, ML_QK_WIDTH), BF16),
        jax.ShapeDtypeStruct((t, ML_WIDTH), BF16),
        jax.ShapeDtypeStruct((t, ML_WIDTH), BF16),
        jax.ShapeDtypeStruct((GATE_ROWS, t), F32),
    ) + tuple(jax.ShapeDtypeStruct(a.shape, BF16) for a in to_cast)
    out_specs = (
        pl.BlockSpec((tm, HG_WIDTH), row), pl.BlockSpec((tm, HG_WIDTH), row),
        pl.BlockSpec((tm, HG_WIDTH), row), pl.BlockSpec((tm, HG_WIDTH), row),
        pl.BlockSpec((tm, HG_WIDTH), row), pl.BlockSpec((tm, ML_QK_WIDTH), row),
        pl.BlockSpec((tm, ML_QK_WIDTH), row), pl.BlockSpec((tm, ML_WIDTH), row),
        pl.BlockSpec((tm, ML_WIDTH), row),
        pl.BlockSpec((GATE_ROWS, tm), lambda i: (0, i)),
    ) + tuple(cast_specs)
    in_specs = [
        pl.BlockSpec((tm, d), row),
        pl.BlockSpec((d, OFF_GATES), const, pipeline_mode=pl.Buffered(1)),
        pl.BlockSpec((1, OFF_GATES), const),
        pl.BlockSpec(wgt.shape, const),
        pl.BlockSpec(bgt.shape, const),
        pl.BlockSpec(lb_logits.shape, const),
    ] + cast_specs
    outs = pl.pallas_call(
        functools.partial(_inproj_kernel, n_cast=len(to_cast)),
        grid=grid, in_specs=in_specs, out_specs=out_specs, out_shape=out_shapes,
        compiler_params=pltpu.CompilerParams(dimension_semantics=("arbitrary",),
                                             vmem_limit_bytes=VMEM_LIMIT),
        name="inproj",
    )(x2, w_bf, b_row, wgt, bgt, lb_logits, *to_cast)
    return outs[:N_MIX_IN], outs[N_MIX_IN:]


def _chunk_tril(ts):
    idx = np.arange(ts)
    return jnp.asarray((idx[:, None] >= idx[None, :]) & (idx[:, None] // CHUNK == idx[None, :] // CHUNK), BF16)


def _hgrn2_phases(logf_ref, q_ref, k_ref, v_ref, og_ref, gn_ref, tril_ref, o_ref,
                  st_ref, b_scr, c_scr, n_chunks):
    row = lax.broadcasted_iota(jnp.int32, (CHUNK, CHUNK), 0)
    col = lax.broadcasted_iota(jnp.int32, (CHUNK, CHUNK), 1)
    exact_mask = (col <= row) & (row // EX == col // EX)
    pair_mask = row // SUB == col // SUB
    key_pos = col % EX
    env = {}

    def blocks(vals):
        return _rows([jnp.broadcast_to(jnp.asarray(x, F32), (EX, HG_WIDTH)) for x in vals])

    def setup():
        b_scr[...] = _cumsum_rows(tril_ref[...], logf_ref[...]) * LOG2E
        c_scr[...] = b_scr[...] - jnp.log2(k_ref[...].astype(F32))
        env["states"] = [st_ref[h] for h in range(HG_HEADS)]

    def scores(c):
        r0 = c * CHUNK
        rows = pl.ds(r0, CHUNK)
        b2 = b_scr[rows, :]
        c2 = c_scr[rows, :]
        qf = q_ref[rows, :].astype(F32)

        bd = [jnp.zeros((1, HG_WIDTH), F32)]
        bd += [b_scr[pl.ds(r0 + EX * m - 1, 1), :] for m in range(1, N_EX + 1)]
        per_sub = EX_PER_SUB
        sub_start = blocks([bd[(m // per_sub) * per_sub] for m in range(N_EX)])
        q_st = qf * jnp.exp2(b2 - sub_start)
        q_in = (q_st * jnp.exp2(sub_start)).astype(BF16)
        k_dec = jnp.exp2(bd[N_EX] - c2).astype(BF16)
        chunk_decay = jnp.exp2(bd[N_EX])
        q_half = qf * jnp.exp2(b2 - blocks([bd[m] if m % per_sub else jnp.inf for m in range(N_EX)]))
        k_half = jnp.exp2(blocks([-jnp.inf if m % per_sub else bd[m + 1] for m in range(N_EX)]) - c2)

        a_heads = []
        for h in range(HG_HEADS):
            hs = slice(h * HG_DK, (h + 1) * HG_DK)
            exact = jnp.zeros((CHUNK, CHUNK), F32)
            for j in range(EX):
                c_j = _rows([jnp.broadcast_to(c_scr[pl.ds(r0 + m * EX + j, 1), hs], (EX, HG_DK))
                            for m in range(N_EX)])
                prod = qf[:, hs] * jnp.exp2(b2[:, hs] - c_j)
                exact = jnp.where(key_pos == j, jnp.sum(prod, axis=-1, keepdims=True), exact)
            half = _dot_nt(q_half[:, hs].astype(BF16), k_half[:, hs].astype(BF16))
            off = [jnp.zeros((SUB, CHUNK), F32)]
            for i in range(1, N_SUB):
                k_i = jnp.exp2(bd[i * per_sub][:, hs] - c2[0:i * SUB, hs])
                k_i = _rows([k_i, jnp.zeros((CHUNK - i * SUB, HG_DK), F32)])
                off.append(_dot_nt(q_st[i * SUB:(i + 1) * SUB, hs].astype(BF16), k_i.astype(BF16)))
            a = jnp.where(exact_mask, exact, jnp.where(pair_mask, half, 0.0)) + _rows(off)
            a_heads.append(a.astype(BF16))
        env[c] = (a_heads, q_in, k_dec, chunk_decay)

    def outputs(c):
        a_heads, q_in, k_dec, chunk_decay = env.pop(c)
        states = env["states"]
        rows = pl.ds(c * CHUNK, CHUNK)
        v = v_ref[rows, :]
        outs = []
        for h in range(HG_HEADS):
            hs = slice(h * HG_DK, (h + 1) * HG_DK)
            st = states[h]
            o_h = _dot(a_heads[h], v[:, hs]) + _dot_nt(q_in[:, hs], st.astype(BF16))
            states[h] = st * chunk_decay[:, hs] + _dot_tn(v[:, hs], k_dec[:, hs])
            outs.append(_head_rms(o_h, HG_DV))
        o = jnp.concatenate(outs, axis=-1)
        o_ref[rows, :] = (o * gn_ref[...] * og_ref[rows, :].astype(F32)).astype(o_ref.dtype)

    def finish():
        for h in range(HG_HEADS):
            st_ref[h] = env["states"][h]

    phases = [setup, functools.partial(scores, 0)]
    for c in range(n_chunks):
        step = [functools.partial(scores, c + 1)] if c + 1 < n_chunks else []
        step.append(functools.partial(outputs, c))
        phases.append(lambda step=step: [f() for f in step])
    phases.append(finish)
    return phases


CONV_PAD = 8
SEL_ROWS = 128
GRP_M, GRP_WI, GRP_EN = 0, 32, 64
ML_AUG = 2 * ML_DV


def _mlstm_constants(tl):
    ident = np.eye(tl, dtype=np.float32)
    triu = np.triu(np.ones((tl, tl), np.float32))
    bias = np.where(np.tril(np.ones((tl, tl), bool)), 0.0, -np.inf).astype(np.float32)
    sel_sum = np.zeros((SEL_ROWS, LANE), np.float32)
    for q, grp in enumerate((GRP_M, GRP_WI, GRP_EN)):
        for h in range(ML_HEADS):
            for k in range(3):
                sel_sum[grp + GATE_TILE * k + h, q * GATE_TILE + h] = 1.0
    return (jnp.asarray(ident, BF16), jnp.asarray(triu, BF16), jnp.asarray(bias), jnp.asarray(sel_sum, BF16))


def _mlstm_phases(mq_ref, mk_ref, mv_ref, og_ref, grow_ref, cw_ref, cb_ref, gn_ref,
                  ident_ref, triu_ref, bias_ref, selsum_ref, o_ref,
                  cbuf, c_ref, m_ref, tl):
    env = {}

    def scalars():
        cbuf[CONV_PAD:CONV_PAD + tl, 0:ML_QK_WIDTH] = mq_ref[...].astype(F32)
        cbuf[CONV_PAD:CONV_PAD + tl, ML_QK_WIDTH:] = mk_ref[...].astype(F32)
        acc = cb_ref[...] + cw_ref[0:1, :] * cbuf[pl.ds(CONV_PAD - (CONV_K - 1), tl), :]
        for tap in range(1, CONV_K):
            acc = acc + cw_ref[tap:tap + 1, :] * cbuf[pl.ds(CONV_PAD - (CONV_K - 1) + tap, tl), :]
        qk = _silu(acc)
        cbuf[0:CONV_PAD, :] = cbuf[tl:tl + CONV_PAD, :]
        env["q"] = (qk[:, 0:ML_QK_WIDTH] * (ML_DQK ** -0.5)).astype(BF16)
        k = qk[:, ML_QK_WIDTH:]
        env["k"] = k.astype(BF16)
        env["k_t"] = k.T

        gates = grow_ref[...]
        i_g = gates[0:GATE_TILE, :]
        hi, mid, lo = _split3(gates[GATE_TILE:, :])
        triu = triu_ref[...]
        part = _dot(_rows([hi, mid]), triu)
        g = part[0:GATE_TILE, :] + part[GATE_TILE:, :] + _dot(_rows([lo, lo]), triu)[0:GATE_TILE, :]
        u = i_g - g
        lane = lax.broadcasted_iota(jnp.int32, (GATE_TILE, tl), 1)
        cm = u
        shift = 1
        while shift < tl:
            cm = jnp.maximum(cm, jnp.where(lane >= shift, pltpu.roll(cm, shift, axis=1), -jnp.inf))
            shift *= 2
        m_prev = m_ref[...]
        m_run = jnp.maximum(m_prev, cm)
        w_inter = jnp.exp(m_prev - m_run)
        e_negm = jnp.exp(-(g + m_run))
        m_last = jnp.broadcast_to(m_run[:, tl - 1:tl], (GATE_TILE, tl))
        g_last = jnp.broadcast_to(g[:, tl - 1:tl], (GATE_TILE, tl))
        env["w_s"] = jnp.exp(u - m_last)
        env["w_old"] = jnp.exp(m_prev - m_last)
        m_ref[...] = g_last + m_last

        env["u"] = u
        zeros8 = jnp.zeros((GATE_TILE, tl), F32)
        env["pack"] = _rows(_split3_f32(m_run) + [zeros8] + _split3_f32(w_inter) + [zeros8]
                            + _split3_f32(e_negm) + [zeros8] * 5).astype(BF16)

    def copies():
        pack_t = _dot_nt(ident_ref[...], env["pack"]).astype(BF16)
        env["cols"] = _dot(pack_t, selsum_ref[...])

    def head(h):
        qs = slice(h * ML_DQK, (h + 1) * ML_DQK)
        vs = slice(h * ML_DV, (h + 1) * ML_DV)
        q_bf, k_bf, cols = env["q"], env["k"], env["cols"]
        m_col = cols[:, h:h + 1]
        wi_rep = jnp.broadcast_to(cols[:, GATE_TILE + h:GATE_TILE + h + 1], (tl, ML_DV))
        en_rep = jnp.broadcast_to(cols[:, 2 * GATE_TILE + h:2 * GATE_TILE + h + 1], (tl, ML_DV))
        w_intra = jnp.exp((env["u"][h:h + 1, :] - m_col) + bias_ref[...])
        p = (_dot_nt(q_bf[:, qs], k_bf[:, qs]) * w_intra).astype(BF16)
        v_aug = jnp.concatenate([mv_ref[:, vs], jnp.ones((tl, ML_DV), BF16)], axis=1)
        c_aug = c_ref[h]
        intra = _dot(p, v_aug)
        inter = _dot(q_bf[:, qs], c_aug.astype(BF16))
        num = intra[:, 0:ML_DV] + wi_rep * inter[:, 0:ML_DV]
        den = intra[:, ML_DV:] + wi_rep * inter[:, ML_DV:]
        hh = num / jnp.maximum(jnp.abs(den), en_rep)
        kw_t = (env["k_t"][qs, :] * env["w_s"][h:h + 1, :]).astype(BF16)
        c_ref[h] = env["w_old"][h:h + 1, 0:ML_AUG] * c_aug + _dot(kw_t, v_aug)
        o_ref[:, vs] = (_head_rms(hh, ML_DV) * gn_ref[:, vs] * og_ref[:, vs].astype(F32)).astype(o_ref.dtype)

    return [scalars, copies] + [functools.partial(head, h) for h in range(ML_HEADS)]


FF_TILE = 256


def _layer_norm(x, g, b):
    mu = jnp.mean(x, axis=-1, keepdims=True)
    xc = x - mu
    var = jnp.mean(xc * xc, axis=-1, keepdims=True)
    return xc * lax.rsqrt(var + LN_EPS) * g + b


def _norm_phases(ohg_ref, oml_ref, x_ref, wo_ref, ln1g_ref, ln1b_ref, h1_ref, alpha):
    env = {}

    def project():
        env["mix"] = _dot(ohg_ref[...], wo_ref[0:HG_WIDTH, :]) + _dot(oml_ref[...], wo_ref[HG_WIDTH:, :])

    def norm():
        h1_ref[...] = _layer_norm(alpha * x_ref[...] + env["mix"], ln1g_ref[...], ln1b_ref[...])

    return project, norm


def _ffn_phases(h1_ref, p_ref, wg_ref, wu_ref, wd_ref, ln2g_ref, ln2b_ref, wpp_ref, wpg_ref, bpg_ref,
                out_ref, act_scr, ffn_scr, alpha, d_ff):
    env = {}
    d = h1_ref.shape[-1]

    def hidden(lo):
        if lo == 0:
            env["h1b"] = h1_ref[...].astype(BF16)
        h1b = env["h1b"]
        gate = _dot(h1b, wg_ref[:, lo:lo + FF_TILE])
        up = _dot(h1b, wu_ref[:, lo:lo + FF_TILE])
        act_scr[:, lo:lo + FF_TILE] = (_silu(gate) * up).astype(BF16)

    def down(lo):
        ffn_scr[:, lo:lo + FF_TILE] = _dot(act_scr[...], wd_ref[:, lo:lo + FF_TILE])

    def norm():
        env["pemb"] = _dot(p_ref[...].astype(BF16), wpp_ref[...])
        env["h2"] = _layer_norm(alpha * h1_ref[...] + ffn_scr[...], ln2g_ref[...], ln2b_ref[...])

    def embed():
        h2 = env["h2"]
        pgate = _sigmoid(_dot(h2.astype(BF16), wpg_ref[...]) + bpg_ref[...])
        out_ref[...] = h2 + pgate * env["pemb"]

    tiles = ([functools.partial(hidden, lo) for lo in range(0, d_ff, FF_TILE)]
             + [functools.partial(down, lo) for lo in range(0, d, FF_TILE)])
    return tiles, norm, embed


N_MIX_CONST = 9
N_TAIL_CONST = 11


def _mixer_groups(hg, ml):
    hg_setup, *hg_chunks, hg_finish = hg
    ml_scalars, ml_copies, *ml_heads = ml
    groups = [[hg_setup, ml_scalars], hg_chunks[:1], [ml_copies] + hg_chunks[1:2]]
    rest_hg, rest_ml = hg_chunks[2:], ml_heads
    for j in range(max(len(rest_hg), len(rest_ml))):
        groups.append(rest_ml[j:j + 1] + rest_hg[j:j + 1])
    groups[-1].append(hg_finish)
    return groups


def _issue_order(ffn, groups):
    order, done = [], 0
    for j, f in enumerate(ffn):
        order.append(f)
        upto = round((j + 1) / len(ffn) * len(groups))
        for group in groups[done:upto]:
            order.extend(group)
        done = max(done, upto)
    return order


def _fused_kernel(*refs, tl, sub, tiles_per_seq, n_tiles, alpha, d_ff):
    it = iter(refs)
    take = lambda n: [next(it) for _ in range(n)]
    (logf_ref, hq_ref, hk_ref, hv_ref, hg_ref, mq_ref, mk_ref, mv_ref, mo_ref, grow_ref) = take(N_MIX_IN)
    (hgn_ref, tril_ref, cw_ref, cb_ref, mgn_ref,
     ident_ref, triu_ref, bias_ref, selsum_ref) = take(N_MIX_CONST)
    x_ref, p_ref = take(2)
    (wo_ref, ln1g_ref, ln1b_ref, wg_ref, wu_ref, wd_ref,
     ln2g_ref, ln2b_ref, wpp_ref, wpg_ref, bpg_ref) = take(N_TAIL_CONST)
    (out_ref,) = take(1)
    st_ref, b_scr, c_scr, cbuf, c_ref, m_ref, ohg_scr, oml_scr, h1_scr, act_scr, ffn_scr = take(11)

    g = pl.program_id(0)

    @pl.when(g % tiles_per_seq == 0)
    def _():
        st_ref[...] = jnp.zeros_like(st_ref)
        cbuf[0:CONV_PAD, :] = jnp.zeros((CONV_PAD, 2 * ML_QK_WIDTH), F32)
        c_ref[...] = jnp.zeros_like(c_ref)
        m_ref[...] = jnp.zeros_like(m_ref)

    cur = g % 2
    prev = 1 - cur

    def step(with_mixers, with_ffn):
        groups = []
        for off in range(0, tl, sub) if with_mixers else ():
            rows = pl.ds(off, sub)
            at = lambda ref: ref.at[rows]
            hg_phases = _hgrn2_phases(at(logf_ref), at(hq_ref), at(hk_ref), at(hv_ref), at(hg_ref), hgn_ref,
                                      tril_ref, ohg_scr.at[rows], st_ref, b_scr, c_scr, sub // CHUNK)
            ml_phases = _mlstm_phases(at(mq_ref), at(mk_ref), at(mv_ref), at(mo_ref), grow_ref.at[:, rows],
                                      cw_ref, cb_ref, mgn_ref, ident_ref, triu_ref, bias_ref, selsum_ref,
                                      oml_scr.at[rows], cbuf, c_ref, m_ref, sub)
            groups += _mixer_groups(hg_phases, ml_phases)
        project, norm1 = _norm_phases(ohg_scr, oml_scr, x_ref, wo_ref, ln1g_ref, ln1b_ref, h1_scr.at[cur], alpha)
        ffn, norm2, embed = _ffn_phases(h1_scr.at[prev], p_ref, wg_ref, wu_ref, wd_ref, ln2g_ref, ln2b_ref,
                                        wpp_ref, wpg_ref, bpg_ref, out_ref, act_scr, ffn_scr, alpha, d_ff)
        if with_mixers and with_ffn:
            order = _issue_order(ffn, groups) + [project, norm2, embed, norm1]
        elif with_mixers:
            order = [f for group in groups for f in group] + [project, norm1]
        else:
            order = ffn + [norm2, embed]
        for f in order:
            f()

    pl.when(g == 0)(functools.partial(step, True, False))
    pl.when((g > 0) & (g < n_tiles))(functools.partial(step, True, True))
    pl.when(g == n_tiles)(functools.partial(step, False, True))


def _fused(mix_in, hgn, conv_w, conv_b, mgn, x2, p2, tail_consts, batch, seq, alpha):
    t, d = x2.shape
    tl, sub = TAIL_TILE, MIX_TILE
    assert seq % tl == 0 and tl % sub == 0 and sub % CHUNK == 0 and sub >= ML_AUG
    tiles_per_seq = seq // tl
    n_tiles = t // tl
    d_ff = tail_consts[3].shape[1]
    assert d_ff % FF_TILE == 0
    mix_consts = (hgn, _chunk_tril(sub), conv_w, conv_b, mgn) + _mlstm_constants(sub)
    assert len(mix_in) == N_MIX_IN and len(mix_consts) == N_MIX_CONST and len(tail_consts) == N_TAIL_CONST

    def tile(lag):
        return lambda g: jnp.clip(g - lag, 0, n_tiles - 1)

    def const(a):
        return pl.BlockSpec(a.shape, lambda g: (0,) * a.ndim, pipeline_mode=pl.Buffered(1))

    mix_specs = [pl.BlockSpec((tl, a.shape[1]), lambda g: (tile(0)(g), 0)) for a in mix_in[:-1]]
    mix_specs.append(pl.BlockSpec((GATE_ROWS, tl), lambda g: (0, tile(0)(g))))
    in_specs = (mix_specs + [const(a) for a in mix_consts]
                + [pl.BlockSpec((tl, d), lambda g: (tile(0)(g), 0)),
                   pl.BlockSpec((tl, p2.shape[1]), lambda g: (tile(1)(g), 0))]
                + [const(a) for a in tail_consts])
    return pl.pallas_call(
        functools.partial(_fused_kernel, tl=tl, sub=sub, tiles_per_seq=tiles_per_seq, n_tiles=n_tiles,
                          alpha=alpha, d_ff=d_ff),
        grid=(n_tiles + 1,),
        in_specs=in_specs,
        out_specs=pl.BlockSpec((tl, d), lambda g: (tile(1)(g), 0)),
        out_shape=jax.ShapeDtypeStruct((t, d), F32),
        scratch_shapes=[pltpu.VMEM((HG_HEADS, HG_DV, HG_DK), F32),
                        pltpu.VMEM((sub, HG_WIDTH), F32),
                        pltpu.VMEM((sub, HG_WIDTH), F32),
                        pltpu.VMEM((sub + CONV_PAD, 2 * ML_QK_WIDTH), F32),
                        pltpu.VMEM((ML_HEADS, ML_DQK, ML_AUG), F32),
                        pltpu.VMEM((GATE_TILE, sub), F32),
                        pltpu.VMEM((tl, HG_WIDTH), BF16),
                        pltpu.VMEM((tl, ML_WIDTH), BF16),
                        pltpu.VMEM((2, tl, d), F32),
                        pltpu.VMEM((tl, d_ff), BF16),
                        pltpu.VMEM((tl, d), F32)],
        compiler_params=pltpu.CompilerParams(dimension_semantics=("arbitrary",),
                                             vmem_limit_bytes=VMEM_LIMIT),
        name="mix_tail",
    )(*mix_in, *mix_consts, x2, p2, *tail_consts)


def kernel(x, p, w_in, b_in, hg_lb_logits, ml_conv_w, ml_conv_b, hg_norm_g, ml_norm_g, w_out, ln1_g, ln1_b,
           w_ffn_gate, w_ffn_up, w_ffn_down, ln2_g, ln2_b, ple_w_proj, ple_w_gate, ple_b_gate):
    batch, seq, d = x.shape
    depth = w_in.shape[0]
    t = batch * seq
    alpha = float((2 * depth) ** 0.25)
    tm = INPROJ_TILE
    assert t % tm == 0
    assert w_in.shape[2] == OFF_GATES + 2 * ML_HEADS
    assert depth == 1, "lower-bound cumsum is specialised to a single layer"

    x2 = x.reshape(t, d)
    for i in range(depth):
        w_i = w_in[i]
        w_bf = w_i.astype(BF16)
        b_row = b_in[i].reshape(1, -1)
        wg_t = w_i[:, OFF_GATES:].T
        gate_pad = ((0, GATE_TILE - ML_HEADS), (0, 0))
        wgt = jnp.concatenate([jnp.pad(wg_t[:ML_HEADS], gate_pad), jnp.pad(wg_t[ML_HEADS:], gate_pad)]).astype(BF16)
        bg = b_in[i, OFF_GATES:].reshape(2 * ML_HEADS, 1)
        bgt = jnp.concatenate([jnp.pad(bg[:ML_HEADS], gate_pad), jnp.pad(bg[ML_HEADS:], gate_pad)])

        later = (w_out[i], w_ffn_gate[i], w_ffn_up[i], w_ffn_down[i], ple_w_proj[i], ple_w_gate[i])
        mix_in, (wo, wg, wu, wd, wpp, wpg) = _inproj(x2, w_bf, b_row, wgt, bgt, hg_lb_logits, later, tm)
        tail_consts = (wo, ln1_g[i].reshape(1, d), ln1_b[i].reshape(1, d), wg, wu, wd,
                       ln2_g[i].reshape(1, d), ln2_b[i].reshape(1, d), wpp, wpg, ple_b_gate[i].reshape(1, d))
        x2 = _fused(mix_in, hg_norm_g[i].reshape(1, HG_WIDTH), ml_conv_w[i], ml_conv_b[i].reshape(1, -1),
                    ml_norm_g[i].reshape(1, ML_WIDTH), x2, p[i].reshape(t, -1), tail_consts, batch, seq, alpha)
    return x2.reshape(batch, seq, d)
```

```python
import functools

import numpy as np
import jax
import jax.numpy as jnp
from jax import lax
from jax.experimental import pallas as pl
from jax.experimental.pallas import tpu as pltpu

F32 = jnp.float32
BF16 = jnp.bfloat16

CHUNK = 64
SUB = 16
N_SUB = CHUNK // SUB
EX = 8
N_EX = CHUNK // EX
EX_PER_SUB = SUB // EX
LOG2E = 1.4426950408889634
HG_HEADS = 4
HG_DK = 128
HG_DV = 128
HG_WIDTH = HG_HEADS * HG_DV
ML_HEADS = 4
ML_DQK = 64
ML_DV = 128
ML_WIDTH = ML_HEADS * ML_DV
ML_QK_WIDTH = ML_HEADS * ML_DQK
CONV_K = 4
LN_EPS = 1e-5
RMS_EPS = 1e-6

OFF_HQ = 0
OFF_HF = OFF_HQ + HG_HEADS * HG_DK
OFF_HV = OFF_HF + HG_HEADS * HG_DK
OFF_HG = OFF_HV + HG_WIDTH
OFF_MQ = OFF_HG + HG_WIDTH
OFF_MK = OFF_MQ + ML_QK_WIDTH
OFF_MV = OFF_MK + ML_QK_WIDTH
OFF_MO = OFF_MV + ML_WIDTH
OFF_GATES = OFF_MO + ML_WIDTH
LANE = 128
GATE_TILE = 8
BF16_ROWS = 16
GATE_ROWS = 2 * GATE_TILE
PROJ_PIECE = 256
N_MIX_IN = 10

VMEM_LIMIT = 60 * 1024 * 1024
MIX_TILE = 256
TAIL_TILE = 512
INPROJ_TILE = 1024


def _sigmoid(x):
    return 0.5 * jnp.tanh(0.5 * x) + 0.5


def _silu(x):
    return x * _sigmoid(x)


def _log_sigmoid(x):
    return jnp.minimum(x, 0.0) - jnp.log(1.0 + jnp.exp(-jnp.abs(x)))


def _split3(x):
    hi = x.astype(BF16)
    r1 = x - hi.astype(F32)
    mid = r1.astype(BF16)
    lo = (r1 - mid.astype(F32)).astype(BF16)
    return hi, mid, lo


def _split3_f32(x):
    return [s.astype(F32) for s in _split3(x)]


def _dot(a, b):
    return jnp.dot(a, b, preferred_element_type=F32)


def _dot_nt(a, b):
    return lax.dot_general(a, b, (((1,), (1,)), ((), ())), preferred_element_type=F32)


def _dot_tn(a, b):
    return lax.dot_general(a, b, (((0,), (0,)), ((), ())), preferred_element_type=F32)


def _rows(blocks):
    return jnp.concatenate(blocks, axis=0)


def _cumsum_rows(tril_bf, x):
    hi = x.astype(BF16)
    lo = (x - hi.astype(F32)).astype(BF16)
    return _dot(tril_bf, hi) + _dot(tril_bf, lo)


def _head_rms(o, width):
    ms = jnp.sum(o * o, axis=-1, keepdims=True) * (1.0 / width)
    return o * lax.rsqrt(ms + RMS_EPS)


def _inproj_kernel(x_ref, w_ref, b_ref, wgt_ref, bgt_ref, lbl_ref, *refs, n_cast):
    cast_src, refs = refs[:n_cast], refs[n_cast:]
    (logf_ref, hq_ref, hk_ref, hv_ref, hg_ref, mq_ref, mk_ref, mv_ref, mo_ref, grow_ref) = refs[:N_MIX_IN]
    for src, dst in zip(cast_src, refs[N_MIX_IN:]):
        dst[...] = src[...].astype(BF16)

    xb = x_ref[...].astype(BF16)

    logits = lbl_ref[...]
    mx = jnp.max(logits, axis=0, keepdims=True)
    ex = jnp.exp(logits - mx)
    den = jnp.sum(ex, axis=0, keepdims=True)
    lb = ex[0:1, :] / den
    one_m_lb = (den - ex[0:1, :]) / den

    def plain(out_ref):
        def store(u, cols):
            out_ref[:, cols] = u.astype(BF16)
        return store

    def act(out_ref, fn):
        def store(u, cols):
            out_ref[:, cols] = fn(u).astype(BF16)
        return store

    def forget(u, cols):
        sig = _sigmoid(u)
        logf_ref[:, cols] = jnp.log(lb[:, cols] + one_m_lb[:, cols] * sig)
        hk_ref[:, cols] = (one_m_lb[:, cols] * (1.0 - sig)).astype(BF16)

    n = PROJ_PIECE
    pieces = {"hq": (OFF_HQ, HG_WIDTH, act(hq_ref, _silu)), "hf": (OFF_HF, HG_WIDTH, forget),
              "hv": (OFF_HV, HG_WIDTH, plain(hv_ref)), "hg": (OFF_HG, HG_WIDTH, act(hg_ref, _silu)),
              "mq": (OFF_MQ, ML_QK_WIDTH, plain(mq_ref)), "mk": (OFF_MK, ML_QK_WIDTH, plain(mk_ref)),
              "mv": (OFF_MV, ML_WIDTH, plain(mv_ref)), "mo": (OFF_MO, ML_WIDTH, act(mo_ref, _sigmoid))}
    order = ["hq", "hv", "hf", "mq", "hg", "hv", "mo", "mk", "hq", "mv", "hf", "mv", "hg", "mo"]
    taken = {name: 0 for name in pieces}
    for name in order:
        off, width, store = pieces[name]
        lo = taken[name]
        taken[name] = lo + n
        store(_dot(xb, w_ref[:, off + lo:off + lo + n]) + b_ref[:, off + lo:off + lo + n], slice(lo, lo + n))
    assert all(taken[name] == pieces[name][1] for name in pieces)

    gt = _dot_nt(wgt_ref[...], xb) + bgt_ref[...]
    sub = lax.broadcasted_iota(jnp.int32, gt.shape, 0)
    is_fgate = (sub >= GATE_TILE) & (sub < GATE_TILE + ML_HEADS)
    grow_ref[...] = jnp.where(is_fgate, _log_sigmoid(gt), gt)


def _cast_blocks(rows, steps):
    per = 1
    while steps % per or rows % (steps // per) or (rows // (steps // per)) % BF16_ROWS:
        per += 1
        assert per <= steps, (rows, steps)
    return rows // (steps // per), per


def _inproj(x2, w_bf, b_row, wgt, bgt, lb_logits, to_cast, tm):
    t, d = x2.shape
    grid = (t // tm,)
    row = lambda i: (i, 0)
    const = lambda i: (0, 0)
    cast_specs = []
    for a in to_cast:
        block_rows, per = _cast_blocks(a.shape[0], grid[0])
        cast_specs.append(pl.BlockSpec((block_rows, a.shape[1]), lambda i, per=per: (i // per, 0)))
    out_shapes = (
        jax.ShapeDtypeStruct((t, HG_WIDTH), F32),
        jax.ShapeDtypeStruct((t, HG_WIDTH), BF16),
        jax.ShapeDtypeStruct((t, HG_WIDTH), BF16),
        jax.ShapeDtypeStruct((t, HG_WIDTH), BF16),
        jax.ShapeDtypeStruct((t, HG_WIDTH), BF16),
        jax.ShapeDtypeStruct((t, ML_QK_WIDTH), BF16),
        jax.ShapeDtypeStruct((t, ML_QK_WIDTH), BF16),
        jax.ShapeDtypeStruct((t, ML_WIDTH), BF16),
        jax.ShapeDtypeStruct((t, ML_WIDTH), BF16),
        jax.ShapeDtypeStruct((GATE_ROWS, t), F32),
    ) + tuple(jax.ShapeDtypeStruct(a.shape, BF16) for a in to_cast)
    out_specs = (
        pl.BlockSpec((tm, HG_WIDTH), row), pl.BlockSpec((tm, HG_WIDTH), row),
        pl.BlockSpec((tm, HG_WIDTH), row), pl.BlockSpec((tm, HG_WIDTH), row),
        pl.BlockSpec((tm, HG_WIDTH), row), pl.BlockSpec((tm, ML_QK_WIDTH), row),
        pl.BlockSpec((tm, ML_QK_WIDTH), row), pl.BlockSpec((tm, ML_WIDTH), row),
        pl.BlockSpec((tm, ML_WIDTH), row),
        pl.BlockSpec((GATE_ROWS, tm), lambda i: (0, i)),
    ) + tuple(cast_specs)
    in_specs = [
        pl.BlockSpec((tm, d), row),
        pl.BlockSpec((d, OFF_GATES), const, pipeline_mode=pl.Buffered(1)),
        pl.BlockSpec((1, OFF_GATES), const),
        pl.BlockSpec(wgt.shape, const),
        pl.BlockSpec(bgt.shape, const),
        pl.BlockSpec(lb_logits.shape, const),
    ] + cast_specs
    outs = pl.pallas_call(
        functools.partial(_inproj_kernel, n_cast=len(to_cast)),
        grid=grid, in_specs=in_specs, out_specs=out_specs, out_shape=out_shapes,
        compiler_params=pltpu.CompilerParams(dimension_semantics=("arbitrary",),
                                             vmem_limit_bytes=VMEM_LIMIT),
        name="inproj",
    )(x2, w_bf, b_row, wgt, bgt, lb_logits, *to_cast)
    return outs[:N_MIX_IN], outs[N_MIX_IN:]


def _chunk_tril(ts):
    idx = np.arange(ts)
    return jnp.asarray((idx[:, None] >= idx[None, :]) & (idx[:, None] // CHUNK == idx[None, :] // CHUNK), BF16)


def _hgrn2_phases(logf_ref, q_ref, k_ref, v_ref, og_ref, gn_ref, tril_ref, o_ref,
                  st_ref, b_scr, c_scr, n_chunks):
    row = lax.broadcasted_iota(jnp.int32, (CHUNK, CHUNK), 0)
    col = lax.broadcasted_iota(jnp.int32, (CHUNK, CHUNK), 1)
    exact_mask = (col <= row) & (row // EX == col // EX)
    pair_mask = row // SUB == col // SUB
    key_pos = col % EX
    env = {}

    def blocks(vals):
        return _rows([jnp.broadcast_to(jnp.asarray(x, F32), (EX, HG_WIDTH)) for x in vals])

    def setup():
        b_scr[...] = _cumsum_rows(tril_ref[...], logf_ref[...]) * LOG2E
        c_scr[...] = b_scr[...] - jnp.log2(k_ref[...].astype(F32))
        env["states"] = [st_ref[h] for h in range(HG_HEADS)]

    def scores(c):
        r0 = c * CHUNK
        rows = pl.ds(r0, CHUNK)
        b2 = b_scr[rows, :]
        c2 = c_scr[rows, :]
        qf = q_ref[rows, :].astype(F32)

        bd = [jnp.zeros((1, HG_WIDTH), F32)]
        bd += [b_scr[pl.ds(r0 + EX * m - 1, 1), :] for m in range(1, N_EX + 1)]
        per_sub = EX_PER_SUB
        sub_start = blocks([bd[(m // per_sub) * per_sub] for m in range(N_EX)])
        q_st = qf * jnp.exp2(b2 - sub_start)
        q_in = (q_st * jnp.exp2(sub_start)).astype(BF16)
        k_dec = jnp.exp2(bd[N_EX] - c2).astype(BF16)
        chunk_decay = jnp.exp2(bd[N_EX])
        q_half = qf * jnp.exp2(b2 - blocks([bd[m] if m % per_sub else jnp.inf for m in range(N_EX)]))
        k_half = jnp.exp2(blocks([-jnp.inf if m % per_sub else bd[m + 1] for m in range(N_EX)]) - c2)

        a_heads = []
        for h in range(HG_HEADS):
            hs = slice(h * HG_DK, (h + 1) * HG_DK)
            exact = jnp.zeros((CHUNK, CHUNK), F32)
            for j in range(EX):
                c_j = _rows([jnp.broadcast_to(c_scr[pl.ds(r0 + m * EX + j, 1), hs], (EX, HG_DK))
                            for m in range(N_EX)])
                prod = qf[:, hs] * jnp.exp2(b2[:, hs] - c_j)
                exact = jnp.where(key_pos == j, jnp.sum(prod, axis=-1, keepdims=True), exact)
            half = _dot_nt(q_half[:, hs].astype(BF16), k_half[:, hs].astype(BF16))
            off = [jnp.zeros((SUB, CHUNK), F32)]
            for i in range(1, N_SUB):
                k_i = jnp.exp2(bd[i * per_sub][:, hs] - c2[0:i * SUB, hs])
                k_i = _rows([k_i, jnp.zeros((CHUNK - i * SUB, HG_DK), F32)])
                off.append(_dot_nt(q_st[i * SUB:(i + 1) * SUB, hs].astype(BF16), k_i.astype(BF16)))
            a = jnp.where(exact_mask, exact, jnp.where(pair_mask, half, 0.0)) + _rows(off)
            a_heads.append(a.astype(BF16))
        env[c] = (a_heads, q_in, k_dec, chunk_decay)

    def outputs(c):
        a_heads, q_in, k_dec, chunk_decay = env.pop(c)
        states = env["states"]
        rows = pl.ds(c * CHUNK, CHUNK)
        v = v_ref[rows, :]
        outs = []
        for h in range(HG_HEADS):
            hs = slice(h * HG_DK, (h + 1) * HG_DK)
            st = states[h]
            o_h = _dot(a_heads[h], v[:, hs]) + _dot_nt(q_in[:, hs], st.astype(BF16))
            states[h] = st * chunk_decay[:, hs] + _dot_tn(v[:, hs], k_dec[:, hs])
            outs.append(_head_rms(o_h, HG_DV))
        o = jnp.concatenate(outs, axis=-1)
        o_ref[rows, :] = (o * gn_ref[...] * og_ref[rows, :].astype(F32)).astype(o_ref.dtype)

    def finish():
        for h in range(HG_HEADS):
            st_ref[h] = env["states"][h]

    phases = [setup, functools.partial(scores, 0)]
    for c in range(n_chunks):
        step = [functools.partial(scores, c + 1)] if c + 1 < n_chunks else []
        step.append(functools.partial(outputs, c))
        phases.append(lambda step=step: [f() for f in step])
    phases.append(finish)
    return phases


CONV_PAD = 8
SEL_ROWS = 128
GRP_M, GRP_WI, GRP_EN = 0, 32, 64
ML_AUG = 2 * ML_DV


def _mlstm_constants(tl):
    ident = np.eye(tl, dtype=np.float32)
    triu = np.triu(np.ones((tl, tl), np.float32))
    bias = np.where(np.tril(np.ones((tl, tl), bool)), 0.0, -np.inf).astype(np.float32)
    sel_sum = np.zeros((SEL_ROWS, LANE), np.float32)
    for q, grp in enumerate((GRP_M, GRP_WI, GRP_EN)):
        for h in range(ML_HEADS):
            for k in range(3):
                sel_sum[grp + GATE_TILE * k + h, q * GATE_TILE + h] = 1.0
    return (jnp.asarray(ident, BF16), jnp.asarray(triu, BF16), jnp.asarray(bias), jnp.asarray(sel_sum, BF16))


def _mlstm_phases(mq_ref, mk_ref, mv_ref, og_ref, grow_ref, cw_ref, cb_ref, gn_ref,
                  ident_ref, triu_ref, bias_ref, selsum_ref, o_ref,
                  cbuf, c_ref, m_ref, tl):
    env = {}

    def scalars():
        cbuf[CONV_PAD:CONV_PAD + tl, 0:ML_QK_WIDTH] = mq_ref[...].astype(F32)
        cbuf[CONV_PAD:CONV_PAD + tl, ML_QK_WIDTH:] = mk_ref[...].astype(F32)
        acc = cb_ref[...] + cw_ref[0:1, :] * cbuf[pl.ds(CONV_PAD - (CONV_K - 1), tl), :]
        for tap in range(1, CONV_K):
            acc = acc + cw_ref[tap:tap + 1, :] * cbuf[pl.ds(CONV_PAD - (CONV_K - 1) + tap, tl), :]
        qk = _silu(acc)
        cbuf[0:CONV_PAD, :] = cbuf[tl:tl + CONV_PAD, :]
        env["q"] = (qk[:, 0:ML_QK_WIDTH] * (ML_DQK ** -0.5)).astype(BF16)
        k = qk[:, ML_QK_WIDTH:]
        env["k"] = k.astype(BF16)
        env["k_t"] = k.T

        gates = grow_ref[...]
        i_g = gates[0:GATE_TILE, :]
        hi, mid, lo = _split3(gates[GATE_TILE:, :])
        triu = triu_ref[...]
        part = _dot(_rows([hi, mid]), triu)
        g = part[0:GATE_TILE, :] + part[GATE_TILE:, :] + _dot(_rows([lo, lo]), triu)[0:GATE_TILE, :]
        u = i_g - g
        lane = lax.broadcasted_iota(jnp.int32, (GATE_TILE, tl), 1)
        cm = u
        shift = 1
        while shift < tl:
            cm = jnp.maximum(cm, jnp.where(lane >= shift, pltpu.roll(cm, shift, axis=1), -jnp.inf))
            shift *= 2
        m_prev = m_ref[...]
        m_run = jnp.maximum(m_prev, cm)
        w_inter = jnp.exp(m_prev - m_run)
        e_negm = jnp.exp(-(g + m_run))
        m_last = jnp.broadcast_to(m_run[:, tl - 1:tl], (GATE_TILE, tl))
        g_last = jnp.broadcast_to(g[:, tl - 1:tl], (GATE_TILE, tl))
        env["w_s"] = jnp.exp(u - m_last)
        env["w_old"] = jnp.exp(m_prev - m_last)
        m_ref[...] = g_last + m_last

        env["u"] = u
        zeros8 = jnp.zeros((GATE_TILE, tl), F32)
        env["pack"] = _rows(_split3_f32(m_run) + [zeros8] + _split3_f32(w_inter) + [zeros8]
                            + _split3_f32(e_negm) + [zeros8] * 5).astype(BF16)

    def copies():
        pack_t = _dot_nt(ident_ref[...], env["pack"]).astype(BF16)
        env["cols"] = _dot(pack_t, selsum_ref[...])

    def head(h):
        qs = slice(h * ML_DQK, (h + 1) * ML_DQK)
        vs = slice(h * ML_DV, (h + 1) * ML_DV)
        q_bf, k_bf, cols = env["q"], env["k"], env["cols"]
        m_col = cols[:, h:h + 1]
        wi_rep = jnp.broadcast_to(cols[:, GATE_TILE + h:GATE_TILE + h + 1], (tl, ML_DV))
        en_rep = jnp.broadcast_to(cols[:, 2 * GATE_TILE + h:2 * GATE_TILE + h + 1], (tl, ML_DV))
        w_intra = jnp.exp((env["u"][h:h + 1, :] - m_col) + bias_ref[...])
        p = (_dot_nt(q_bf[:, qs], k_bf[:, qs]) * w_intra).astype(BF16)
        v_aug = jnp.concatenate([mv_ref[:, vs], jnp.ones((tl, ML_DV), BF16)], axis=1)
        c_aug = c_ref[h]
        intra = _dot(p, v_aug)
        inter = _dot(q_bf[:, qs], c_aug.astype(BF16))
        num = intra[:, 0:ML_DV] + wi_rep * inter[:, 0:ML_DV]
        den = intra[:, ML_DV:] + wi_rep * inter[:, ML_DV:]
        hh = num / jnp.maximum(jnp.abs(den), en_rep)
        kw_t = (env["k_t"][qs, :] * env["w_s"][h:h + 1, :]).astype(BF16)
        c_ref[h] = env["w_old"][h:h + 1, 0:ML_AUG] * c_aug + _dot(kw_t, v_aug)
        o_ref[:, vs] = (_head_rms(hh, ML_DV) * gn_ref[:, vs] * og_ref[:, vs].astype(F32)).astype(o_ref.dtype)

    return [scalars, copies] + [functools.partial(head, h) for h in range(ML_HEADS)]


FF_TILE = 256


def _layer_norm(x, g, b):
    mu = jnp.mean(x, axis=-1, keepdims=True)
    xc = x - mu
    var = jnp.mean(xc * xc, axis=-1, keepdims=True)
    return xc * lax.rsqrt(var + LN_EPS) * g + b


def _norm_phases(ohg_ref, oml_ref, x_ref, wo_ref, ln1g_ref, ln1b_ref, h1_ref, alpha):
    env = {}

    def project():
        env["mix"] = _dot(ohg_ref[...], wo_ref[0:HG_WIDTH, :]) + _dot(oml_ref[...], wo_ref[HG_WIDTH:, :])

    def norm():
        h1_ref[...] = _layer_norm(alpha * x_ref[...] + env["mix"], ln1g_ref[...], ln1b_ref[...])

    return project, norm


def _ffn_phases(h1_ref, p_ref, wg_ref, wu_ref, wd_ref, ln2g_ref, ln2b_ref, wpp_ref, wpg_ref, bpg_ref,
                out_ref, act_scr, ffn_scr, alpha, d_ff):
    env = {}
    d = h1_ref.shape[-1]

    def hidden(lo):
        if lo == 0:
            env["h1b"] = h1_ref[...].astype(BF16)
        h1b = env["h1b"]
        gate = _dot(h1b, wg_ref[:, lo:lo + FF_TILE])
        up = _dot(h1b, wu_ref[:, lo:lo + FF_TILE])
        act_scr[:, lo:lo + FF_TILE] = _silu(gate.astype(BF16)) * up.astype(BF16)

    def down(lo):
        ffn_scr[:, lo:lo + FF_TILE] = _dot(act_scr[...], wd_ref[:, lo:lo + FF_TILE])

    def norm():
        env["pemb"] = _dot(p_ref[...].astype(BF16), wpp_ref[...])
        env["h2"] = _layer_norm(alpha * h1_ref[...] + ffn_scr[...], ln2g_ref[...], ln2b_ref[...])

    def embed():
        h2 = env["h2"]
        pgate = _sigmoid(_dot(h2.astype(BF16), wpg_ref[...]) + bpg_ref[...])
        out_ref[...] = h2 + pgate * env["pemb"]

    tiles = ([functools.partial(hidden, lo) for lo in range(0, d_ff, FF_TILE)]
             + [functools.partial(down, lo) for lo in range(0, d, FF_TILE)])
    return tiles, norm, embed


N_MIX_CONST = 9
N_TAIL_CONST = 11


def _mixer_groups(hg, ml):
    hg_setup, *hg_chunks, hg_finish = hg
    ml_scalars, ml_copies, *ml_heads = ml
    groups = [[hg_setup, ml_scalars], hg_chunks[:1], [ml_copies] + hg_chunks[1:2]]
    rest_hg, rest_ml = hg_chunks[2:], ml_heads
    for j in range(max(len(rest_hg), len(rest_ml))):
        groups.append(rest_ml[j:j + 1] + rest_hg[j:j + 1])
    groups[-1].append(hg_finish)
    return groups


def _issue_order(ffn, groups):
    order, done = [], 0
    for j, f in enumerate(ffn):
        order.append(f)
        upto = round((j + 1) / len(ffn) * len(groups))
        for group in groups[done:upto]:
            order.extend(group)
        done = max(done, upto)
    return order


def _fused_kernel(*refs, tl, sub, tiles_per_seq, alpha, d_ff):
    it = iter(refs)
    take = lambda n: [next(it) for _ in range(n)]
    (logf_ref, hq_ref, hk_ref, hv_ref, hg_ref, mq_ref, mk_ref, mv_ref, mo_ref, grow_ref) = take(N_MIX_IN)
    (hgn_ref, tril_ref, cw_ref, cb_ref, mgn_ref,
     ident_ref, triu_ref, bias_ref, selsum_ref) = take(N_MIX_CONST)
    x_ref, p_ref = take(2)
    (wo_ref, ln1g_ref, ln1b_ref, wg_ref, wu_ref, wd_ref,
     ln2g_ref, ln2b_ref, wpp_ref, wpg_ref, bpg_ref) = take(N_TAIL_CONST)
    (out_ref,) = take(1)
    st_ref, b_scr, c_scr, cbuf, c_ref, m_ref, ohg_scr, oml_scr, h1_scr, act_scr, ffn_scr = take(11)

    g = pl.program_id(0)

    @pl.when(g == 0)
    def _():
        h1_scr[...] = jnp.zeros_like(h1_scr)

    @pl.when(g % tiles_per_seq == 0)
    def _():
        st_ref[...] = jnp.zeros_like(st_ref)
        cbuf[0:CONV_PAD, :] = jnp.zeros((CONV_PAD, 2 * ML_QK_WIDTH), F32)
        c_ref[...] = jnp.zeros_like(c_ref)
        m_ref[...] = jnp.zeros_like(m_ref)

    cur = g % 2
    prev = 1 - cur
    groups = []
    for off in range(0, tl, sub):
        rows = pl.ds(off, sub)
        at = lambda ref: ref.at[rows]
        hg_phases = _hgrn2_phases(at(logf_ref), at(hq_ref), at(hk_ref), at(hv_ref), at(hg_ref), hgn_ref,
                                  tril_ref, ohg_scr.at[rows], st_ref, b_scr, c_scr, sub // CHUNK)
        ml_phases = _mlstm_phases(at(mq_ref), at(mk_ref), at(mv_ref), at(mo_ref), grow_ref.at[:, rows],
                                  cw_ref, cb_ref, mgn_ref, ident_ref, triu_ref, bias_ref, selsum_ref,
                                  oml_scr.at[rows], cbuf, c_ref, m_ref, sub)
        groups += _mixer_groups(hg_phases, ml_phases)
    project, norm1 = _norm_phases(ohg_scr, oml_scr, x_ref, wo_ref, ln1g_ref, ln1b_ref, h1_scr.at[cur], alpha)
    ffn, norm2, embed = _ffn_phases(h1_scr.at[prev], p_ref, wg_ref, wu_ref, wd_ref, ln2g_ref, ln2b_ref,
                                    wpp_ref, wpg_ref, bpg_ref, out_ref, act_scr, ffn_scr, alpha, d_ff)
    for f in _issue_order(ffn, groups) + [project, norm2, embed, norm1]:
        f()


def _fused(mix_in, hgn, conv_w, conv_b, mgn, x2, p2, tail_consts, batch, seq, alpha):
    t, d = x2.shape
    tl, sub = TAIL_TILE, MIX_TILE
    assert seq % tl == 0 and tl % sub == 0 and sub % CHUNK == 0 and sub >= ML_AUG
    tiles_per_seq = seq // tl
    n_tiles = t // tl
    d_ff = tail_consts[3].shape[1]
    assert d_ff % FF_TILE == 0
    mix_consts = (hgn, _chunk_tril(sub), conv_w, conv_b, mgn) + _mlstm_constants(sub)
    assert len(mix_in) == N_MIX_IN and len(mix_consts) == N_MIX_CONST and len(tail_consts) == N_TAIL_CONST

    def tile(lag):
        return lambda g: jnp.clip(g - lag, 0, n_tiles - 1)

    def const(a):
        return pl.BlockSpec(a.shape, lambda g: (0,) * a.ndim, pipeline_mode=pl.Buffered(1))

    mix_specs = [pl.BlockSpec((tl, a.shape[1]), lambda g: (tile(0)(g), 0)) for a in mix_in[:-1]]
    mix_specs.append(pl.BlockSpec((GATE_ROWS, tl), lambda g: (0, tile(0)(g))))
    in_specs = (mix_specs + [const(a) for a in mix_consts]
                + [pl.BlockSpec((tl, d), lambda g: (tile(0)(g), 0)),
                   pl.BlockSpec((tl, p2.shape[1]), lambda g: (tile(1)(g), 0))]
                + [const(a) for a in tail_consts])
    return pl.pallas_call(
        functools.partial(_fused_kernel, tl=tl, sub=sub, tiles_per_seq=tiles_per_seq, alpha=alpha, d_ff=d_ff),
        grid=(n_tiles + 1,),
        in_specs=in_specs,
        out_specs=pl.BlockSpec((tl, d), lambda g: (tile(1)(g), 0)),
        out_shape=jax.ShapeDtypeStruct((t, d), F32),
        scratch_shapes=[pltpu.VMEM((HG_HEADS, HG_DV, HG_DK), F32),
                        pltpu.VMEM((sub, HG_WIDTH), F32),
                        pltpu.VMEM((sub, HG_WIDTH), F32),
                        pltpu.VMEM((sub + CONV_PAD, 2 * ML_QK_WIDTH), F32),
                        pltpu.VMEM((ML_HEADS, ML_DQK, ML_AUG), F32),
                        pltpu.VMEM((GATE_TILE, sub), F32),
                        pltpu.VMEM((tl, HG_WIDTH), BF16),
                        pltpu.VMEM((tl, ML_WIDTH), BF16),
                        pltpu.VMEM((2, tl, d), F32),
                        pltpu.VMEM((tl, d_ff), BF16),
                        pltpu.VMEM((tl, d), F32)],
        compiler_params=pltpu.CompilerParams(dimension_semantics=("arbitrary",),
                                             vmem_limit_bytes=VMEM_LIMIT),
        name="mix_tail",
    )(*mix_in, *mix_consts, x2, p2, *tail_consts)


def kernel(x, p, w_in, b_in, hg_lb_logits, ml_conv_w, ml_conv_b, hg_norm_g, ml_norm_g, w_out, ln1_g, ln1_b,
           w_ffn_gate, w_ffn_up, w_ffn_down, ln2_g, ln2_b, ple_w_proj, ple_w_gate, ple_b_gate):
    batch, seq, d = x.shape
    depth = w_in.shape[0]
    t = batch * seq
    alpha = float((2 * depth) ** 0.25)
    tm = INPROJ_TILE
    assert t % tm == 0
    assert w_in.shape[2] == OFF_GATES + 2 * ML_HEADS
    assert depth == 1, "lower-bound cumsum is specialised to a single layer"

    x2 = x.reshape(t, d)
    for i in range(depth):
        w_i = w_in[i]
        w_bf = w_i.astype(BF16)
        b_row = b_in[i].reshape(1, -1)
        wg_t = w_i[:, OFF_GATES:].T
        gate_pad = ((0, GATE_TILE - ML_HEADS), (0, 0))
        wgt = jnp.concatenate([jnp.pad(wg_t[:ML_HEADS], gate_pad), jnp.pad(wg_t[ML_HEADS:], gate_pad)]).astype(BF16)
        bg = b_in[i, OFF_GATES:].reshape(2 * ML_HEADS, 1)
        bgt = jnp.concatenate([jnp.pad(bg[:ML_HEADS], gate_pad), jnp.pad(bg[ML_HEADS:], gate_pad)])

        later = (w_out[i], w_ffn_gate[i], w_ffn_up[i], w_ffn_down[i], ple_w_proj[i], ple_w_gate[i])
        mix_in, (wo, wg, wu, wd, wpp, wpg) = _inproj(x2, w_bf, b_row, wgt, bgt, hg_lb_logits, later, tm)
        tail_consts = (wo, ln1_g[i].reshape(1, d), ln1_b[i].reshape(1, d), wg, wu, wd,
                       ln2_g[i].reshape(1, d), ln2_b[i].reshape(1, d), wpp, wpg, ple_b_gate[i].reshape(1, d))
        x2 = _fused(mix_in, hg_norm_g[i].reshape(1, HG_WIDTH), ml_conv_w[i], ml_conv_b[i].reshape(1, -1),
                    ml_norm_g[i].reshape(1, ML_WIDTH), x2, p[i].reshape(t, -1), tail_consts, batch, seq, alpha)
    return x2.reshape(batch, seq, d)
```
